```python
import jax
import jax.numpy as jnp
from jax import lax
import numpy as np

D_MODEL = 1024
BATCH = 1
SEQ = 16384
DEPTH = 2

GRID_W = 64
CTX_LEN = 256
BRANCH_W = D_MODEL
RET_HEADS = 4
RET_DK = 128
RET_DV = BRANCH_W // RET_HEADS
RET_CHUNK = 128
GQA_HEADS = 8
GQA_KV_HEADS = 2
GQA_DIM = BRANCH_W // GQA_HEADS
Q_BLOCK = 128
NA_HEADS = 16
NA_DIM = BRANCH_W // NA_HEADS
NA_WIN_ROWS = 8
NA_WIN_COLS = 16
RWKV_HEADS = 16
RWKV_DIM = BRANCH_W // RWKV_HEADS
DECAY_LORA = 64
ICLR_LORA = 64
ROPE_THETA = 10000.0
NORM_EPS = 1e-6
RWKV_GN_EPS = 64e-5

EVEN_SPLIT = (RET_HEADS * RET_DK, RET_HEADS * RET_DK, BRANCH_W,
              GQA_HEADS * GQA_DIM, GQA_KV_HEADS * GQA_DIM, GQA_KV_HEADS * GQA_DIM,
              BRANCH_W, BRANCH_W)
RWKV_SPLIT = (BRANCH_W, BRANCH_W, BRANCH_W, DECAY_LORA, DECAY_LORA, ICLR_LORA, ICLR_LORA)
SHIFT_W = sum(RWKV_SPLIT)
ODD_SPLIT = (3 * BRANCH_W, SHIFT_W, BRANCH_W, BRANCH_W)
EVEN_IN = sum(EVEN_SPLIT)
ODD_IN = sum(ODD_SPLIT)

kernel_name = 'hybrid_ret_gqa_natten_rwkv7_dit'


def _split(p, sizes):
    return jnp.split(p, np.cumsum(sizes)[:-1].tolist(), axis=-1)


def rms_norm(x, g):
    xf = x.astype(jnp.float32)
    y = xf * lax.rsqrt(jnp.mean(xf * xf, axis=-1, keepdims=True) + NORM_EPS)
    return (y * g.astype(jnp.float32)).astype(x.dtype)


def head_layer_norm(x, eps):
    xf = x.astype(jnp.float32)
    xc = xf - jnp.mean(xf, axis=-1, keepdims=True)
    return xc * lax.rsqrt(jnp.mean(xc * xc, axis=-1, keepdims=True) + eps)


def axial_rope(n, dim):
    t = jnp.arange(n)
    row = (t // GRID_W).astype(jnp.float32)
    col = (t % GRID_W).astype(jnp.float32)
    n_freq = dim // 4
    inv = ROPE_THETA ** (-jnp.arange(n_freq, dtype=jnp.float32) / n_freq)
    ang = jnp.concatenate([row[:, None] * inv, col[:, None] * inv], axis=-1)
    return jnp.cos(ang), jnp.sin(ang)


def apply_rope(x, cos, sin):
    cos = cos[None, :, None, :].astype(x.dtype)
    sin = sin[None, :, None, :].astype(x.dtype)
    x1, x2 = x[..., 0::2], x[..., 1::2]
    return jnp.stack([x1 * cos - x2 * sin, x1 * sin + x2 * cos], axis=-1).reshape(x.shape)


def block_attention(q, k, v):
    B, Tq, Hq, d = q.shape
    Hkv = k.shape[2]
    qg = (q * d ** -0.5).reshape(B, Tq // Q_BLOCK, Q_BLOCK, Hkv, Hq // Hkv, d)

    def one(qb):
        s = jnp.einsum('bqgrd,bkgd->bgrqk', qb, k).astype(jnp.float32)
        p = jax.nn.softmax(s, axis=-1).astype(v.dtype)
        return jnp.einsum('bgrqk,bkgd->bqgrd', p, v)

    out = lax.map(one, jnp.moveaxis(qg, 1, 0))
    return jnp.moveaxis(out, 0, 1).reshape(B, Tq, Hq, d)


def retention_scan(q, k, v, log_g, s0, emit):
    B, T, H, dk = q.shape
    dv = v.shape[-1]
    n = T // RET_CHUNK
    f32 = jnp.float32
    qc = q.astype(f32).reshape(B, n, RET_CHUNK, H, dk)
    kc = k.astype(f32).reshape(B, n, RET_CHUNK, H, dk)
    vc = v.astype(f32).reshape(B, n, RET_CHUNK, H, dv)
    pos = jnp.arange(RET_CHUNK, dtype=f32)
    zeta = jnp.exp((RET_CHUNK - 1 - pos)[None, :] * log_g[:, None])
    u = jnp.einsum('bnjhd,hj,bnjhe->nbhde', kc, zeta, vc)
    g_chunk = jnp.exp(RET_CHUNK * log_g)[None, :, None, None]

    def step(s, u_i):
        return g_chunk * s + u_i, (s if emit else None)

    s_final, s_prev = lax.scan(step, s0.astype(f32), u)
    if not emit:
        return None, s_final
    diff = pos[:, None] - pos[None, :]
    decay_in = jnp.where(diff >= 0, jnp.exp(jnp.maximum(diff, 0.0)[None] * log_g[:, None, None]), 0.0)
    scores = jnp.einsum('bnihd,bnjhd->bnhij', qc, kc) * decay_in
    inner = jnp.einsum('bnhij,bnjhe->bnihe', scores, vc)
    xi = jnp.exp((pos + 1)[None, :] * log_g[:, None])
    cross = jnp.einsum('bnihd,nbhde,hi->bnihe', qc, s_prev, xi)
    return (inner + cross).reshape(B, T, H, dv), s_final


def bidir_retention(lat, ctx, log_g, need_ctx):
    B = lat[0].shape[0]
    s0 = jnp.zeros((B, RET_HEADS, RET_DK, RET_DV), jnp.float32)
    out_l, out_c = [], []
    for d in range(2):
        rev = (lambda t: jnp.flip(t, axis=1)) if d else (lambda t: t)
        y_c, s_c = retention_scan(*[rev(t) for t in ctx], log_g[d], s0, need_ctx)
        y_l, _ = retention_scan(*[rev(t) for t in lat], log_g[d], s_c, True)
        out_l.append(rev(y_l))
        if need_ctx:
            out_c.append(rev(y_c))
    return out_l[0] + out_l[1], (out_c[0] + out_c[1] if need_ctx else None)


def neighbourhood_attention(q, k, v, k_ctx, v_ctx, rpb):
    B, T, H, d = q.shape
    rows = T // GRID_W
    wh = min(NA_WIN_ROWS, rows)
    ww = NA_WIN_COLS
    qg = (q * d ** -0.5).reshape(B, rows, GRID_W, H, d)
    kg = k.reshape(B, rows, GRID_W, H, d)
    vg = v.reshape(B, rows, GRID_W, H, d)
    cols = np.arange(GRID_W)
    col_idx = np.clip(cols - ww // 2, 0, GRID_W - ww)[:, None] + np.arange(ww)[None, :]
    rpb_c = rpb[:, :, col_idx - cols[:, None] + (NA_WIN_COLS - 1)]

    def one_row(args):
        r, q_row = args
        r0 = jnp.clip(r - wh // 2, 0, rows - wh)
        k_win = lax.dynamic_slice_in_dim(kg, r0, wh, axis=1)[:, :, col_idx]
        v_win = lax.dynamic_slice_in_dim(vg, r0, wh, axis=1)[:, :, col_idx]
        bias = rpb_c[:, r0 + jnp.arange(wh) - r + (NA_WIN_ROWS - 1)]
        s_win = (jnp.einsum('bqhd,biqjhd->bhqij', q_row, k_win).astype(jnp.float32)
                 + jnp.transpose(bias, (0, 2, 1, 3)).astype(jnp.float32))
        s_ctx = jnp.einsum('bqhd,bkhd->bhqk', q_row, k_ctx).astype(jnp.float32)
        p = jax.nn.softmax(jnp.concatenate([s_win.reshape(B, H, GRID_W, wh * ww), s_ctx], axis=-1),
                           axis=-1).astype(v.dtype)
        p_win = p[..., :wh * ww].reshape(B, H, GRID_W, wh, ww)
        return (jnp.einsum('bhqij,biqjhd->bqhd', p_win, v_win)
                + jnp.einsum('bhqk,bkhd->bqhd', p[..., wh * ww:], v_ctx))

    out = lax.map(one_row, (jnp.arange(rows), jnp.moveaxis(qg, 1, 0)))
    return jnp.moveaxis(out, 0, 1).reshape(B, T, H, d)


def centred_token_shift(z, mu):
    zeros = jnp.zeros_like(z[:, :1])
    prev = jnp.concatenate([zeros, z[:, :-1]], axis=1)
    nxt = jnp.concatenate([z[:, 1:], zeros], axis=1)
    return z + (0.5 * (prev + nxt) - z) * mu


def rwkv_direction(parts, d, w0, w2, a0, a2, k_k, k_a):
    r, k, v = parts[0], parts[1], parts[2]
    zw, za = parts[3 + d], parts[5 + d]
    B, n, _ = r.shape
    heads = lambda t: t.reshape(B, n, RWKV_HEADS, RWKV_DIM)
    log_w = -jax.nn.softplus(-(w0 + jnp.tanh(zw) @ w2).astype(jnp.float32)) - 0.5
    decay = jnp.exp(-jnp.exp(log_w))
    a = jax.nn.sigmoid(a0 + za @ a2)
    kk = heads(k * k_k).astype(jnp.float32)
    kk = kk / jnp.maximum(jnp.linalg.norm(kk, axis=-1, keepdims=True), 1e-12)
    kd = k * (1.0 + (a - 1.0) * k_a)
    return heads(r), heads(decay), heads(kd), heads(v), -kk, kk * heads(a).astype(jnp.float32)


def rwkv7_scan(r, w, k, v, a, b, s0, emit):
    xs = tuple(jnp.moveaxis(t.astype(jnp.float32), 1, 0) for t in (r, w, k, v, a, b))

    def step(s, inp):
        rt, wt, kt, vt, at, bt = inp
        sa = jnp.einsum('bhij,bhj->bhi', s, at)
        s = s * wt[:, :, None, :] + sa[..., None] * bt[:, :, None, :] + vt[..., None] * kt[:, :, None, :]
        return s, (jnp.einsum('bhij,bhj->bhi', s, rt) if emit else None)

    s_final, ys = lax.scan(step, s0, xs)
    return (jnp.moveaxis(ys, 0, 1) if emit else None), s_final


def rwkv_bonus(ins, r_k):
    r, _, kd, v, _, _ = ins
    return jnp.sum(r * kd * r_k, axis=-1, keepdims=True) * v


def rwkv_branch(z, zc, w0s, w2s, a0s, a2s, k_k, k_a, r_k, lnx_g, lnx_b, need_ctx):
    B = z.shape[0]
    parts, parts_c = _split(z, RWKV_SPLIT), _split(zc, RWKV_SPLIT)
    s0 = jnp.zeros((B, RWKV_HEADS, RWKV_DIM, RWKV_DIM), jnp.float32)
    wkv_l, bon_l, wkv_c, bon_c = [], [], [], []
    for d in range(2):
        rev = (lambda t: jnp.flip(t, axis=1)) if d else (lambda t: t)
        ins_l = rwkv_direction(parts, d, w0s[d], w2s[d], a0s[d], a2s[d], k_k, k_a)
        ins_c = rwkv_direction(parts_c, d, w0s[d], w2s[d], a0s[d], a2s[d], k_k, k_a)
        y_c, s_c = rwkv7_scan(*[rev(t) for t in ins_c], s0, need_ctx)
        y_l, _ = rwkv7_scan(*[rev(t) for t in ins_l], s_c, True)
        wkv_l.append(rev(y_l))
        bon_l.append(rwkv_bonus(ins_l, r_k))
        if need_ctx:
            wkv_c.append(rev(y_c))
            bon_c.append(rwkv_bonus(ins_c, r_k))

    def finish(wkv, bon):
        y = (head_layer_norm(wkv[0] + wkv[1], RWKV_GN_EPS) * lnx_g.reshape(RWKV_HEADS, RWKV_DIM)
             + lnx_b.reshape(RWKV_HEADS, RWKV_DIM))
        return y + bon[0] + bon[1]

    return finish(wkv_l, bon_l), (finish(wkv_c, bon_c) if need_ctx else None)


def even_mixer(h, hc, w_in, w_out, dec_f, dec_b, q_norm, k_norm, need_ctx):
    B, T, _ = h.shape

    def project(hs):
        n = hs.shape[1]
        qa, ka, va, qb, kb, vb, ga, gb = _split(hs @ w_in, EVEN_SPLIT)
        return (qa.reshape(B, n, RET_HEADS, RET_DK),
                ka.reshape(B, n, RET_HEADS, RET_DK) * RET_DK ** -0.5,
                va.reshape(B, n, RET_HEADS, RET_DV),
                rms_norm(qb.reshape(B, n, GQA_HEADS, GQA_DIM), q_norm),
                rms_norm(kb.reshape(B, n, GQA_KV_HEADS, GQA_DIM), k_norm),
                vb.reshape(B, n, GQA_KV_HEADS, GQA_DIM), ga, gb)

    qa, ka, va, qb, kb, vb, ga, gb = project(h)
    qa_c, ka_c, va_c, qb_c, kb_c, vb_c, ga_c, gb_c = project(hc)
    cos_a, sin_a = axial_rope(T, RET_DK)
    cos_b, sin_b = axial_rope(T, GQA_DIM)
    qa, ka = apply_rope(qa, cos_a, sin_a), apply_rope(ka, cos_a, sin_a)
    qb, kb = apply_rope(qb, cos_b, sin_b), apply_rope(kb, cos_b, sin_b)
    log_g = (jax.nn.log_sigmoid(dec_f.astype(jnp.float32)), jax.nn.log_sigmoid(dec_b.astype(jnp.float32)))
    ret, ret_c = bidir_retention((qa, ka, va), (qa_c, ka_c, va_c), log_g, need_ctx)
    att = block_attention(qb, jnp.concatenate([kb_c, kb], axis=1), jnp.concatenate([vb_c, vb], axis=1))

    def merge(ret_o, att_o, g_a, g_b):
        n = ret_o.shape[1]
        return jnp.concatenate([jax.nn.silu(g_a) * head_layer_norm(ret_o, NORM_EPS).reshape(B, n, BRANCH_W),
                                jax.nn.silu(g_b) * att_o.reshape(B, n, BRANCH_W)], axis=-1) @ w_out

    y = merge(ret, att, ga, gb)
    if not need_ctx:
        return y, None
    return y, merge(ret_c, block_attention(qb_c, kb_c, vb_c), ga_c, gb_c)


def odd_mixer(h, hc, w_in, w_out, rpb, shift_mu, w0s, w2s, a0s, a2s, k_k, k_a, r_k, lnx_g, lnx_b, need_ctx):
    B, T, _ = h.shape

    def project(hs):
        n = hs.shape[1]
        na_p, rw_p, g_c, g_d = _split(hs @ w_in, ODD_SPLIT)
        q, k, v = [t.reshape(B, n, NA_HEADS, NA_DIM) for t in _split(na_p, (BRANCH_W,) * 3)]
        return q, k, v, centred_token_shift(rw_p, shift_mu), g_c, g_d

    q, k, v, z, g_c, g_d = project(h)
    q_cx, k_cx, v_cx, z_cx, g_c_cx, g_d_cx = project(hc)
    na = neighbourhood_attention(q, k, v, k_cx, v_cx, rpb)
    rw, rw_cx = rwkv_branch(z, z_cx, w0s, w2s, a0s, a2s, k_k, k_a, r_k, lnx_g, lnx_b, need_ctx)

    def merge(na_o, rw_o, gc, gd):
        n = na_o.shape[1]
        return jnp.concatenate([jax.nn.silu(gc) * na_o.reshape(B, n, BRANCH_W),
                                jax.nn.silu(gd) * rw_o.reshape(B, n, BRANCH_W)], axis=-1) @ w_out

    y = merge(na, rw, g_c, g_d)
    if not need_ctx:
        return y, None
    return y, merge(block_attention(q_cx, k_cx, v_cx), rw_cx, g_c_cx, g_d_cx)


def setup_inputs(seed: int = 0) -> dict:
    key = jax.random.key(seed)
    keys = iter(jax.random.split(key, 48))
    n_even = (DEPTH + 1) // 2
    n_odd = DEPTH // 2
    f32 = jnp.float32

    def nrm(shape, std):
        return std * jax.random.normal(next(keys), shape, f32)

    def near_one(shape):
        return 1.0 + nrm(shape, 0.02)

    gamma0 = 1.0 - 2.0 ** (-5.0 - np.arange(RET_HEADS))
    logit0 = jnp.asarray(np.log(gamma0 / (1.0 - gamma0)), f32)
    w0_base = -6.0 + 5.0 * jnp.linspace(0.0, 1.0, BRANCH_W, dtype=f32)
    return {
        'x': nrm((BATCH, SEQ, D_MODEL), 1.0),
        'c': nrm((BATCH, D_MODEL), 1.0),
        'ctx': nrm((BATCH, CTX_LEN, D_MODEL), 1.0),
        'c_ctx': nrm((D_MODEL,), 1.0),
        'w_mod': nrm((DEPTH, D_MODEL, 3 * D_MODEL), 0.5 * D_MODEL ** -0.5),
        'b_mod': nrm((DEPTH, 3 * D_MODEL), 0.02),
        'g_pre': near_one((DEPTH, D_MODEL)),
        'g_post': near_one((DEPTH, D_MODEL)),
        'ev_w_in': nrm((n_even, D_MODEL, EVEN_IN), D_MODEL ** -0.5),
        'ev_w_out': nrm((n_even, 2 * BRANCH_W, D_MODEL), (2 * BRANCH_W) ** -0.5),
        'ret_decay_fwd': logit0 + nrm((n_even, RET_HEADS), 0.1),
        'ret_decay_bwd': logit0 + nrm((n_even, RET_HEADS), 0.1),
        'gqa_q_norm': near_one((n_even, GQA_DIM)),
        'gqa_k_norm': near_one((n_even, GQA_DIM)),
        'od_w_in': nrm((n_odd, D_MODEL, ODD_IN), D_MODEL ** -0.5),
        'od_w_out': nrm((n_odd, 2 * BRANCH_W, D_MODEL), (2 * BRANCH_W) ** -0.5),
        'na_rpb': nrm((n_odd, NA_HEADS, 2 * NA_WIN_ROWS - 1, 2 * NA_WIN_COLS - 1), 0.1),
        'rwkv_shift_mu': jax.random.uniform(next(keys), (n_odd, SHIFT_W), f32),
        'rwkv_w0_fwd': w0_base + nrm((n_odd, BRANCH_W), 0.1),
        'rwkv_w2_fwd': nrm((n_odd, DECAY_LORA, BRANCH_W), 0.1),
        'rwkv_w0_bwd': w0_base + nrm((n_odd, BRANCH_W), 0.1),
        'rwkv_w2_bwd': nrm((n_odd, DECAY_LORA, BRANCH_W), 0.1),
        'rwkv_a0_fwd': nrm((n_odd, BRANCH_W), 0.1),
        'rwkv_a2_fwd': nrm((n_odd, ICLR_LORA, BRANCH_W), 0.1),
        'rwkv_a0_bwd': nrm((n_odd, BRANCH_W), 0.1),
        'rwkv_a2_bwd': nrm((n_odd, ICLR_LORA, BRANCH_W), 0.1),
        'rwkv_k_k': 0.85 + nrm((n_odd, BRANCH_W), 0.02),
        'rwkv_k_a': near_one((n_odd, BRANCH_W)),
        'rwkv_r_k': nrm((n_odd, RWKV_HEADS, RWKV_DIM), 0.1),
        'rwkv_lnx_g': near_one((n_odd, BRANCH_W)),
        'rwkv_lnx_b': nrm((n_odd, BRANCH_W), 0.02),
    }


def reference(x, c, ctx, c_ctx, w_mod, b_mod, g_pre, g_post, ev_w_in, ev_w_out, ret_decay_fwd, ret_decay_bwd,
              gqa_q_norm, gqa_k_norm, od_w_in, od_w_out, na_rpb, rwkv_shift_mu, rwkv_w0_fwd, rwkv_w2_fwd,
              rwkv_w0_bwd, rwkv_w2_bwd, rwkv_a0_fwd, rwkv_a2_fwd, rwkv_a0_bwd, rwkv_a2_bwd, rwkv_k_k, rwkv_k_a,
              rwkv_r_k, rwkv_lnx_g, rwkv_lnx_b):
    for layer in range(DEPTH):
        need_ctx = layer < DEPTH - 1
        i = layer // 2
        mod = jax.nn.silu(c) @ w_mod[layer] + b_mod[layer]
        mod_c = jax.nn.silu(c_ctx) @ w_mod[layer] + b_mod[layer]
        shift, scale, gate = jnp.split(mod[:, None, :], 3, axis=-1)
        shift_c, scale_c, gate_c = jnp.split(mod_c, 3, axis=-1)
        h = rms_norm(x, g_pre[layer]) * (1.0 + scale) + shift
        hc = rms_norm(ctx, g_pre[layer]) * (1.0 + scale_c) + shift_c
        if layer % 2 == 0:
            y, y_c = even_mixer(h, hc, ev_w_in[i], ev_w_out[i], ret_decay_fwd[i], ret_decay_bwd[i],
                                gqa_q_norm[i], gqa_k_norm[i], need_ctx)
        else:
            y, y_c = odd_mixer(h, hc, od_w_in[i], od_w_out[i], na_rpb[i], rwkv_shift_mu[i],
                               (rwkv_w0_fwd[i], rwkv_w0_bwd[i]), (rwkv_w2_fwd[i], rwkv_w2_bwd[i]),
                               (rwkv_a0_fwd[i], rwkv_a0_bwd[i]), (rwkv_a2_fwd[i], rwkv_a2_bwd[i]),
                               rwkv_k_k[i], rwkv_k_a[i], rwkv_r_k[i], rwkv_lnx_g[i], rwkv_lnx_b[i], need_ctx)
        x = x + gate * rms_norm(y, g_post[layer])
        if need_ctx:
            ctx = ctx + gate_c * rms_norm(y_c, g_post[layer])
    return x
```

```python
import functools

import jax
import jax.numpy as jnp
import numpy as np
from jax import lax
from jax.experimental import pallas as pl
from jax.experimental.pallas import tpu as pltpu

F32 = jnp.float32
BF16 = jnp.bfloat16

D_MODEL = 1024
DEPTH = 2
GRID_W = 64
CTX_LEN = 256
BRANCH_W = D_MODEL
RET_HEADS = 4
RET_DK = 128
RET_DV = BRANCH_W // RET_HEADS
RET_CHUNK = 128
GQA_HEADS = 8
GQA_KV_HEADS = 2
GQA_DIM = BRANCH_W // GQA_HEADS
GQA_GROUP = GQA_HEADS // GQA_KV_HEADS
NA_HEADS = 16
NA_DIM = BRANCH_W // NA_HEADS
NA_WIN_ROWS = 8
NA_WIN_COLS = 16
RWKV_HEADS = 16
RWKV_DIM = BRANCH_W // RWKV_HEADS
DECAY_LORA = 64
ICLR_LORA = 64
ROPE_THETA = 10000.0
NORM_EPS = 1e-6
RWKV_GN_EPS = 64e-5
SHIFT_W = 3 * BRANCH_W + 2 * DECAY_LORA + 2 * ICLR_LORA
EVEN_IN = 2 * RET_HEADS * RET_DK + BRANCH_W + GQA_HEADS * GQA_DIM + 2 * GQA_KV_HEADS * GQA_DIM + 2 * BRANCH_W
ODD_IN = 3 * BRANCH_W + SHIFT_W + 2 * BRANCH_W

V7X_LANES = 128
V7X_SUBLANES = 8
V7X_VMEM_BYTES = 64 * 1024 * 1024

ROW_BLOCK = CTX_LEN
RWKV_CHUNK = 64
ATT_Q_BLOCK = 256
ATT_KV_BLOCK = 640
NA_ROWS_PER_STEP = 4
MASK_VALUE = -1e30


def _vmem_limit(nbytes):
    return int(min(V7X_VMEM_BYTES - 4 * 1024 * 1024, max(32 * 1024 * 1024, nbytes)))


def _cparams(sem, vmem_bytes):
    return pltpu.CompilerParams(dimension_semantics=sem, vmem_limit_bytes=_vmem_limit(vmem_bytes))


def _silu(x):
    return x / (1.0 + jnp.exp(-x))


def _sigmoid(x):
    return 1.0 / (1.0 + jnp.exp(-x))


def _softplus(x):
    return jnp.maximum(x, 0.0) + jnp.log(1.0 + jnp.exp(-jnp.abs(x)))


def _dot(a, b):
    return jnp.dot(a, b, preferred_element_type=F32)


def _dot_nt(a, b):
    return lax.dot_general(a, b, (((1,), (1,)), ((), ())), preferred_element_type=F32)


def _split3(x):
    hi = x.astype(BF16)
    r1 = x - hi.astype(F32)
    mid = r1.astype(BF16)
    lo = (r1 - mid.astype(F32)).astype(BF16)
    return hi, mid, lo


def _dot_exact_lhs(a_bf16, x):
    hi, mid, lo = _split3(x)
    return _dot(a_bf16, hi) + _dot(a_bf16, mid) + _dot(a_bf16, lo)


def _dot_exact_rhs(x, b_bf16):
    hi, mid, lo = _split3(x)
    return _dot(hi, b_bf16) + _dot(mid, b_bf16) + _dot(lo, b_bf16)


def _rms_rows(x):
    return x * lax.rsqrt(jnp.mean(x * x, axis=-1, keepdims=True) + NORM_EPS)


def _rope(t, cos, sin_signed, even):
    nxt = pltpu.roll(t, t.shape[1] - 1, 1)
    prv = pltpu.roll(t, 1, 1)
    return t * cos + jnp.where(even, nxt, prv) * sin_signed


def _mod_kernel(cc_ref, w_ref, b_ref, o_ref):
    s = _silu(cc_ref[...])
    o_ref[0] = jnp.dot(s, w_ref[0], preferred_element_type=F32, precision=lax.Precision.HIGHEST) + b_ref[0]


def _modulation(c, c_ctx, w_mod, b_mod):
    d = c.shape[-1]
    cc = jnp.concatenate([c[:1], c_ctx[None, :], jnp.zeros((V7X_SUBLANES - 2, d), F32)], axis=0)
    return pl.pallas_call(
        _mod_kernel,
        grid=(DEPTH, 3),
        in_specs=[pl.BlockSpec((V7X_SUBLANES, d), lambda l, j: (0, 0)),
                  pl.BlockSpec((1, d, d), lambda l, j: (l, 0, j)),
                  pl.BlockSpec((1, 1, d), lambda l, j: (l, 0, j))],
        out_specs=pl.BlockSpec((1, V7X_SUBLANES, d), lambda l, j: (l, 0, j)),
        out_shape=jax.ShapeDtypeStruct((DEPTH, V7X_SUBLANES, 3 * d), F32),
        compiler_params=_cparams(("arbitrary", "arbitrary"), 40 * 1024 * 1024),
        name="modulation",
    )(cc, w_mod, b_mod.reshape(DEPTH, 1, 3 * d))


def _adaln(xb, mod, is_ctx, g_pre):
    d = xb.shape[-1]
    m = jnp.where(is_ctx, mod[1:2, :], mod[0:1, :])
    shift, scale = m[:, :d], m[:, d:2 * d]
    return (_rms_rows(xb) * g_pre) * (1.0 + scale) + shift


def _even_in_kernel(x_ref, ctx_ref, mod_ref, gpre_ref, w_ref, cos_ref, sin_ref, qn_ref, kn_ref,
                    qa_ref, ka_ref, va_ref, qb_ref, kbt_ref, vb_ref, ga_ref, gb_ref):
    is_ctx = pl.program_id(0) == 0
    xb = jnp.where(is_ctx, ctx_ref[...], x_ref[...])
    hb = _adaln(xb, mod_ref[0], is_ctx, gpre_ref[...]).astype(BF16)
    cos, sin_s = cos_ref[...], sin_ref[...]
    even = (lax.broadcasted_iota(jnp.int32, cos.shape, 1) & 1) == 0
    o = 0

    def seg(width):
        nonlocal o
        y = _dot(hb, w_ref[:, o:o + width])
        o += width
        return y

    y = seg(RET_HEADS * RET_DK)
    for h in range(RET_HEADS):
        sl = slice(h * RET_DK, (h + 1) * RET_DK)
        qa_ref[:, sl] = _rope(y[:, sl], cos, sin_s, even).astype(BF16)
    y = seg(RET_HEADS * RET_DK)
    for h in range(RET_HEADS):
        sl = slice(h * RET_DK, (h + 1) * RET_DK)
        ka_ref[:, sl] = _rope(y[:, sl] * RET_DK ** -0.5, cos, sin_s, even).astype(BF16)
    va_ref[...] = seg(BRANCH_W).astype(BF16)
    y = seg(GQA_HEADS * GQA_DIM)
    for h in range(GQA_HEADS):
        sl = slice(h * GQA_DIM, (h + 1) * GQA_DIM)
        t = _rms_rows(y[:, sl]) * qn_ref[...]
        qb_ref[:, sl] = (_rope(t, cos, sin_s, even) * GQA_DIM ** -0.5).astype(BF16)
    y = seg(GQA_KV_HEADS * GQA_DIM)
    for h in range(GQA_KV_HEADS):
        sl = slice(h * GQA_DIM, (h + 1) * GQA_DIM)
        t = _rope(_rms_rows(y[:, sl]) * kn_ref[...], cos, sin_s, even)
        kbt_ref[sl, :] = t.T.astype(BF16)
    vb_ref[...] = seg(GQA_KV_HEADS * GQA_DIM).astype(BF16)
    ga_ref[...] = _silu(seg(BRANCH_W)).astype(BF16)
    gb_ref[...] = _silu(seg(BRANCH_W)).astype(BF16)


def _even_in_proj(x, ctx, mod, g_pre, w_in, cos_t, sin_t, q_norm, k_norm):
    t, d = x.shape
    tt = t + CTX_LEN
    nblk = tt // ROW_BLOCK
    tm = ROW_BLOCK
    kvw = GQA_KV_HEADS * GQA_DIM
    row = lambda w: pl.BlockSpec((tm, w), lambda i: (i, 0))
    const = lambda shape: pl.BlockSpec(shape, lambda i: tuple(0 for _ in shape))
    outs = [((tt, RET_HEADS * RET_DK), row(RET_HEADS * RET_DK)),
            ((tt, RET_HEADS * RET_DK), row(RET_HEADS * RET_DK)),
            ((tt, BRANCH_W), row(BRANCH_W)),
            ((tt, GQA_HEADS * GQA_DIM), row(GQA_HEADS * GQA_DIM)),
            ((kvw, tt), pl.BlockSpec((kvw, tm), lambda i: (0, i))),
            ((tt, kvw), row(kvw)),
            ((tt, BRANCH_W), row(BRANCH_W)),
            ((tt, BRANCH_W), row(BRANCH_W))]
    return pl.pallas_call(
        _even_in_kernel,
        grid=(nblk,),
        in_specs=[pl.BlockSpec((tm, d), lambda i: (jnp.maximum(i - 1, 0), 0)),
                  const((CTX_LEN, d)),
                  pl.BlockSpec((1, V7X_SUBLANES, 3 * d), lambda i: (0, 0, 0)),
                  const((1, d)),
                  const((d, EVEN_IN)),
                  row(RET_DK), row(RET_DK),
                  const((1, GQA_DIM)), const((1, GQA_DIM))],
        out_specs=[s for _, s in outs],
        out_shape=[jax.ShapeDtypeStruct(shp, BF16) for shp, _ in outs],
        compiler_params=_cparams(("arbitrary",), 2 * d * EVEN_IN * 2 + 16 * 1024 * 1024),
        name="even_in_proj",
    )(x, ctx, mod, g_pre.reshape(1, d), w_in, cos_t, sin_t, q_norm.reshape(1, -1), k_norm.reshape(1, -1))


def _log_sigmoid(x):
    return jnp.minimum(x, 0.0) - jnp.log(1.0 + jnp.exp(-jnp.abs(x)))


def _ret_bwd_chunk(t, nchunks):
    nctx = CTX_LEN // RET_CHUNK
    return jnp.where(t < nctx, nctx - 1 - t, nchunks - 1 - (t - nctx))


def _ret_state_kernel(dec_ref, k_ref, v_ref, sb_ref, s_scr):
    c = RET_CHUNK

    @pl.when(pl.program_id(0) == 0)
    def _():
        s_scr[...] = jnp.zeros_like(s_scr)

    lg = _log_sigmoid(dec_ref[...])
    pos = lax.broadcasted_iota(jnp.int32, (c, RET_DK), 0).astype(F32)
    for h in range(RET_HEADS):
        lgb = lg[RET_HEADS + h:RET_HEADS + h + 1, :]
        s_old = s_scr[h]
        sb_ref[0, h] = s_old.astype(BF16)
        kz = k_ref[:, h * RET_DK:(h + 1) * RET_DK].astype(F32) * jnp.exp(pos * lgb)
        u = _dot(kz.T.astype(BF16), v_ref[:, h * RET_DV:(h + 1) * RET_DV])
        s_scr[h] = jnp.exp(c * lgb[:, :1]) * s_old + u


def _ret_out_kernel(dec_ref, q_ref, k_ref, v_ref, g_ref, sb_ref, o_ref, s_scr):
    c = RET_CHUNK

    @pl.when(pl.program_id(0) == 0)
    def _():
        s_scr[...] = jnp.zeros_like(s_scr)

    lg = _log_sigmoid(dec_ref[...])
    ii = lax.broadcasted_iota(jnp.int32, (c, c), 0)
    jj = lax.broadcasted_iota(jnp.int32, (c, c), 1)
    dlt = (ii - jj).astype(F32)
    pos = lax.broadcasted_iota(jnp.int32, (c, RET_DK), 0).astype(F32)
    for h in range(RET_HEADS):
        lgf = lg[h:h + 1, :]
        lgb = lg[RET_HEADS + h:RET_HEADS + h + 1, :]
        dec = jnp.where(dlt > 0, jnp.exp(jnp.maximum(dlt, 0.0) * lgf),
                        jnp.where(dlt < 0, jnp.exp(jnp.maximum(-dlt, 0.0) * lgb), 2.0))
        q = q_ref[:, h * RET_DK:(h + 1) * RET_DK]
        k = k_ref[:, h * RET_DK:(h + 1) * RET_DK]
        v = v_ref[:, h * RET_DV:(h + 1) * RET_DV]
        qf = q.astype(F32)
        p = (_dot_nt(q, k) * dec).astype(BF16)
        s_old = s_scr[h]
        ret = (_dot(p, v)
               + _dot((qf * jnp.exp((pos + 1.0) * lgf)).astype(BF16), s_old.astype(BF16))
               + _dot((qf * jnp.exp((c - pos) * lgb)).astype(BF16), sb_ref[0, h]))
        xc = ret - jnp.mean(ret, axis=-1, keepdims=True)
        y = xc * lax.rsqrt(jnp.mean(xc * xc, axis=-1, keepdims=True) + NORM_EPS)
        sl = slice(h * RET_DV, (h + 1) * RET_DV)
        o_ref[:, sl] = (g_ref[:, sl].astype(F32) * y).astype(BF16)
        kz = k.astype(F32) * jnp.exp((c - 1.0 - pos) * lgf)
        s_scr[h] = jnp.exp(c * lgf[:, :1]) * s_old + _dot(kz.T.astype(BF16), v)


def _retention(qa, ka, va, ga, dec_f, dec_b):
    tt = qa.shape[0]
    c = RET_CHUNK
    n = tt // c
    dec = jnp.broadcast_to(jnp.concatenate([dec_f, dec_b]).astype(F32)[:, None], (2 * RET_HEADS, V7X_LANES))
    kw, vw = RET_HEADS * RET_DK, BRANCH_W
    dec_spec = pl.BlockSpec((2 * RET_HEADS, V7X_LANES), lambda t: (0, 0))
    state_shape = (RET_HEADS, RET_DK, RET_DV)
    sb = pl.pallas_call(
        _ret_state_kernel,
        grid=(n,),
        in_specs=[dec_spec,
                  pl.BlockSpec((c, kw), lambda t: (_ret_bwd_chunk(t, n), 0)),
                  pl.BlockSpec((c, vw), lambda t: (_ret_bwd_chunk(t, n), 0))],
        out_specs=pl.BlockSpec((1,) + state_shape, lambda t: (_ret_bwd_chunk(t, n), 0, 0, 0)),
        out_shape=jax.ShapeDtypeStruct((n,) + state_shape, BF16),
        scratch_shapes=[pltpu.VMEM(state_shape, F32)],
        compiler_params=_cparams(("arbitrary",), 32 * 1024 * 1024),
        name="retention_state",
    )(dec, ka, va)
    return pl.pallas_call(
        _ret_out_kernel,
        grid=(n,),
        in_specs=[dec_spec,
                  pl.BlockSpec((c, kw), lambda t: (t, 0)),
                  pl.BlockSpec((c, kw), lambda t: (t, 0)),
                  pl.BlockSpec((c, vw), lambda t: (t, 0)),
                  pl.BlockSpec((c, vw), lambda t: (t, 0)),
                  pl.BlockSpec((1,) + state_shape, lambda t: (t, 0, 0, 0))],
        out_specs=pl.BlockSpec((c, vw), lambda t: (t, 0)),
        out_shape=jax.ShapeDtypeStruct((tt, vw), BF16),
        scratch_shapes=[pltpu.VMEM(state_shape, F32)],
        compiler_params=_cparams(("arbitrary",), 32 * 1024 * 1024),
        name="retention_out",
    )(dec, qa, ka, va, ga, sb)


def _gqa_kernel(q_ref, kt_ref, v_ref, g_ref, o_ref, *, kv_block, n_kv):
    tq = q_ref.shape[0]
    for h in range(GQA_GROUP):
        sl = slice(h * GQA_DIM, (h + 1) * GQA_DIM)
        q = q_ref[:, sl]

        def body(j, carry):
            m, l, acc = carry
            start = pl.multiple_of(j * kv_block, V7X_LANES)
            s = _dot(q, kt_ref[:, pl.ds(start, kv_block)])
            m_new = jnp.maximum(m, jnp.max(s, axis=-1, keepdims=True))
            alpha = jnp.exp(m - m_new)
            p = jnp.exp(s - m_new)
            l = alpha * l + jnp.sum(p, axis=-1, keepdims=True)
            acc = alpha * acc + _dot(p.astype(BF16), v_ref[pl.ds(start, kv_block), :])
            return m_new, l, acc

        init = (jnp.full((tq, 1), MASK_VALUE, F32), jnp.zeros((tq, 1), F32), jnp.zeros((tq, GQA_DIM), F32))
        _, l, acc = lax.fori_loop(0, n_kv, body, init)
        o_ref[:, sl] = (g_ref[:, sl].astype(F32) * (acc / l)).astype(BF16)


def _gqa_attention(qb, kbt, vb, gb, *, q_row0, n_q, n_keys, kv_block, out_rows):
    gw = GQA_GROUP * GQA_DIM
    tq = ATT_Q_BLOCK
    qoff = q_row0 // tq
    return pl.pallas_call(
        functools.partial(_gqa_kernel, kv_block=kv_block, n_kv=n_keys // kv_block),
        grid=(GQA_KV_HEADS, n_q // tq),
        in_specs=[pl.BlockSpec((tq, gw), lambda g, i: (qoff + i, g)),
                  pl.BlockSpec((GQA_DIM, n_keys), lambda g, i: (g, 0)),
                  pl.BlockSpec((n_keys, GQA_DIM), lambda g, i: (0, g)),
                  pl.BlockSpec((tq, gw), lambda g, i: (qoff + i, g))],
        out_specs=pl.BlockSpec((tq, gw), lambda g, i: (i, g)),
        out_shape=jax.ShapeDtypeStruct((out_rows, GQA_HEADS * GQA_DIM), BF16),
        compiler_params=_cparams(("arbitrary", "arbitrary"), 48 * 1024 * 1024),
        name="gqa_attention",
    )(qb, kbt, vb, gb)


def _out_proj_kernel(m1_ref, m2_ref, w_ref, res_ref, mod_ref, gpost_ref, o_ref, *, ctx_blocks):
    d = o_ref.shape[-1]
    is_ctx = pl.program_id(0) < ctx_blocks
    y = _dot(m1_ref[...], w_ref[:BRANCH_W, :]) + _dot(m2_ref[...], w_ref[BRANCH_W:, :])
    mod = mod_ref[0]
    gate = jnp.where(is_ctx, mod[1:2, 2 * d:], mod[0:1, 2 * d:])
    o_ref[...] = res_ref[...] + gate * (_rms_rows(y) * gpost_ref[...])


def _out_proj(m1, m2, w_out, res, mod, layer, g_post, *, ctx_blocks):
    rows, d = res.shape
    tm = ROW_BLOCK
    row = lambda w: pl.BlockSpec((tm, w), lambda i: (i, 0))
    return pl.pallas_call(
        functools.partial(_out_proj_kernel, ctx_blocks=ctx_blocks),
        grid=(rows // tm,),
        in_specs=[row(BRANCH_W), row(BRANCH_W),
                  pl.BlockSpec((2 * BRANCH_W, d), lambda i: (0, 0)),
                  row(d),
                  pl.BlockSpec((1, V7X_SUBLANES, 3 * d), lambda i: (layer, 0, 0)),
                  pl.BlockSpec((1, d), lambda i: (0, 0))],
        out_specs=row(d),
        out_shape=jax.ShapeDtypeStruct((rows, d), F32),
        compiler_params=_cparams(("arbitrary",), 40 * 1024 * 1024),
        name="out_proj",
    )(m1, m2, w_out, res, mod, g_post.reshape(1, d))


def _odd_in_kernel(s_ref, mod_ref, gpre_ref, w_ref, q_ref, k_ref, v_ref, rw_ref, gc_ref, gd_ref):
    is_ctx = pl.program_id(0) == 0
    hb = _adaln(s_ref[...], mod_ref[0], is_ctx, gpre_ref[...]).astype(BF16)
    o = 0

    def seg(width):
        nonlocal o
        y = _dot(hb, w_ref[:, o:o + width])
        o += width
        return y

    q_ref[...] = (seg(BRANCH_W) * NA_DIM ** -0.5).astype(BF16)
    k_ref[...] = seg(BRANCH_W).astype(BF16)
    v_ref[...] = seg(BRANCH_W).astype(BF16)
    rw_ref[...] = seg(SHIFT_W)
    gc_ref[...] = _silu(seg(BRANCH_W)).astype(BF16)
    gd_ref[...] = _silu(seg(BRANCH_W)).astype(BF16)


def _odd_in_proj(stream, mod, g_pre, w_in):
    tt, d = stream.shape
    tm = ROW_BLOCK
    row = lambda w: pl.BlockSpec((tm, w), lambda i: (i, 0))
    widths = [(BRANCH_W, BF16), (BRANCH_W, BF16), (BRANCH_W, BF16), (SHIFT_W, F32), (BRANCH_W, BF16), (BRANCH_W, BF16)]
    return pl.pallas_call(
        _odd_in_kernel,
        grid=(tt // tm,),
        in_specs=[row(d),
                  pl.BlockSpec((1, V7X_SUBLANES, 3 * d), lambda i: (1, 0, 0)),
                  pl.BlockSpec((1, d), lambda i: (0, 0)),
                  pl.BlockSpec((d, ODD_IN), lambda i: (0, 0), pipeline_mode=pl.Buffered(1))],
        out_specs=[row(w) for w, _ in widths],
        out_shape=[jax.ShapeDtypeStruct((tt, w), dt) for w, dt in widths],
        compiler_params=_cparams(("arbitrary",), d * ODD_IN * 2 + 24 * 1024 * 1024),
        name="odd_in_proj",
    )(stream, mod, g_pre.reshape(1, d), w_in)


def _na_kernel(q_ref, k_ref, v_ref, bias_ref, g_ref, o_ref, *, rows):
    win = NA_WIN_ROWS * GRID_W
    lane = lax.broadcasted_iota(jnp.int32, (GRID_W, 2 * NA_DIM), 1)
    first = lane < NA_DIM
    kc, vc = k_ref[:CTX_LEN, :], v_ref[:CTX_LEN, :]
    for a in range(NA_ROWS_PER_STEP):
        qr = pl.program_id(1) * NA_ROWS_PER_STEP + a
        r0 = jnp.clip(qr - NA_WIN_ROWS // 2, 0, rows - NA_WIN_ROWS)
        cls = qr - r0
        start = pl.multiple_of(CTX_LEN + r0 * GRID_W, GRID_W)
        kw, vw = k_ref[pl.ds(start, win), :], v_ref[pl.ds(start, win), :]
        q2 = q_ref[a * GRID_W:(a + 1) * GRID_W, :]
        outs = []
        for h in range(2):
            qh = jnp.where(first if h == 0 else jnp.logical_not(first), q2, jnp.zeros_like(q2))
            sw = _dot_nt(qh, kw) + bias_ref[h, cls]
            sc = _dot_nt(qh, kc)
            m = jnp.maximum(jnp.max(sw, axis=-1, keepdims=True), jnp.max(sc, axis=-1, keepdims=True))
            pw, pc = jnp.exp(sw - m), jnp.exp(sc - m)
            l = jnp.sum(pw, axis=-1, keepdims=True) + jnp.sum(pc, axis=-1, keepdims=True)
            outs.append((_dot(pw.astype(BF16), vw) + _dot(pc.astype(BF16), vc)) / l)
        rs = slice(a * GRID_W, (a + 1) * GRID_W)
        o_ref[rs, :] = (g_ref[rs, :].astype(F32) * jnp.where(first, outs[0], outs[1])).astype(BF16)


def _na_bias_table(rpb):
    cols = np.arange(GRID_W)
    c0 = np.clip(cols - NA_WIN_COLS // 2, 0, GRID_W - NA_WIN_COLS)
    kc = np.arange(GRID_W)
    valid = (kc[None, :] >= c0[:, None]) & (kc[None, :] < c0[:, None] + NA_WIN_COLS)
    dc = np.clip(kc[None, :] - cols[:, None] + NA_WIN_COLS - 1, 0, 2 * NA_WIN_COLS - 2)
    cls = np.arange(NA_WIN_ROWS)
    dr = np.arange(NA_WIN_ROWS)[None, :] - cls[:, None] + NA_WIN_ROWS - 1
    t = rpb[:, dr][:, :, :, dc]
    t = jnp.where(jnp.asarray(valid)[None, None, None], t, MASK_VALUE)
    t = jnp.transpose(t, (0, 1, 3, 2, 4))
    return t.reshape(NA_HEADS, NA_WIN_ROWS, GRID_W, NA_WIN_ROWS * GRID_W).astype(F32)


def _neighbourhood_attention(q, k, v, gc, rpb):
    tt = q.shape[0]
    t = tt - CTX_LEN
    rows = t // GRID_W
    assert rows >= NA_WIN_ROWS and rows % NA_ROWS_PER_STEP == 0
    bias = _na_bias_table(rpb)
    pw = 2 * NA_DIM
    qrows = NA_ROWS_PER_STEP * GRID_W
    qoff = CTX_LEN // qrows
    assert CTX_LEN % qrows == 0
    return pl.pallas_call(
        functools.partial(_na_kernel, rows=rows),
        grid=(NA_HEADS // 2, rows // NA_ROWS_PER_STEP),
        in_specs=[pl.BlockSpec((qrows, pw), lambda p, i: (qoff + i, p)),
                  pl.BlockSpec((tt, pw), lambda p, i: (0, p)),
                  pl.BlockSpec((tt, pw), lambda p, i: (0, p)),
                  pl.BlockSpec((2, NA_WIN_ROWS, GRID_W, NA_WIN_ROWS * GRID_W), lambda p, i: (p, 0, 0, 0)),
                  pl.BlockSpec((qrows, pw), lambda p, i: (qoff + i, p))],
        out_specs=pl.BlockSpec((qrows, pw), lambda p, i: (i, p)),
        out_shape=jax.ShapeDtypeStruct((t, BRANCH_W), BF16),
        compiler_params=_cparams(("arbitrary", "arbitrary"), 40 * 1024 * 1024),
        name="neighbourhood_attention",
    )(q, k, v, bias, gc)


def _rwkv_chunk_index(d, t, nch):
    nctx = CTX_LEN // RWKV_CHUNK
    bwd = jnp.where(t < nctx, nctx - 1 - t, nch - 1 - (t - nctx))
    return jnp.where(d == 0, t, bwd)


def _rwkv_kernel(main_ref, prev_ref, next_ref, mu_ref, w0_ref, w2_ref, a0_ref, a2_ref, kk_ref, ka_ref, rk_ref,
                 bd_ref, wkv_ref, bon_ref, s_scr, *, nch):
    lc = RWKV_CHUNK
    hd = RWKV_DIM
    nctx = CTX_LEN // lc
    d = pl.program_id(0)
    t = pl.program_id(1)
    c = _rwkv_chunk_index(d, t, nch)

    @pl.when(t == 0)
    def _():
        s_scr[...] = jnp.zeros_like(s_scr)

    p = main_ref[...]
    row = lax.broadcasted_iota(jnp.int32, p.shape, 0)
    first_zero = jnp.logical_or(c == 0, c == nctx)
    last_zero = jnp.logical_or(c == nctx - 1, c == nch - 1)
    pr = jnp.where(first_zero, 0.0, prev_ref[V7X_SUBLANES - 1:V7X_SUBLANES, :])
    nx = jnp.where(last_zero, 0.0, next_ref[0:1, :])
    prev = jnp.where(row == 0, pr, pltpu.roll(p, 1, 0))
    nxt = jnp.where(row == lc - 1, nx, pltpu.roll(p, lc - 1, 0))
    z = p + (0.5 * (prev + nxt) - p) * mu_ref[...]

    bw = BRANCH_W
    r, k, v = z[:, :bw], z[:, bw:2 * bw], z[:, 2 * bw:3 * bw]
    zw = z[:, 3 * bw:3 * bw + 2 * DECAY_LORA]
    za = z[:, 3 * bw + 2 * DECAY_LORA:]
    lw = w0_ref[0] + _dot(jnp.tanh(zw).astype(BF16), w2_ref[0])
    ld = -jnp.exp(-_softplus(-lw) - 0.5)
    asig = _sigmoid(a0_ref[0] + _dot(za.astype(BF16), a2_ref[0]))
    bd = bd_ref[...]
    kk = k * kk_ref[...]
    kkn = kk / jnp.maximum(jnp.sqrt(_dot_exact_rhs(kk * kk, bd)), 1e-12)
    kd = k * (1.0 + (asig - 1.0) * ka_ref[...])
    a_vec = -kkn
    b_vec = kkn * asig
    bon_ref[0] = _dot_exact_rhs(r * kd * rk_ref[...], bd) * v

    sgn = 1 - 2 * d
    ti = lax.broadcasted_iota(jnp.int32, (lc, lc), 0)
    si = lax.broadcasted_iota(jnp.int32, (lc, lc), 1)
    diff = sgn * (ti - si)
    incl = diff >= 0
    strict = diff > 0
    eye = (ti == si).astype(F32)
    cum = _dot_exact_lhs(jnp.where(incl, 1.0, 0.0).astype(BF16), ld)
    tot = jnp.sum(ld, axis=0, keepdims=True)
    rem = jnp.exp(tot - cum)
    pinv = jnp.exp(-cum)
    rt = r * jnp.exp(cum)
    kt = kd * pinv
    bt = b_vec * pinv
    at = a_vec * jnp.exp(cum - ld)
    bh = b_vec * rem
    kh = kd * rem
    pend = jnp.exp(tot)

    for h in range(RWKV_HEADS):
        hs = slice(h * hd, (h + 1) * hd)
        at_h, rt_h, v_h = at[:, hs], rt[:, hs], v[:, hs].astype(BF16)
        lhs = jnp.concatenate([at_h, rt_h], axis=0).astype(BF16)
        rhs = jnp.concatenate([bt[:, hs], kt[:, hs]], axis=0).astype(BF16)
        g4 = _dot_nt(lhs, rhs)
        aab = jnp.where(strict, g4[:lc, :lc], 0.0)
        aak = jnp.where(strict, g4[:lc, lc:], 0.0)
        arb = jnp.where(incl, g4[lc:, :lc], 0.0)
        ark = jnp.where(incl, g4[lc:, lc:], 0.0)
        nk = aab
        x = eye + nk
        for _ in range(int(np.log2(lc)) - 1):
            nkb = nk.astype(BF16)
            nk = _dot(nkb, nkb)
            x = x + _dot(x.astype(BF16), nk.astype(BF16))
        aakv = _dot(aak.astype(BF16), v_h)
        xc = _dot(x.astype(BF16), jnp.concatenate([at_h, aakv], axis=1).astype(BF16))
        xcb = xc.astype(BF16)
        arbxc = _dot(arb.astype(BF16), xcb)
        rhat = rt_h + arbxc[:, :hd]
        y0 = arbxc[:, hd:] + _dot(ark.astype(BF16), v_h)
        xtb = _dot(xc.T.astype(BF16), bh[:, hs].astype(BF16))
        mab = xtb[:hd]
        gt = xtb[hd:] + _dot(v[:, hs].T.astype(BF16), kh[:, hs].astype(BF16))
        s0 = s_scr[h]
        s0b = s0.astype(BF16)
        wkv_ref[0, :, hs] = _dot_nt(rhat.astype(BF16), s0b) + y0
        s_scr[h] = s0 * pend[:, hs] + _dot(s0b, mab.astype(BF16)) + gt


def _head_block_ones(width, head_dim):
    idx = np.arange(width) // head_dim
    return jnp.asarray(idx[:, None] == idx[None, :], BF16)


def _rwkv(rw_p, mu, w0s, w2s, a0s, a2s, k_k, k_a, r_k):
    tt, w = rw_p.shape
    lc = RWKV_CHUNK
    nch = tt // lc
    bw = BRANCH_W
    sub = V7X_SUBLANES
    zeros = jnp.zeros((DECAY_LORA, bw), F32)
    w2p = jnp.stack([jnp.concatenate([w2s[0], zeros]), jnp.concatenate([zeros, w2s[1]])]).astype(BF16)
    a2p = jnp.stack([jnp.concatenate([a2s[0], zeros]), jnp.concatenate([zeros, a2s[1]])]).astype(BF16)
    w0 = jnp.stack(w0s).reshape(2, 1, bw)
    a0 = jnp.stack(a0s).reshape(2, 1, bw)
    bd = _head_block_ones(bw, RWKV_DIM)
    cidx = lambda d, t: _rwkv_chunk_index(d, t, nch)
    vec = lambda: pl.BlockSpec((1, bw), lambda d, t: (0, 0))
    per_dir = lambda rows: pl.BlockSpec((1, rows, bw), lambda d, t: (d, 0, 0))
    out_spec = pl.BlockSpec((1, lc, bw), lambda d, t: (d, cidx(d, t), 0))
    return pl.pallas_call(
        functools.partial(_rwkv_kernel, nch=nch),
        grid=(2, nch),
        in_specs=[pl.BlockSpec((lc, w), lambda d, t: (cidx(d, t), 0)),
                  pl.BlockSpec((sub, w), lambda d, t: (jnp.maximum(cidx(d, t) * (lc // sub) - 1, 0), 0)),
                  pl.BlockSpec((sub, w), lambda d, t: (jnp.minimum((cidx(d, t) + 1) * (lc // sub), tt // sub - 1), 0)),
                  pl.BlockSpec((1, w), lambda d, t: (0, 0)),
                  per_dir(1), per_dir(2 * DECAY_LORA), per_dir(1), per_dir(2 * ICLR_LORA),
                  vec(), vec(), vec(),
                  pl.BlockSpec((bw, bw), lambda d, t: (0, 0))],
        out_specs=[out_spec, out_spec],
        out_shape=[jax.ShapeDtypeStruct((2, tt, bw), F32), jax.ShapeDtypeStruct((2, tt, bw), F32)],
        scratch_shapes=[pltpu.VMEM((RWKV_HEADS, RWKV_DIM, RWKV_DIM), F32)],
        compiler_params=_cparams(("arbitrary", "arbitrary"), 40 * 1024 * 1024),
        name="rwkv7",
    )(rw_p, rw_p, rw_p, mu.reshape(1, w), w0, w2p, a0, a2p,
      k_k.reshape(1, bw), k_a.reshape(1, bw), r_k.reshape(1, bw), bd)


def _odd_out_kernel(na_ref, wkv_ref, bon_ref, gd_ref, lng_ref, lnb_ref, bd_ref, w_ref, res_ref, mod_ref, gpost_ref,
                    o_ref):
    d = o_ref.shape[-1]
    bd = bd_ref[...]
    inv = 1.0 / RWKV_DIM
    wkv = wkv_ref[0] + wkv_ref[1]
    xc = wkv - _dot_exact_rhs(wkv, bd) * inv
    var = _dot_exact_rhs(xc * xc, bd) * inv
    y = xc * lax.rsqrt(var + RWKV_GN_EPS) * lng_ref[...] + lnb_ref[...] + bon_ref[0] + bon_ref[1]
    m2 = (gd_ref[...].astype(F32) * y).astype(BF16)
    out = _dot(na_ref[...], w_ref[:BRANCH_W, :]) + _dot(m2, w_ref[BRANCH_W:, :])
    gate = mod_ref[0][0:1, 2 * d:]
    o_ref[...] = res_ref[...] + gate * (_rms_rows(out) * gpost_ref[...])


def _odd_out(na_g, wkv, bon, gd, lnx_g, lnx_b, w_out, stream, mod, g_post):
    t, bw = na_g.shape
    d = stream.shape[1]
    tm = ROW_BLOCK
    off = CTX_LEN // tm
    lat = lambda w: pl.BlockSpec((tm, w), lambda i: (off + i, 0))
    vec = lambda w: pl.BlockSpec((1, w), lambda i: (0, 0))
    two = pl.BlockSpec((2, tm, bw), lambda i: (0, off + i, 0))
    return pl.pallas_call(
        _odd_out_kernel,
        grid=(t // tm,),
        in_specs=[pl.BlockSpec((tm, bw), lambda i: (i, 0)), two, two, lat(bw), vec(bw), vec(bw),
                  pl.BlockSpec((bw, bw), lambda i: (0, 0)),
                  pl.BlockSpec((2 * bw, d), lambda i: (0, 0)),
                  lat(d),
                  pl.BlockSpec((1, V7X_SUBLANES, 3 * d), lambda i: (1, 0, 0)),
                  vec(d)],
        out_specs=pl.BlockSpec((tm, d), lambda i: (i, 0)),
        out_shape=jax.ShapeDtypeStruct((t, d), F32),
        compiler_params=_cparams(("arbitrary",), 48 * 1024 * 1024),
        name="odd_out_proj",
    )(na_g, wkv, bon, gd, lnx_g.reshape(1, bw), lnx_b.reshape(1, bw), _head_block_ones(bw, RWKV_DIM), w_out,
      stream, mod, g_post.reshape(1, d))


def _rope_tables(t):
    pos = jnp.arange(t)
    rowp = (pos // GRID_W).astype(F32)
    colp = (pos % GRID_W).astype(F32)
    n_freq = RET_DK // 4
    inv = ROPE_THETA ** (-jnp.arange(n_freq, dtype=F32) / n_freq)
    ang = jnp.concatenate([rowp[:, None] * inv, colp[:, None] * inv], axis=-1)
    cos = jnp.repeat(jnp.cos(ang), 2, axis=-1)
    sin = jnp.repeat(jnp.sin(ang), 2, axis=-1) * jnp.tile(jnp.asarray([-1.0, 1.0], F32), RET_DK // 2)
    cos = jnp.concatenate([jnp.ones((CTX_LEN, RET_DK), F32), cos], axis=0)
    sin = jnp.concatenate([jnp.zeros((CTX_LEN, RET_DK), F32), sin], axis=0)
    return cos, sin


def kernel(x, c, ctx, c_ctx, w_mod, b_mod, g_pre, g_post, ev_w_in, ev_w_out, ret_decay_fwd, ret_decay_bwd, gqa_q_norm, gqa_k_norm, od_w_in, od_w_out, na_rpb, rwkv_shift_mu, rwkv_w0_fwd, rwkv_w2_fwd, rwkv_w0_bwd, rwkv_w2_bwd, rwkv_a0_fwd, rwkv_a2_fwd, rwkv_a0_bwd, rwkv_a2_bwd, rwkv_k_k, rwkv_k_a, rwkv_r_k, rwkv_lnx_g, rwkv_lnx_b):
    assert x.shape[0] == 1 and DEPTH == 2 and RET_DK == GQA_DIM
    t = x.shape[1]
    tt = t + CTX_LEN
    assert t % ROW_BLOCK == 0 and tt % ATT_KV_BLOCK == 0 and t % (GRID_W * NA_ROWS_PER_STEP) == 0
    x2, ctx2 = x[0], ctx[0]
    mod = _modulation(c, c_ctx, w_mod, b_mod)
    cos_t, sin_t = _rope_tables(t)

    qa, ka, va, qb, kbt, vb, ga, gb = _even_in_proj(x2, ctx2, mod, g_pre[0], ev_w_in[0].astype(BF16), cos_t, sin_t,
                                                    gqa_q_norm[0], gqa_k_norm[0])
    ret_g = _retention(qa, ka, va, ga, ret_decay_fwd[0], ret_decay_bwd[0])
    att_lat = _gqa_attention(qb, kbt, vb, gb, q_row0=CTX_LEN, n_q=t, n_keys=tt, kv_block=ATT_KV_BLOCK, out_rows=t)
    att_ctx = _gqa_attention(qb, kbt, vb, gb, q_row0=0, n_q=CTX_LEN, n_keys=CTX_LEN, kv_block=CTX_LEN,
                             out_rows=CTX_LEN)
    att_g = jnp.concatenate([att_ctx, att_lat], axis=0)
    stream0 = jnp.concatenate([ctx2, x2], axis=0)
    stream1 = _out_proj(ret_g, att_g, ev_w_out[0].astype(BF16), stream0, mod, 0, g_post[0],
                        ctx_blocks=CTX_LEN // ROW_BLOCK)

    q, k, v, rw_p, gc, gd = _odd_in_proj(stream1, mod, g_pre[1], od_w_in[0].astype(BF16))
    na_g = _neighbourhood_attention(q, k, v, gc, na_rpb[0])
    wkv, bon = _rwkv(rw_p, rwkv_shift_mu[0], (rwkv_w0_fwd[0], rwkv_w0_bwd[0]), (rwkv_w2_fwd[0], rwkv_w2_bwd[0]),
                     (rwkv_a0_fwd[0], rwkv_a0_bwd[0]), (rwkv_a2_fwd[0], rwkv_a2_bwd[0]),
                     rwkv_k_k[0], rwkv_k_a[0], rwkv_r_k[0].reshape(-1))
    out = _odd_out(na_g, wkv, bon, gd, rwkv_lnx_g[0], rwkv_lnx_b[0], od_w_out[0].astype(BF16), stream1, mod,
                   g_post[1])
    return out[None]
```

```python
import functools

import jax
import jax.numpy as jnp
import numpy as np
from jax import lax
from jax.experimental import pallas as pl
from jax.experimental.pallas import tpu as pltpu

F32 = jnp.float32
BF16 = jnp.bfloat16

D_MODEL = 1024
DEPTH = 2
GRID_W = 64
CTX_LEN = 256
BRANCH_W = D_MODEL
RET_HEADS = 4
RET_DK = 128
RET_DV = BRANCH_W // RET_HEADS
RET_CHUNK = 128
GQA_HEADS = 8
GQA_KV_HEADS = 2
GQA_DIM = BRANCH_W // GQA_HEADS
GQA_GROUP = GQA_HEADS // GQA_KV_HEADS
NA_HEADS = 16
NA_DIM = BRANCH_W // NA_HEADS
NA_WIN_ROWS = 8
NA_WIN_COLS = 16
RWKV_HEADS = 16
RWKV_DIM = BRANCH_W // RWKV_HEADS
DECAY_LORA = 64
ICLR_LORA = 64
ROPE_THETA = 10000.0
NORM_EPS = 1e-6
RWKV_GN_EPS = 64e-5
SHIFT_W = 3 * BRANCH_W + 2 * DECAY_LORA + 2 * ICLR_LORA
EVEN_IN = 2 * RET_HEADS * RET_DK + BRANCH_W + GQA_HEADS * GQA_DIM + 2 * GQA_KV_HEADS * GQA_DIM + 2 * BRANCH_W
ODD_IN = 3 * BRANCH_W + SHIFT_W + 2 * BRANCH_W

V7X_LANES = 128
V7X_SUBLANES = 8
V7X_VMEM_BYTES = 64 * 1024 * 1024

ROW_BLOCK = CTX_LEN
RWKV_CHUNK = 64
ATT_Q_BLOCK = 256
ATT_KV_BLOCK = 1280
LOG2_E = 1.4426950408889634
NA_ROWS_PER_STEP = 4
MASK_VALUE = -1e30


def _vmem_limit(nbytes):
    return int(min(V7X_VMEM_BYTES - 4 * 1024 * 1024, max(32 * 1024 * 1024, nbytes)))


def _cparams(sem, vmem_bytes):
    return pltpu.CompilerParams(dimension_semantics=sem, vmem_limit_bytes=_vmem_limit(vmem_bytes))


def _silu(x):
    return x / (1.0 + jnp.exp(-x))


def _sigmoid(x):
    return 1.0 / (1.0 + jnp.exp(-x))


def _softplus(x):
    return jnp.maximum(x, 0.0) + jnp.log(1.0 + jnp.exp(-jnp.abs(x)))


def _dot(a, b):
    return jnp.dot(a, b, preferred_element_type=F32)


def _dot_nt(a, b):
    return lax.dot_general(a, b, (((1,), (1,)), ((), ())), preferred_element_type=F32)


def _split3(x):
    hi = x.astype(BF16)
    r1 = x - hi.astype(F32)
    mid = r1.astype(BF16)
    lo = (r1 - mid.astype(F32)).astype(BF16)
    return hi, mid, lo


def _dot_exact_lhs(a_bf16, x):
    hi, mid, lo = _split3(x)
    return _dot(a_bf16, hi) + _dot(a_bf16, mid) + _dot(a_bf16, lo)


def _dot_exact_rhs(x, b_bf16):
    hi, mid, lo = _split3(x)
    return _dot(hi, b_bf16) + _dot(mid, b_bf16) + _dot(lo, b_bf16)


def _rms_rows(x):
    return x * lax.rsqrt(jnp.mean(x * x, axis=-1, keepdims=True) + NORM_EPS)


def _rope(t, cos, sin_signed, even):
    nxt = pltpu.roll(t, t.shape[1] - 1, 1)
    prv = pltpu.roll(t, 1, 1)
    return t * cos + jnp.where(even, nxt, prv) * sin_signed


def _mod_kernel(cc_ref, w_ref, b_ref, o_ref):
    s = _silu(cc_ref[...])
    o_ref[0] = jnp.dot(s, w_ref[0], preferred_element_type=F32, precision=lax.Precision.HIGHEST) + b_ref[0]


def _modulation(c, c_ctx, w_mod, b_mod):
    d = c.shape[-1]
    cc = jnp.concatenate([c[:1], c_ctx[None, :], jnp.zeros((V7X_SUBLANES - 2, d), F32)], axis=0)
    return pl.pallas_call(
        _mod_kernel,
        grid=(DEPTH, 3),
        in_specs=[pl.BlockSpec((V7X_SUBLANES, d), lambda l, j: (0, 0)),
                  pl.BlockSpec((1, d, d), lambda l, j: (l, 0, j)),
                  pl.BlockSpec((1, 1, d), lambda l, j: (l, 0, j))],
        out_specs=pl.BlockSpec((1, V7X_SUBLANES, d), lambda l, j: (l, 0, j)),
        out_shape=jax.ShapeDtypeStruct((DEPTH, V7X_SUBLANES, 3 * d), F32),
        compiler_params=_cparams(("arbitrary", "arbitrary"), 40 * 1024 * 1024),
        name="modulation",
    )(cc, w_mod, b_mod.reshape(DEPTH, 1, 3 * d))


def _adaln(xb, mod, is_ctx, g_pre):
    d = xb.shape[-1]
    m = jnp.where(is_ctx, mod[1:2, :], mod[0:1, :])
    shift, scale = m[:, :d], m[:, d:2 * d]
    return (_rms_rows(xb) * g_pre) * (1.0 + scale) + shift


def _even_in_kernel(x_ref, ctx_ref, mod_ref, gpre_ref, w_ref, cos_ref, sin_ref, qn_ref, kn_ref,
                    qa_ref, ka_ref, va_ref, qb_ref, kbt_ref, vb_ref, ga_ref, gb_ref):
    is_ctx = pl.program_id(0) == 0
    xb = jnp.where(is_ctx, ctx_ref[...], x_ref[...])
    hb = _adaln(xb, mod_ref[0], is_ctx, gpre_ref[...]).astype(BF16)
    cos, sin_s = cos_ref[...], sin_ref[...]
    even = (lax.broadcasted_iota(jnp.int32, cos.shape, 1) & 1) == 0
    o = 0

    def seg(width):
        nonlocal o
        y = _dot(hb, w_ref[:, o:o + width])
        o += width
        return y

    y = seg(RET_HEADS * RET_DK)
    for h in range(RET_HEADS):
        sl = slice(h * RET_DK, (h + 1) * RET_DK)
        qa_ref[:, sl] = _rope(y[:, sl], cos, sin_s, even).astype(BF16)
    y = seg(RET_HEADS * RET_DK)
    for h in range(RET_HEADS):
        sl = slice(h * RET_DK, (h + 1) * RET_DK)
        ka_ref[:, sl] = _rope(y[:, sl] * RET_DK ** -0.5, cos, sin_s, even).astype(BF16)
    va_ref[...] = seg(BRANCH_W).astype(BF16)
    y = seg(GQA_HEADS * GQA_DIM)
    for h in range(GQA_HEADS):
        sl = slice(h * GQA_DIM, (h + 1) * GQA_DIM)
        t = _rms_rows(y[:, sl]) * qn_ref[...]
        qb_ref[:, sl] = (_rope(t, cos, sin_s, even) * (GQA_DIM ** -0.5 * LOG2_E)).astype(BF16)
    y = seg(GQA_KV_HEADS * GQA_DIM)
    for h in range(GQA_KV_HEADS):
        sl = slice(h * GQA_DIM, (h + 1) * GQA_DIM)
        t = _rope(_rms_rows(y[:, sl]) * kn_ref[...], cos, sin_s, even)
        kbt_ref[sl, :] = t.T.astype(BF16)
    y = seg(GQA_KV_HEADS * GQA_DIM)
    for h in range(GQA_KV_HEADS):
        vb_ref[:, 2 * h * GQA_DIM:(2 * h + 1) * GQA_DIM] = y[:, h * GQA_DIM:(h + 1) * GQA_DIM].astype(BF16)
        vb_ref[:, (2 * h + 1) * GQA_DIM:(2 * h + 2) * GQA_DIM] = jnp.ones((y.shape[0], GQA_DIM), BF16)
    ga_ref[...] = _silu(seg(BRANCH_W)).astype(BF16)
    gb_ref[...] = _silu(seg(BRANCH_W)).astype(BF16)


def _even_in_proj(x, ctx, mod, g_pre, w_in, cos_t, sin_t, q_norm, k_norm):
    t, d = x.shape
    tt = t + CTX_LEN
    nblk = tt // ROW_BLOCK
    tm = ROW_BLOCK
    kvw = GQA_KV_HEADS * GQA_DIM
    row = lambda w: pl.BlockSpec((tm, w), lambda i: (i, 0))
    const = lambda shape: pl.BlockSpec(shape, lambda i: tuple(0 for _ in shape))
    outs = [((tt, RET_HEADS * RET_DK), row(RET_HEADS * RET_DK)),
            ((tt, RET_HEADS * RET_DK), row(RET_HEADS * RET_DK)),
            ((tt, BRANCH_W), row(BRANCH_W)),
            ((tt, GQA_HEADS * GQA_DIM), row(GQA_HEADS * GQA_DIM)),
            ((kvw, tt), pl.BlockSpec((kvw, tm), lambda i: (0, i))),
            ((tt, 2 * kvw), row(2 * kvw)),
            ((tt, BRANCH_W), row(BRANCH_W)),
            ((tt, BRANCH_W), row(BRANCH_W))]
    return pl.pallas_call(
        _even_in_kernel,
        grid=(nblk,),
        in_specs=[pl.BlockSpec((tm, d), lambda i: (jnp.maximum(i - 1, 0), 0)),
                  const((CTX_LEN, d)),
                  pl.BlockSpec((1, V7X_SUBLANES, 3 * d), lambda i: (0, 0, 0)),
                  const((1, d)),
                  const((d, EVEN_IN)),
                  row(RET_DK), row(RET_DK),
                  const((1, GQA_DIM)), const((1, GQA_DIM))],
        out_specs=[s for _, s in outs],
        out_shape=[jax.ShapeDtypeStruct(shp, BF16) for shp, _ in outs],
        compiler_params=_cparams(("arbitrary",), 2 * d * EVEN_IN * 2 + 16 * 1024 * 1024),
        name="even_in_proj",
    )(x, ctx, mod, g_pre.reshape(1, d), w_in, cos_t, sin_t, q_norm.reshape(1, -1), k_norm.reshape(1, -1))


def _log_sigmoid(x):
    return jnp.minimum(x, 0.0) - jnp.log(1.0 + jnp.exp(-jnp.abs(x)))


def _ret_bwd_chunk(t, nchunks):
    nctx = CTX_LEN // RET_CHUNK
    return jnp.where(t < nctx, nctx - 1 - t, nchunks - 1 - (t - nctx))


def _ret_state_kernel(dec_ref, k_ref, v_ref, sb_ref, s_scr):
    c = RET_CHUNK

    @pl.when(pl.program_id(0) == 0)
    def _():
        s_scr[...] = jnp.zeros_like(s_scr)

    lg = _log_sigmoid(dec_ref[...])
    pos = lax.broadcasted_iota(jnp.int32, (c, RET_DK), 0).astype(F32)
    for h in range(RET_HEADS):
        lgb = lg[RET_HEADS + h:RET_HEADS + h + 1, :]
        s_old = s_scr[h]
        sb_ref[0, h] = s_old.astype(BF16)
        kz = k_ref[:, h * RET_DK:(h + 1) * RET_DK].astype(F32) * jnp.exp(pos * lgb)
        u = _dot(kz.T.astype(BF16), v_ref[:, h * RET_DV:(h + 1) * RET_DV])
        s_scr[h] = jnp.exp(c * lgb[:, :1]) * s_old + u


def _ret_out_kernel(dec_ref, q_ref, k_ref, v_ref, g_ref, sb_ref, o_ref, s_scr):
    c = RET_CHUNK

    @pl.when(pl.program_id(0) == 0)
    def _():
        s_scr[...] = jnp.zeros_like(s_scr)

    lg = _log_sigmoid(dec_ref[...])
    ii = lax.broadcasted_iota(jnp.int32, (c, c), 0)
    jj = lax.broadcasted_iota(jnp.int32, (c, c), 1)
    dlt = (ii - jj).astype(F32)
    pos = lax.broadcasted_iota(jnp.int32, (c, RET_DK), 0).astype(F32)
    for h in range(RET_HEADS):
        lgf = lg[h:h + 1, :]
        lgb = lg[RET_HEADS + h:RET_HEADS + h + 1, :]
        dec = jnp.where(dlt > 0, jnp.exp(jnp.maximum(dlt, 0.0) * lgf),
                        jnp.where(dlt < 0, jnp.exp(jnp.maximum(-dlt, 0.0) * lgb), 2.0))
        q = q_ref[:, h * RET_DK:(h + 1) * RET_DK]
        k = k_ref[:, h * RET_DK:(h + 1) * RET_DK]
        v = v_ref[:, h * RET_DV:(h + 1) * RET_DV]
        qf = q.astype(F32)
        p = (_dot_nt(q, k) * dec).astype(BF16)
        s_old = s_scr[h]
        ret = (_dot(p, v)
               + _dot((qf * jnp.exp((pos + 1.0) * lgf)).astype(BF16), s_old.astype(BF16))
               + _dot((qf * jnp.exp((c - pos) * lgb)).astype(BF16), sb_ref[0, h]))
        xc = ret - jnp.mean(ret, axis=-1, keepdims=True)
        y = xc * lax.rsqrt(jnp.mean(xc * xc, axis=-1, keepdims=True) + NORM_EPS)
        sl = slice(h * RET_DV, (h + 1) * RET_DV)
        o_ref[:, sl] = (g_ref[:, sl].astype(F32) * y).astype(BF16)
        kz = k.astype(F32) * jnp.exp((c - 1.0 - pos) * lgf)
        s_scr[h] = jnp.exp(c * lgf[:, :1]) * s_old + _dot(kz.T.astype(BF16), v)


def _retention(qa, ka, va, ga, dec_f, dec_b):
    tt = qa.shape[0]
    c = RET_CHUNK
    n = tt // c
    dec = jnp.broadcast_to(jnp.concatenate([dec_f, dec_b]).astype(F32)[:, None], (2 * RET_HEADS, V7X_LANES))
    kw, vw = RET_HEADS * RET_DK, BRANCH_W
    dec_spec = pl.BlockSpec((2 * RET_HEADS, V7X_LANES), lambda t: (0, 0))
    state_shape = (RET_HEADS, RET_DK, RET_DV)
    sb = pl.pallas_call(
        _ret_state_kernel,
        grid=(n,),
        in_specs=[dec_spec,
                  pl.BlockSpec((c, kw), lambda t: (_ret_bwd_chunk(t, n), 0)),
                  pl.BlockSpec((c, vw), lambda t: (_ret_bwd_chunk(t, n), 0))],
        out_specs=pl.BlockSpec((1,) + state_shape, lambda t: (_ret_bwd_chunk(t, n), 0, 0, 0)),
        out_shape=jax.ShapeDtypeStruct((n,) + state_shape, BF16),
        scratch_shapes=[pltpu.VMEM(state_shape, F32)],
        compiler_params=_cparams(("arbitrary",), 32 * 1024 * 1024),
        name="retention_state",
    )(dec, ka, va)
    return pl.pallas_call(
        _ret_out_kernel,
        grid=(n,),
        in_specs=[dec_spec,
                  pl.BlockSpec((c, kw), lambda t: (t, 0)),
                  pl.BlockSpec((c, kw), lambda t: (t, 0)),
                  pl.BlockSpec((c, vw), lambda t: (t, 0)),
                  pl.BlockSpec((c, vw), lambda t: (t, 0)),
                  pl.BlockSpec((1,) + state_shape, lambda t: (t, 0, 0, 0))],
        out_specs=pl.BlockSpec((c, vw), lambda t: (t, 0)),
        out_shape=jax.ShapeDtypeStruct((tt, vw), BF16),
        scratch_shapes=[pltpu.VMEM(state_shape, F32)],
        compiler_params=_cparams(("arbitrary",), 32 * 1024 * 1024),
        name="retention_out",
    )(dec, qa, ka, va, ga, sb)


def _gqa_kernel(q_ref, kt_ref, v_ref, g_ref, o_ref, s0_scr, s1_scr, *, kv_block, n_kv):
    tq = q_ref.shape[0]
    heads = [slice(h * GQA_DIM, (h + 1) * GQA_DIM) for h in range(GQA_GROUP)]
    q = jnp.concatenate([q_ref[:, sl] for sl in heads], axis=0)
    rows = GQA_GROUP * tq

    def scores(j, s_ref):
        start = pl.multiple_of(j * kv_block, V7X_LANES)
        s_ref[...] = _dot(q, kt_ref[:, pl.ds(start, kv_block)])

    def softmax_pv(j, s_ref, carry):
        m, acc = carry
        start = pl.multiple_of(j * kv_block, V7X_LANES)
        s = s_ref[...]
        m_new = jnp.maximum(m, jnp.max(s, axis=-1, keepdims=True))
        p = jnp.exp2(s - m_new).astype(BF16)
        acc = jnp.exp2(m - m_new) * acc + _dot(p, v_ref[pl.ds(start, kv_block), :])
        return m_new, acc

    def body(i, carry):
        j = 2 * i
        scores(j + 1, s1_scr)
        carry = softmax_pv(j, s0_scr, carry)
        scores(j + 2, s0_scr)
        return softmax_pv(j + 1, s1_scr, carry)

    assert n_kv % 2 == 1
    carry = (jnp.full((rows, 1), MASK_VALUE, F32), jnp.zeros((rows, 2 * GQA_DIM), F32))
    scores(0, s0_scr)
    carry = lax.fori_loop(0, n_kv // 2, body, carry)
    _, acc = softmax_pv(n_kv - 1, s0_scr, carry)
    out = acc[:, :GQA_DIM] / acc[:, GQA_DIM:]
    for h, sl in enumerate(heads):
        o_ref[:, sl] = (g_ref[:, sl].astype(F32) * out[h * tq:(h + 1) * tq]).astype(BF16)


def _gqa_attention(qb, kbt, vb, gb, *, q_row0, n_q, n_keys, kv_block, out_rows):
    gw = GQA_GROUP * GQA_DIM
    tq = ATT_Q_BLOCK
    qoff = q_row0 // tq
    return pl.pallas_call(
        functools.partial(_gqa_kernel, kv_block=kv_block, n_kv=n_keys // kv_block),
        grid=(GQA_KV_HEADS, n_q // tq),
        in_specs=[pl.BlockSpec((tq, gw), lambda g, i: (qoff + i, g)),
                  pl.BlockSpec((GQA_DIM, n_keys), lambda g, i: (g, 0)),
                  pl.BlockSpec((n_keys, 2 * GQA_DIM), lambda g, i: (0, g)),
                  pl.BlockSpec((tq, gw), lambda g, i: (qoff + i, g))],
        out_specs=pl.BlockSpec((tq, gw), lambda g, i: (i, g)),
        out_shape=jax.ShapeDtypeStruct((out_rows, GQA_HEADS * GQA_DIM), BF16),
        scratch_shapes=[pltpu.VMEM((GQA_GROUP * tq, kv_block), F32)] * 2,
        compiler_params=_cparams(("arbitrary", "arbitrary"), 56 * 1024 * 1024),
        name="gqa_attention",
    )(qb, kbt, vb, gb)


def _out_proj_kernel(m1_ref, m2c_ref, m2l_ref, w_ref, resc_ref, resl_ref, mod_ref, gpost_ref, o_ref):
    d = o_ref.shape[-1]
    is_ctx = pl.program_id(0) == 0
    m2 = jnp.where(is_ctx, m2c_ref[...], m2l_ref[...])
    y = _dot(m1_ref[...], w_ref[:BRANCH_W, :]) + _dot(m2, w_ref[BRANCH_W:, :])
    mod = mod_ref[0]
    gate = jnp.where(is_ctx, mod[1:2, 2 * d:], mod[0:1, 2 * d:])
    res = jnp.where(is_ctx, resc_ref[...], resl_ref[...])
    o_ref[...] = res + gate * (_rms_rows(y) * gpost_ref[...])


def _out_proj(m1, m2_ctx, m2_lat, w_out, res_ctx, res_lat, mod, layer, g_post):
    tt = m1.shape[0]
    d = res_lat.shape[1]
    tm = ROW_BLOCK
    row = lambda w: pl.BlockSpec((tm, w), lambda i: (i, 0))
    lat = lambda w: pl.BlockSpec((tm, w), lambda i: (jnp.maximum(i - 1, 0), 0))
    ctx = lambda w: pl.BlockSpec((CTX_LEN, w), lambda i: (0, 0))
    return pl.pallas_call(
        _out_proj_kernel,
        grid=(tt // tm,),
        in_specs=[row(BRANCH_W), ctx(BRANCH_W), lat(BRANCH_W),
                  pl.BlockSpec((2 * BRANCH_W, d), lambda i: (0, 0)),
                  ctx(d), lat(d),
                  pl.BlockSpec((1, V7X_SUBLANES, 3 * d), lambda i: (layer, 0, 0)),
                  pl.BlockSpec((1, d), lambda i: (0, 0))],
        out_specs=row(d),
        out_shape=jax.ShapeDtypeStruct((tt, d), F32),
        compiler_params=_cparams(("arbitrary",), 40 * 1024 * 1024),
        name="out_proj",
    )(m1, m2_ctx, m2_lat, w_out, res_ctx, res_lat, mod, g_post.reshape(1, d))


def _odd_in_kernel(s_ref, mod_ref, gpre_ref, w_ref, q_ref, k_ref, v_ref, rw_ref, gc_ref, gd_ref):
    is_ctx = pl.program_id(0) == 0
    hb = _adaln(s_ref[...], mod_ref[0], is_ctx, gpre_ref[...]).astype(BF16)
    o = 0

    def seg(width):
        nonlocal o
        y = _dot(hb, w_ref[:, o:o + width])
        o += width
        return y

    q_ref[...] = (seg(BRANCH_W) * NA_DIM ** -0.5).astype(BF16)
    k_ref[...] = seg(BRANCH_W).astype(BF16)
    v_ref[...] = seg(BRANCH_W).astype(BF16)
    rw_ref[...] = seg(SHIFT_W)
    gc_ref[...] = _silu(seg(BRANCH_W)).astype(BF16)
    gd_ref[...] = _silu(seg(BRANCH_W)).astype(BF16)


def _odd_in_proj(stream, mod, g_pre, w_in):
    tt, d = stream.shape
    tm = ROW_BLOCK
    row = lambda w: pl.BlockSpec((tm, w), lambda i: (i, 0))
    widths = [(BRANCH_W, BF16), (BRANCH_W, BF16), (BRANCH_W, BF16), (SHIFT_W, F32), (BRANCH_W, BF16), (BRANCH_W, BF16)]
    return pl.pallas_call(
        _odd_in_kernel,
        grid=(tt // tm,),
        in_specs=[row(d),
                  pl.BlockSpec((1, V7X_SUBLANES, 3 * d), lambda i: (1, 0, 0)),
                  pl.BlockSpec((1, d), lambda i: (0, 0)),
                  pl.BlockSpec((d, ODD_IN), lambda i: (0, 0), pipeline_mode=pl.Buffered(1))],
        out_specs=[row(w) for w, _ in widths],
        out_shape=[jax.ShapeDtypeStruct((tt, w), dt) for w, dt in widths],
        compiler_params=_cparams(("arbitrary",), d * ODD_IN * 2 + 24 * 1024 * 1024),
        name="odd_in_proj",
    )(stream, mod, g_pre.reshape(1, d), w_in)


def _na_kernel(q_ref, k_ref, v_ref, bias_ref, g_ref, o_ref, *, rows):
    win = NA_WIN_ROWS * GRID_W
    lane = lax.broadcasted_iota(jnp.int32, (GRID_W, 2 * NA_DIM), 1)
    first = lane < NA_DIM
    kc, vc = k_ref[:CTX_LEN, :], v_ref[:CTX_LEN, :]
    for a in range(NA_ROWS_PER_STEP):
        qr = pl.program_id(1) * NA_ROWS_PER_STEP + a
        r0 = jnp.clip(qr - NA_WIN_ROWS // 2, 0, rows - NA_WIN_ROWS)
        cls = qr - r0
        start = pl.multiple_of(CTX_LEN + r0 * GRID_W, GRID_W)
        kw, vw = k_ref[pl.ds(start, win), :], v_ref[pl.ds(start, win), :]
        q2 = q_ref[a * GRID_W:(a + 1) * GRID_W, :]
        outs = []
        for h in range(2):
            qh = jnp.where(first if h == 0 else jnp.logical_not(first), q2, jnp.zeros_like(q2))
            sw = _dot_nt(qh, kw) + bias_ref[h, cls]
            sc = _dot_nt(qh, kc)
            m = jnp.maximum(jnp.max(sw, axis=-1, keepdims=True), jnp.max(sc, axis=-1, keepdims=True))
            pw, pc = jnp.exp(sw - m), jnp.exp(sc - m)
            l = jnp.sum(pw, axis=-1, keepdims=True) + jnp.sum(pc, axis=-1, keepdims=True)
            outs.append((_dot(pw.astype(BF16), vw) + _dot(pc.astype(BF16), vc)) / l)
        rs = slice(a * GRID_W, (a + 1) * GRID_W)
        o_ref[rs, :] = (g_ref[rs, :].astype(F32) * jnp.where(first, outs[0], outs[1])).astype(BF16)


def _na_bias_table(rpb):
    cols = np.arange(GRID_W)
    c0 = np.clip(cols - NA_WIN_COLS // 2, 0, GRID_W - NA_WIN_COLS)
    kc = np.arange(GRID_W)
    valid = (kc[None, :] >= c0[:, None]) & (kc[None, :] < c0[:, None] + NA_WIN_COLS)
    dc = np.clip(kc[None, :] - cols[:, None] + NA_WIN_COLS - 1, 0, 2 * NA_WIN_COLS - 2)
    cls = np.arange(NA_WIN_ROWS)
    dr = np.arange(NA_WIN_ROWS)[None, :] - cls[:, None] + NA_WIN_ROWS - 1
    t = rpb[:, dr][:, :, :, dc]
    t = jnp.where(jnp.asarray(valid)[None, None, None], t, MASK_VALUE)
    t = jnp.transpose(t, (0, 1, 3, 2, 4))
    return t.reshape(NA_HEADS, NA_WIN_ROWS, GRID_W, NA_WIN_ROWS * GRID_W).astype(F32)


def _neighbourhood_attention(q, k, v, gc, rpb):
    tt = q.shape[0]
    t = tt - CTX_LEN
    rows = t // GRID_W
    assert rows >= NA_WIN_ROWS and rows % NA_ROWS_PER_STEP == 0
    bias = _na_bias_table(rpb)
    pw = 2 * NA_DIM
    qrows = NA_ROWS_PER_STEP * GRID_W
    qoff = CTX_LEN // qrows
    assert CTX_LEN % qrows == 0
    return pl.pallas_call(
        functools.partial(_na_kernel, rows=rows),
        grid=(NA_HEADS // 2, rows // NA_ROWS_PER_STEP),
        in_specs=[pl.BlockSpec((qrows, pw), lambda p, i: (qoff + i, p)),
                  pl.BlockSpec((tt, pw), lambda p, i: (0, p)),
                  pl.BlockSpec((tt, pw), lambda p, i: (0, p)),
                  pl.BlockSpec((2, NA_WIN_ROWS, GRID_W, NA_WIN_ROWS * GRID_W), lambda p, i: (p, 0, 0, 0)),
                  pl.BlockSpec((qrows, pw), lambda p, i: (qoff + i, p))],
        out_specs=pl.BlockSpec((qrows, pw), lambda p, i: (i, p)),
        out_shape=jax.ShapeDtypeStruct((t, BRANCH_W), BF16),
        compiler_params=_cparams(("arbitrary", "arbitrary"), 40 * 1024 * 1024),
        name="neighbourhood_attention",
    )(q, k, v, bias, gc)


def _rwkv_chunk_index(d, t, nch):
    nctx = CTX_LEN // RWKV_CHUNK
    bwd = jnp.where(t < nctx, nctx - 1 - t, nch - 1 - (t - nctx))
    return jnp.where(d == 0, t, bwd)


def _rwkv_kernel(main_ref, prev_ref, next_ref, mu_ref, w0_ref, w2_ref, a0_ref, a2_ref, kk_ref, ka_ref, rk_ref,
                 e_ref, et_ref, wkv_ref, bon_ref, s_scr, *, nch):
    lc = RWKV_CHUNK
    hd = RWKV_DIM
    nctx = CTX_LEN // lc
    d = pl.program_id(0)
    t = pl.program_id(1)
    c = _rwkv_chunk_index(d, t, nch)

    @pl.when(t == 0)
    def _():
        s_scr[...] = jnp.zeros_like(s_scr)

    p = main_ref[...]
    row = lax.broadcasted_iota(jnp.int32, p.shape, 0)
    first_zero = jnp.logical_or(c == 0, c == nctx)
    last_zero = jnp.logical_or(c == nctx - 1, c == nch - 1)
    pr = jnp.where(first_zero, 0.0, prev_ref[V7X_SUBLANES - 1:V7X_SUBLANES, :])
    nx = jnp.where(last_zero, 0.0, next_ref[0:1, :])
    prev = jnp.where(row == 0, pr, pltpu.roll(p, 1, 0))
    nxt = jnp.where(row == lc - 1, nx, pltpu.roll(p, lc - 1, 0))
    z = p + (0.5 * (prev + nxt) - p) * mu_ref[...]

    bw = BRANCH_W
    r, k, v = z[:, :bw], z[:, bw:2 * bw], z[:, 2 * bw:3 * bw]
    zw = z[:, 3 * bw:3 * bw + 2 * DECAY_LORA]
    za = z[:, 3 * bw + 2 * DECAY_LORA:]
    lw = w0_ref[0] + _dot(jnp.tanh(zw).astype(BF16), w2_ref[0])
    ld = -jnp.exp(-_softplus(-lw) - 0.5)
    asig = _sigmoid(a0_ref[0] + _dot(za.astype(BF16), a2_ref[0]))
    kk = k * kk_ref[...]
    kd = k * (1.0 + (asig - 1.0) * ka_ref[...])
    parts = _split3(kk * kk) + _split3(r * kd * rk_ref[...])
    hsum = _dot(jnp.concatenate(parts, axis=0), e_ref[...])
    hsum = [hsum[i * lc:(i + 1) * lc] for i in range(6)]
    parts = _split3(hsum[0] + hsum[1] + hsum[2]) + _split3(hsum[3] + hsum[4] + hsum[5])
    hbc = _dot(jnp.concatenate(parts, axis=0), et_ref[...])
    hbc = [hbc[i * lc:(i + 1) * lc] for i in range(6)]
    kkn = kk / jnp.maximum(jnp.sqrt(hbc[0] + hbc[1] + hbc[2]), 1e-12)
    a_vec = -kkn
    b_vec = kkn * asig
    bon_ref[0] = (hbc[3] + hbc[4] + hbc[5]) * v

    sgn = 1 - 2 * d
    ti = lax.broadcasted_iota(jnp.int32, (lc, lc), 0)
    si = lax.broadcasted_iota(jnp.int32, (lc, lc), 1)
    cum = _dot_exact_lhs(jnp.where(sgn * (ti - si) >= 0, 1.0, 0.0).astype(BF16), ld)
    tot = jnp.sum(ld, axis=0, keepdims=True)
    rem = jnp.exp(tot - cum)
    pinv = jnp.exp(-cum)
    rt = r * jnp.exp(cum)
    kt = kd * pinv
    bt = b_vec * pinv
    at = a_vec * jnp.exp(cum - ld)
    bh = b_vec * rem
    kh = kd * rem
    pend = jnp.exp(tot)

    pw = 2 * hd
    pairs = range(RWKV_HEADS // 2)
    lanes = [slice(p * pw, (p + 1) * pw) for p in pairs]
    lane = lax.broadcasted_iota(jnp.int32, (lc, pw), 1)
    tok = lax.broadcasted_iota(jnp.int32, (lc, pw), 0)
    first = lane < hd
    diff = sgn * (tok - (lane & (hd - 1)))
    incl = diff >= 0
    strict = diff > 0
    eye = (diff == 0).astype(F32)
    rr = lax.broadcasted_iota(jnp.int32, (pw, pw), 0)
    cc = lax.broadcasted_iota(jnp.int32, (pw, pw), 1)
    same_head = (rr < hd) == (cc < hd)

    def bdiag(m):
        zero = jnp.zeros_like(m)
        return jnp.concatenate([jnp.where(first, m, zero), jnp.where(first, zero, m)], axis=0).astype(BF16)

    at_p = [at[:, s] for s in lanes]
    rt_p = [rt[:, s] for s in lanes]
    v_p = [v[:, s] for s in lanes]
    bh_p = [bh[:, s].astype(BF16) for s in lanes]
    kh_p = [kh[:, s].astype(BF16) for s in lanes]
    lhs = [jnp.concatenate([at_p[p], rt_p[p]], axis=0).astype(BF16) for p in pairs]
    gb = [_dot_nt(lhs[p], bdiag(bt[:, lanes[p]])) for p in pairs]
    gk = [_dot_nt(lhs[p], bdiag(kt[:, lanes[p]])) for p in pairs]
    aab = [jnp.where(strict, g[:lc], 0.0) for g in gb]
    arb = [jnp.where(incl, g[lc:], 0.0) for g in gb]
    aak = [jnp.where(strict, g[:lc], 0.0) for g in gk]
    ark = [jnp.where(incl, g[lc:], 0.0) for g in gk]
    vbd = [bdiag(m) for m in v_p]
    aakv = [_dot(aak[p].astype(BF16), vbd[p]) for p in pairs]
    x = [eye + n for n in aab]
    nk = [_dot(n.astype(BF16), bdiag(n)) for n in aab]
    for _ in range(int(np.log2(lc)) - 2):
        out = [_dot(jnp.concatenate([nk[p], x[p]], axis=0).astype(BF16), bdiag(nk[p])) for p in pairs]
        nk = [o[:lc] for o in out]
        x = [x[p] + out[p][lc:] for p in pairs]
    x = [x[p] + _dot(x[p].astype(BF16), bdiag(nk[p])) for p in pairs]
    xc = [_dot(x[p].astype(BF16), jnp.concatenate([bdiag(at_p[p]), bdiag(aakv[p])], axis=1)) for p in pairs]
    ahat = [m[:, :pw] for m in xc]
    wmat = [m[:, pw:] for m in xc]
    rhat = [rt_p[p] + _dot(arb[p].astype(BF16), bdiag(ahat[p])) for p in pairs]
    y0 = [_dot(jnp.concatenate([arb[p], ark[p]], axis=1).astype(BF16),
               jnp.concatenate([bdiag(wmat[p]), vbd[p]], axis=0)) for p in pairs]
    mab = [jnp.where(same_head, _dot(ahat[p].T.astype(BF16), bh_p[p]), 0.0).astype(BF16) for p in pairs]
    gtf = [_dot(jnp.concatenate([wmat[p].T, v_p[p].T], axis=1).astype(BF16),
                jnp.concatenate([bh_p[p], kh_p[p]], axis=0)) for p in pairs]
    for p in pairs:
        s0 = s_scr[p]
        wkv_ref[0, :, lanes[p]] = _dot_nt(rhat[p].astype(BF16), bdiag(s0)) + y0[p]
        s_scr[p] = (s0 * pend[:, lanes[p]] + _dot(s0.astype(BF16), mab[p])
                    + jnp.where(first, gtf[p][:lc], gtf[p][lc:]))


def _head_block_ones(width, head_dim):
    idx = np.arange(width) // head_dim
    return jnp.asarray(idx[:, None] == idx[None, :], BF16)


def _head_indicator(width, head_dim):
    idx = np.arange(width) // head_dim
    return jnp.asarray(idx[:, None] == np.arange(V7X_LANES)[None, :], BF16)


def _rwkv(rw_p, mu, w0s, w2s, a0s, a2s, k_k, k_a, r_k):
    tt, w = rw_p.shape
    lc = RWKV_CHUNK
    nch = tt // lc
    bw = BRANCH_W
    sub = V7X_SUBLANES
    zeros = jnp.zeros((DECAY_LORA, bw), F32)
    w2p = jnp.stack([jnp.concatenate([w2s[0], zeros]), jnp.concatenate([zeros, w2s[1]])]).astype(BF16)
    a2p = jnp.stack([jnp.concatenate([a2s[0], zeros]), jnp.concatenate([zeros, a2s[1]])]).astype(BF16)
    w0 = jnp.stack(w0s).reshape(2, 1, bw)
    a0 = jnp.stack(a0s).reshape(2, 1, bw)
    e_mat = _head_indicator(bw, RWKV_DIM)
    cidx = lambda d, t: _rwkv_chunk_index(d, t, nch)
    vec = lambda: pl.BlockSpec((1, bw), lambda d, t: (0, 0))
    per_dir = lambda rows: pl.BlockSpec((1, rows, bw), lambda d, t: (d, 0, 0))
    out_spec = pl.BlockSpec((1, lc, bw), lambda d, t: (d, cidx(d, t), 0))
    return pl.pallas_call(
        functools.partial(_rwkv_kernel, nch=nch),
        grid=(2, nch),
        in_specs=[pl.BlockSpec((lc, w), lambda d, t: (cidx(d, t), 0)),
                  pl.BlockSpec((sub, w), lambda d, t: (jnp.maximum(cidx(d, t) * (lc // sub) - 1, 0), 0)),
                  pl.BlockSpec((sub, w), lambda d, t: (jnp.minimum((cidx(d, t) + 1) * (lc // sub), tt // sub - 1), 0)),
                  pl.BlockSpec((1, w), lambda d, t: (0, 0)),
                  per_dir(1), per_dir(2 * DECAY_LORA), per_dir(1), per_dir(2 * ICLR_LORA),
                  vec(), vec(), vec(),
                  pl.BlockSpec((bw, V7X_LANES), lambda d, t: (0, 0)),
                  pl.BlockSpec((V7X_LANES, bw), lambda d, t: (0, 0))],
        out_specs=[out_spec, out_spec],
        out_shape=[jax.ShapeDtypeStruct((2, tt, bw), F32), jax.ShapeDtypeStruct((2, tt, bw), F32)],
        scratch_shapes=[pltpu.VMEM((RWKV_HEADS // 2, RWKV_DIM, 2 * RWKV_DIM), F32)],
        compiler_params=_cparams(("arbitrary", "arbitrary"), 40 * 1024 * 1024),
        name="rwkv7",
    )(rw_p, rw_p, rw_p, mu.reshape(1, w), w0, w2p, a0, a2p,
      k_k.reshape(1, bw), k_a.reshape(1, bw), r_k.reshape(1, bw), e_mat, e_mat.T)


def _odd_out_kernel(na_ref, wkv_ref, bon_ref, gd_ref, lng_ref, lnb_ref, bd_ref, w_ref, res_ref, mod_ref, gpost_ref,
                    o_ref):
    d = o_ref.shape[-1]
    bd = bd_ref[...]
    inv = 1.0 / RWKV_DIM
    wkv = wkv_ref[0] + wkv_ref[1]
    xc = wkv - _dot_exact_rhs(wkv, bd) * inv
    var = _dot_exact_rhs(xc * xc, bd) * inv
    y = xc * lax.rsqrt(var + RWKV_GN_EPS) * lng_ref[...] + lnb_ref[...] + bon_ref[0] + bon_ref[1]
    m2 = (gd_ref[...].astype(F32) * y).astype(BF16)
    out = _dot(na_ref[...], w_ref[:BRANCH_W, :]) + _dot(m2, w_ref[BRANCH_W:, :])
    gate = mod_ref[0][0:1, 2 * d:]
    o_ref[...] = res_ref[...] + gate * (_rms_rows(out) * gpost_ref[...])


def _odd_out(na_g, wkv, bon, gd, lnx_g, lnx_b, w_out, stream, mod, g_post):
    t, bw = na_g.shape
    d = stream.shape[1]
    tm = ROW_BLOCK
    off = CTX_LEN // tm
    lat = lambda w: pl.BlockSpec((tm, w), lambda i: (off + i, 0))
    vec = lambda w: pl.BlockSpec((1, w), lambda i: (0, 0))
    two = pl.BlockSpec((2, tm, bw), lambda i: (0, off + i, 0))
    return pl.pallas_call(
        _odd_out_kernel,
        grid=(t // tm,),
        in_specs=[pl.BlockSpec((tm, bw), lambda i: (i, 0)), two, two, lat(bw), vec(bw), vec(bw),
                  pl.BlockSpec((bw, bw), lambda i: (0, 0)),
                  pl.BlockSpec((2 * bw, d), lambda i: (0, 0)),
                  lat(d),
                  pl.BlockSpec((1, V7X_SUBLANES, 3 * d), lambda i: (1, 0, 0)),
                  vec(d)],
        out_specs=pl.BlockSpec((tm, d), lambda i: (i, 0)),
        out_shape=jax.ShapeDtypeStruct((t, d), F32),
        compiler_params=_cparams(("arbitrary",), 48 * 1024 * 1024),
        name="odd_out_proj",
    )(na_g, wkv, bon, gd, lnx_g.reshape(1, bw), lnx_b.reshape(1, bw), _head_block_ones(bw, RWKV_DIM), w_out,
      stream, mod, g_post.reshape(1, d))


def _rope_tables(t):
    pos = jnp.arange(t)
    rowp = (pos // GRID_W).astype(F32)
    colp = (pos % GRID_W).astype(F32)
    n_freq = RET_DK // 4
    inv = ROPE_THETA ** (-jnp.arange(n_freq, dtype=F32) / n_freq)
    ang = jnp.concatenate([rowp[:, None] * inv, colp[:, None] * inv], axis=-1)
    cos = jnp.repeat(jnp.cos(ang), 2, axis=-1)
    sin = jnp.repeat(jnp.sin(ang), 2, axis=-1) * jnp.tile(jnp.asarray([-1.0, 1.0], F32), RET_DK // 2)
    cos = jnp.concatenate([jnp.ones((CTX_LEN, RET_DK), F32), cos], axis=0)
    sin = jnp.concatenate([jnp.zeros((CTX_LEN, RET_DK), F32), sin], axis=0)
    return cos, sin


def kernel(x, c, ctx, c_ctx, w_mod, b_mod, g_pre, g_post, ev_w_in, ev_w_out, ret_decay_fwd, ret_decay_bwd, gqa_q_norm, gqa_k_norm, od_w_in, od_w_out, na_rpb, rwkv_shift_mu, rwkv_w0_fwd, rwkv_w2_fwd, rwkv_w0_bwd, rwkv_w2_bwd, rwkv_a0_fwd, rwkv_a2_fwd, rwkv_a0_bwd, rwkv_a2_bwd, rwkv_k_k, rwkv_k_a, rwkv_r_k, rwkv_lnx_g, rwkv_lnx_b):
    assert x.shape[0] == 1 and DEPTH == 2 and RET_DK == GQA_DIM
    t = x.shape[1]
    tt = t + CTX_LEN
    assert t % ROW_BLOCK == 0 and tt % ATT_KV_BLOCK == 0 and t % (GRID_W * NA_ROWS_PER_STEP) == 0
    x2, ctx2 = x[0], ctx[0]
    mod = _modulation(c, c_ctx, w_mod, b_mod)
    cos_t, sin_t = _rope_tables(t)

    qa, ka, va, qb, kbt, vb, ga, gb = _even_in_proj(x2, ctx2, mod, g_pre[0], ev_w_in[0].astype(BF16), cos_t, sin_t,
                                                    gqa_q_norm[0], gqa_k_norm[0])
    ret_g = _retention(qa, ka, va, ga, ret_decay_fwd[0], ret_decay_bwd[0])
    att_lat = _gqa_attention(qb, kbt, vb, gb, q_row0=CTX_LEN, n_q=t, n_keys=tt, kv_block=ATT_KV_BLOCK, out_rows=t)
    att_ctx = _gqa_attention(qb, kbt, vb, gb, q_row0=0, n_q=CTX_LEN, n_keys=CTX_LEN, kv_block=CTX_LEN,
                             out_rows=CTX_LEN)
    stream1 = _out_proj(ret_g, att_ctx, att_lat, ev_w_out[0].astype(BF16), ctx2, x2, mod, 0, g_post[0])

    q, k, v, rw_p, gc, gd = _odd_in_proj(stream1, mod, g_pre[1], od_w_in[0].astype(BF16))
    na_g = _neighbourhood_attention(q, k, v, gc, na_rpb[0])
    wkv, bon = _rwkv(rw_p, rwkv_shift_mu[0], (rwkv_w0_fwd[0], rwkv_w0_bwd[0]), (rwkv_w2_fwd[0], rwkv_w2_bwd[0]),
                     (rwkv_a0_fwd[0], rwkv_a0_bwd[0]), (rwkv_a2_fwd[0], rwkv_a2_bwd[0]),
                     rwkv_k_k[0], rwkv_k_a[0], rwkv_r_k[0].reshape(-1))
    out = _odd_out(na_g, wkv, bon, gd, rwkv_lnx_g[0], rwkv_lnx_b[0], od_w_out[0].astype(BF16), stream1, mod,
                   g_post[1])
    return out[None]
```

```python
import functools

import jax
import jax.numpy as jnp
import numpy as np
from jax import lax
from jax.experimental import pallas as pl
from jax.experimental.pallas import tpu as pltpu

F32 = jnp.float32
BF16 = jnp.bfloat16

D_MODEL = 1024
DEPTH = 2
GRID_W = 64
CTX_LEN = 256
BRANCH_W = D_MODEL
RET_HEADS = 4
RET_DK = 128
RET_DV = BRANCH_W // RET_HEADS
RET_CHUNK = 128
GQA_HEADS = 8
GQA_KV_HEADS = 2
GQA_DIM = BRANCH_W // GQA_HEADS
GQA_GROUP = GQA_HEADS // GQA_KV_HEADS
NA_HEADS = 16
NA_DIM = BRANCH_W // NA_HEADS
NA_WIN_ROWS = 8
NA_WIN_COLS = 16
RWKV_HEADS = 16
RWKV_DIM = BRANCH_W // RWKV_HEADS
DECAY_LORA = 64
ICLR_LORA = 64
ROPE_THETA = 10000.0
NORM_EPS = 1e-6
RWKV_GN_EPS = 64e-5
SHIFT_W = 3 * BRANCH_W + 2 * DECAY_LORA + 2 * ICLR_LORA
EVEN_IN = 2 * RET_HEADS * RET_DK + BRANCH_W + GQA_HEADS * GQA_DIM + 2 * GQA_KV_HEADS * GQA_DIM + 2 * BRANCH_W
ODD_IN = 3 * BRANCH_W + SHIFT_W + 2 * BRANCH_W

V7X_LANES = 128
V7X_SUBLANES = 8
V7X_VMEM_BYTES = 64 * 1024 * 1024

ROW_BLOCK = CTX_LEN
RWKV_CHUNK = 64
RWKV_INV_BASE = 8
ATT_Q_BLOCK = 256
ATT_KV_BLOCK = 1280
LOG2_E = 1.4426950408889634
NA_BLOCK_ROWS = 4
NA_UNION_ROWS = 12
NA_BLOCKS_PER_STEP = 2
ODD_OUT_ROW_GROUPS = 2
MASK_VALUE = -1e30


def _vmem_limit(nbytes):
    return int(min(V7X_VMEM_BYTES - 4 * 1024 * 1024, max(32 * 1024 * 1024, nbytes)))


def _cparams(sem, vmem_bytes):
    return pltpu.CompilerParams(dimension_semantics=sem, vmem_limit_bytes=_vmem_limit(vmem_bytes))


def _silu(x):
    return x / (1.0 + jnp.exp(-x))


def _sigmoid(x):
    return 1.0 / (1.0 + jnp.exp(-x))


def _dot(a, b):
    return jnp.dot(a, b, preferred_element_type=F32)


def _dot_nt(a, b):
    return lax.dot_general(a, b, (((1,), (1,)), ((), ())), preferred_element_type=F32)


def _split2(x):
    hi = x.astype(BF16)
    lo = (x - hi.astype(F32)).astype(BF16)
    return hi, lo


def _dot_split_rhs(a_bf16, x):
    hi, lo = _split2(x)
    return _dot(a_bf16, hi) + _dot(a_bf16, lo)


def _head_sums(xs, e, et):
    n = xs[0].shape[0]

    def stacked_dot(vals, w):
        parts = [t for x in vals for t in _split2(x)]
        y = _dot(jnp.concatenate(parts, axis=0), w)
        return [y[2 * i * n:(2 * i + 1) * n] + y[(2 * i + 1) * n:(2 * i + 2) * n] for i in range(len(vals))]

    return stacked_dot(stacked_dot(xs, e), et)


def _rms_rows(x):
    return x * lax.rsqrt(jnp.mean(x * x, axis=-1, keepdims=True) + NORM_EPS)


def _rope(t, cos, sin_signed, even):
    nxt = pltpu.roll(t, t.shape[1] - 1, 1)
    prv = pltpu.roll(t, 1, 1)
    return t * cos + jnp.where(even, nxt, prv) * sin_signed


def _mod_kernel(cc_ref, w_ref, b_ref, o_ref):
    s = _silu(cc_ref[...])
    o_ref[0] = jnp.dot(s, w_ref[0], preferred_element_type=F32, precision=lax.Precision.HIGHEST) + b_ref[0]


def _modulation(c, c_ctx, w_mod, b_mod):
    d = c.shape[-1]
    cc = jnp.concatenate([c[:1], c_ctx[None, :], jnp.zeros((V7X_SUBLANES - 2, d), F32)], axis=0)
    return pl.pallas_call(
        _mod_kernel,
        grid=(DEPTH, 3),
        in_specs=[pl.BlockSpec((V7X_SUBLANES, d), lambda l, j: (0, 0)),
                  pl.BlockSpec((1, d, d), lambda l, j: (l, 0, j)),
                  pl.BlockSpec((1, 1, d), lambda l, j: (l, 0, j))],
        out_specs=pl.BlockSpec((1, V7X_SUBLANES, d), lambda l, j: (l, 0, j)),
        out_shape=jax.ShapeDtypeStruct((DEPTH, V7X_SUBLANES, 3 * d), F32),
        compiler_params=_cparams(("arbitrary", "arbitrary"), 40 * 1024 * 1024),
        name="modulation",
    )(cc, w_mod, b_mod.reshape(DEPTH, 1, 3 * d))


def _adaln(xb, mod, is_ctx, g_pre):
    d = xb.shape[-1]
    m = jnp.where(is_ctx, mod[1:2, :], mod[0:1, :])
    shift, scale = m[:, :d], m[:, d:2 * d]
    return (_rms_rows(xb) * g_pre) * (1.0 + scale) + shift


def _even_in_kernel(x_ref, ctx_ref, mod_ref, gpre_ref, w_ref, cos_ref, sin_ref, qn_ref, kn_ref,
                    qa_ref, ka_ref, va_ref, qb_ref, kbt_ref, vb_ref, ga_ref, gb_ref):
    is_ctx = pl.program_id(0) == 0
    xb = jnp.where(is_ctx, ctx_ref[...], x_ref[...])
    hb = _adaln(xb, mod_ref[0], is_ctx, gpre_ref[...]).astype(BF16)
    cos, sin_s = cos_ref[...], sin_ref[...]
    even = (lax.broadcasted_iota(jnp.int32, cos.shape, 1) & 1) == 0
    o = 0

    def seg(width):
        nonlocal o
        y = _dot(hb, w_ref[:, o:o + width])
        o += width
        return y

    y = seg(RET_HEADS * RET_DK)
    for h in range(RET_HEADS):
        sl = slice(h * RET_DK, (h + 1) * RET_DK)
        qa_ref[:, sl] = _rope(y[:, sl], cos, sin_s, even).astype(BF16)
    y = seg(RET_HEADS * RET_DK)
    for h in range(RET_HEADS):
        sl = slice(h * RET_DK, (h + 1) * RET_DK)
        ka_ref[:, sl] = _rope(y[:, sl] * RET_DK ** -0.5, cos, sin_s, even).astype(BF16)
    va_ref[...] = seg(BRANCH_W).astype(BF16)
    y = seg(GQA_HEADS * GQA_DIM)
    for h in range(GQA_HEADS):
        sl = slice(h * GQA_DIM, (h + 1) * GQA_DIM)
        t = _rms_rows(y[:, sl]) * qn_ref[...]
        qb_ref[:, sl] = (_rope(t, cos, sin_s, even) * (GQA_DIM ** -0.5 * LOG2_E)).astype(BF16)
    y = seg(GQA_KV_HEADS * GQA_DIM)
    for h in range(GQA_KV_HEADS):
        sl = slice(h * GQA_DIM, (h + 1) * GQA_DIM)
        t = _rope(_rms_rows(y[:, sl]) * kn_ref[...], cos, sin_s, even)
        kbt_ref[sl, :] = t.T.astype(BF16)
    y = seg(GQA_KV_HEADS * GQA_DIM)
    for h in range(GQA_KV_HEADS):
        vb_ref[:, 2 * h * GQA_DIM:(2 * h + 1) * GQA_DIM] = y[:, h * GQA_DIM:(h + 1) * GQA_DIM].astype(BF16)
        vb_ref[:, (2 * h + 1) * GQA_DIM:(2 * h + 2) * GQA_DIM] = jnp.ones((y.shape[0], GQA_DIM), BF16)
    ga_ref[...] = _silu(seg(BRANCH_W)).astype(BF16)
    gb_ref[...] = _silu(seg(BRANCH_W)).astype(BF16)


def _even_in_proj(x, ctx, mod, g_pre, w_in, cos_t, sin_t, q_norm, k_norm):
    t, d = x.shape
    tt = t + CTX_LEN
    nblk = tt // ROW_BLOCK
    tm = ROW_BLOCK
    kvw = GQA_KV_HEADS * GQA_DIM
    row = lambda w: pl.BlockSpec((tm, w), lambda i: (i, 0))
    const = lambda shape: pl.BlockSpec(shape, lambda i: tuple(0 for _ in shape))
    outs = [((tt, RET_HEADS * RET_DK), row(RET_HEADS * RET_DK)),
            ((tt, RET_HEADS * RET_DK), row(RET_HEADS * RET_DK)),
            ((tt, BRANCH_W), row(BRANCH_W)),
            ((tt, GQA_HEADS * GQA_DIM), row(GQA_HEADS * GQA_DIM)),
            ((kvw, tt), pl.BlockSpec((kvw, tm), lambda i: (0, i))),
            ((tt, 2 * kvw), row(2 * kvw)),
            ((tt, BRANCH_W), row(BRANCH_W)),
            ((tt, BRANCH_W), row(BRANCH_W))]
    return pl.pallas_call(
        _even_in_kernel,
        grid=(nblk,),
        in_specs=[pl.BlockSpec((tm, d), lambda i: (jnp.maximum(i - 1, 0), 0)),
                  const((CTX_LEN, d)),
                  pl.BlockSpec((1, V7X_SUBLANES, 3 * d), lambda i: (0, 0, 0)),
                  const((1, d)),
                  const((d, EVEN_IN)),
                  row(RET_DK), row(RET_DK),
                  const((1, GQA_DIM)), const((1, GQA_DIM))],
        out_specs=[s for _, s in outs],
        out_shape=[jax.ShapeDtypeStruct(shp, BF16) for shp, _ in outs],
        compiler_params=_cparams(("arbitrary",), 2 * d * EVEN_IN * 2 + 16 * 1024 * 1024),
        name="even_in_proj",
    )(x, ctx, mod, g_pre.reshape(1, d), w_in, cos_t, sin_t, q_norm.reshape(1, -1), k_norm.reshape(1, -1))


def _log_sigmoid(x):
    return jnp.minimum(x, 0.0) - jnp.log(1.0 + jnp.exp(-jnp.abs(x)))


def _ret_bwd_chunk(t, nchunks):
    nctx = CTX_LEN // RET_CHUNK
    return jnp.where(t < nctx, nctx - 1 - t, nchunks - 1 - (t - nctx))


def _ret_state_kernel(dec_ref, k_ref, v_ref, sb_ref, s_scr):
    c = RET_CHUNK

    @pl.when(pl.program_id(0) == 0)
    def _():
        s_scr[...] = jnp.zeros_like(s_scr)

    lg = _log_sigmoid(dec_ref[...])
    pos = lax.broadcasted_iota(jnp.int32, (c, RET_DK), 0).astype(F32)
    for h in range(RET_HEADS):
        lgb = lg[RET_HEADS + h:RET_HEADS + h + 1, :]
        s_old = s_scr[h]
        sb_ref[0, h] = s_old.astype(BF16)
        kz = k_ref[:, h * RET_DK:(h + 1) * RET_DK].astype(F32) * jnp.exp(pos * lgb)
        u = _dot(kz.T.astype(BF16), v_ref[:, h * RET_DV:(h + 1) * RET_DV])
        s_scr[h] = jnp.exp(c * lgb[:, :1]) * s_old + u


def _ret_out_kernel(dec_ref, q_ref, k_ref, v_ref, g_ref, sb_ref, o_ref, s_scr):
    c = RET_CHUNK

    @pl.when(pl.program_id(0) == 0)
    def _():
        s_scr[...] = jnp.zeros_like(s_scr)

    lg = _log_sigmoid(dec_ref[...])
    ii = lax.broadcasted_iota(jnp.int32, (c, c), 0)
    jj = lax.broadcasted_iota(jnp.int32, (c, c), 1)
    dlt = (ii - jj).astype(F32)
    pos = lax.broadcasted_iota(jnp.int32, (c, RET_DK), 0).astype(F32)
    hs = range(RET_HEADS)
    lgf = [lg[h:h + 1, :] for h in hs]
    lgb = [lg[RET_HEADS + h:RET_HEADS + h + 1, :] for h in hs]
    dec = [jnp.where(dlt > 0, jnp.exp(jnp.maximum(dlt, 0.0) * lgf[h]),
                     jnp.where(dlt < 0, jnp.exp(jnp.maximum(-dlt, 0.0) * lgb[h]), 2.0)) for h in hs]
    q = [q_ref[:, h * RET_DK:(h + 1) * RET_DK] for h in hs]
    k = [k_ref[:, h * RET_DK:(h + 1) * RET_DK] for h in hs]
    v = [v_ref[:, h * RET_DV:(h + 1) * RET_DV] for h in hs]
    s_old = [s_scr[h] for h in hs]
    p = [(_dot_nt(q[h], k[h]) * dec[h]).astype(BF16) for h in hs]
    qf = [q[h].astype(F32) for h in hs]
    ret = [_dot(p[h], v[h])
           + _dot((qf[h] * jnp.exp((pos + 1.0) * lgf[h])).astype(BF16), s_old[h].astype(BF16))
           + _dot((qf[h] * jnp.exp((c - pos) * lgb[h])).astype(BF16), sb_ref[0, h]) for h in hs]
    kzt = [(k[h].astype(F32) * jnp.exp((c - 1.0 - pos) * lgf[h])).T.astype(BF16) for h in hs]
    upd = [_dot(kzt[h], v[h]) for h in hs]
    xc = [ret[h] - jnp.mean(ret[h], axis=-1, keepdims=True) for h in hs]
    y = [xc[h] * lax.rsqrt(jnp.mean(xc[h] * xc[h], axis=-1, keepdims=True) + NORM_EPS) for h in hs]
    for h in hs:
        sl = slice(h * RET_DV, (h + 1) * RET_DV)
        o_ref[:, sl] = (g_ref[:, sl].astype(F32) * y[h]).astype(BF16)
        s_scr[h] = jnp.exp(c * lgf[h][:, :1]) * s_old[h] + upd[h]


def _retention(qa, ka, va, ga, dec_f, dec_b):
    tt = qa.shape[0]
    c = RET_CHUNK
    n = tt // c
    dec = jnp.broadcast_to(jnp.concatenate([dec_f, dec_b]).astype(F32)[:, None], (2 * RET_HEADS, V7X_LANES))
    kw, vw = RET_HEADS * RET_DK, BRANCH_W
    dec_spec = pl.BlockSpec((2 * RET_HEADS, V7X_LANES), lambda t: (0, 0))
    state_shape = (RET_HEADS, RET_DK, RET_DV)
    sb = pl.pallas_call(
        _ret_state_kernel,
        grid=(n,),
        in_specs=[dec_spec,
                  pl.BlockSpec((c, kw), lambda t: (_ret_bwd_chunk(t, n), 0)),
                  pl.BlockSpec((c, vw), lambda t: (_ret_bwd_chunk(t, n), 0))],
        out_specs=pl.BlockSpec((1,) + state_shape, lambda t: (_ret_bwd_chunk(t, n), 0, 0, 0)),
        out_shape=jax.ShapeDtypeStruct((n,) + state_shape, BF16),
        scratch_shapes=[pltpu.VMEM(state_shape, F32)],
        compiler_params=_cparams(("arbitrary",), 32 * 1024 * 1024),
        name="retention_state",
    )(dec, ka, va)
    return pl.pallas_call(
        _ret_out_kernel,
        grid=(n,),
        in_specs=[dec_spec,
                  pl.BlockSpec((c, kw), lambda t: (t, 0)),
                  pl.BlockSpec((c, kw), lambda t: (t, 0)),
                  pl.BlockSpec((c, vw), lambda t: (t, 0)),
                  pl.BlockSpec((c, vw), lambda t: (t, 0)),
                  pl.BlockSpec((1,) + state_shape, lambda t: (t, 0, 0, 0))],
        out_specs=pl.BlockSpec((c, vw), lambda t: (t, 0)),
        out_shape=jax.ShapeDtypeStruct((tt, vw), BF16),
        scratch_shapes=[pltpu.VMEM(state_shape, F32)],
        compiler_params=_cparams(("arbitrary",), 32 * 1024 * 1024),
        name="retention_out",
    )(dec, qa, ka, va, ga, sb)


def _gqa_kernel(q_ref, kt_ref, v_ref, g_ref, o_ref, s0_scr, s1_scr, *, kv_block, n_kv):
    tq = q_ref.shape[0]
    heads = [slice(h * GQA_DIM, (h + 1) * GQA_DIM) for h in range(GQA_GROUP)]
    q = jnp.concatenate([q_ref[:, sl] for sl in heads], axis=0)
    rows = GQA_GROUP * tq

    def scores(j, s_ref):
        start = pl.multiple_of(j * kv_block, V7X_LANES)
        s_ref[...] = _dot(q, kt_ref[:, pl.ds(start, kv_block)])

    def softmax_pv(j, s_ref, carry):
        m, acc = carry
        start = pl.multiple_of(j * kv_block, V7X_LANES)
        s = s_ref[...]
        m_new = jnp.maximum(m, jnp.max(s, axis=-1, keepdims=True))
        p = jnp.exp2(s - m_new).astype(BF16)
        acc = jnp.exp2(m - m_new) * acc + _dot(p, v_ref[pl.ds(start, kv_block), :])
        return m_new, acc

    def body(i, carry):
        j = 2 * i
        scores(j + 1, s1_scr)
        carry = softmax_pv(j, s0_scr, carry)
        scores(j + 2, s0_scr)
        return softmax_pv(j + 1, s1_scr, carry)

    assert n_kv % 2 == 1
    carry = (jnp.full((rows, 1), MASK_VALUE, F32), jnp.zeros((rows, 2 * GQA_DIM), F32))
    scores(0, s0_scr)
    carry = lax.fori_loop(0, n_kv // 2, body, carry)
    _, acc = softmax_pv(n_kv - 1, s0_scr, carry)
    out = acc[:, :GQA_DIM] / acc[:, GQA_DIM:]
    for h, sl in enumerate(heads):
        o_ref[:, sl] = (g_ref[:, sl].astype(F32) * out[h * tq:(h + 1) * tq]).astype(BF16)


def _gqa_attention(qb, kbt, vb, gb, *, q_row0, n_q, n_keys, kv_block, out_rows):
    gw = GQA_GROUP * GQA_DIM
    tq = ATT_Q_BLOCK
    qoff = q_row0 // tq
    return pl.pallas_call(
        functools.partial(_gqa_kernel, kv_block=kv_block, n_kv=n_keys // kv_block),
        grid=(GQA_KV_HEADS, n_q // tq),
        in_specs=[pl.BlockSpec((tq, gw), lambda g, i: (qoff + i, g)),
                  pl.BlockSpec((GQA_DIM, n_keys), lambda g, i: (g, 0)),
                  pl.BlockSpec((n_keys, 2 * GQA_DIM), lambda g, i: (0, g)),
                  pl.BlockSpec((tq, gw), lambda g, i: (qoff + i, g))],
        out_specs=pl.BlockSpec((tq, gw), lambda g, i: (i, g)),
        out_shape=jax.ShapeDtypeStruct((out_rows, GQA_HEADS * GQA_DIM), BF16),
        scratch_shapes=[pltpu.VMEM((GQA_GROUP * tq, kv_block), F32)] * 2,
        compiler_params=_cparams(("arbitrary", "arbitrary"), 56 * 1024 * 1024),
        name="gqa_attention",
    )(qb, kbt, vb, gb)


def _out_proj_kernel(m1_ref, m2c_ref, m2l_ref, w_ref, resc_ref, resl_ref, mod_ref, gpost_ref, o_ref):
    d = o_ref.shape[-1]
    is_ctx = pl.program_id(0) == 0
    m2 = jnp.where(is_ctx, m2c_ref[...], m2l_ref[...])
    y = _dot(m1_ref[...], w_ref[:BRANCH_W, :]) + _dot(m2, w_ref[BRANCH_W:, :])
    mod = mod_ref[0]
    gate = jnp.where(is_ctx, mod[1:2, 2 * d:], mod[0:1, 2 * d:])
    res = jnp.where(is_ctx, resc_ref[...], resl_ref[...])
    o_ref[...] = res + gate * (_rms_rows(y) * gpost_ref[...])


def _out_proj(m1, m2_ctx, m2_lat, w_out, res_ctx, res_lat, mod, layer, g_post):
    tt = m1.shape[0]
    d = res_lat.shape[1]
    tm = ROW_BLOCK
    row = lambda w: pl.BlockSpec((tm, w), lambda i: (i, 0))
    lat = lambda w: pl.BlockSpec((tm, w), lambda i: (jnp.maximum(i - 1, 0), 0))
    ctx = lambda w: pl.BlockSpec((CTX_LEN, w), lambda i: (0, 0))
    return pl.pallas_call(
        _out_proj_kernel,
        grid=(tt // tm,),
        in_specs=[row(BRANCH_W), ctx(BRANCH_W), lat(BRANCH_W),
                  pl.BlockSpec((2 * BRANCH_W, d), lambda i: (0, 0)),
                  ctx(d), lat(d),
                  pl.BlockSpec((1, V7X_SUBLANES, 3 * d), lambda i: (layer, 0, 0)),
                  pl.BlockSpec((1, d), lambda i: (0, 0))],
        out_specs=row(d),
        out_shape=jax.ShapeDtypeStruct((tt, d), F32),
        compiler_params=_cparams(("arbitrary",), 40 * 1024 * 1024),
        name="out_proj",
    )(m1, m2_ctx, m2_lat, w_out, res_ctx, res_lat, mod, g_post.reshape(1, d))


def _odd_in_kernel(s_ref, mod_ref, gpre_ref, w_ref, q_ref, k_ref, v_ref, rw_ref, gc_ref, gd_ref):
    is_ctx = pl.program_id(0) == 0
    hb = _adaln(s_ref[...], mod_ref[0], is_ctx, gpre_ref[...]).astype(BF16)
    o = 0

    def seg(width):
        nonlocal o
        y = _dot(hb, w_ref[:, o:o + width])
        o += width
        return y

    q_ref[...] = (seg(BRANCH_W) * NA_DIM ** -0.5).astype(BF16)
    k_ref[...] = seg(BRANCH_W).astype(BF16)
    v_ref[...] = seg(BRANCH_W).astype(BF16)
    rw_ref[...] = seg(SHIFT_W)
    gc_ref[...] = _silu(seg(BRANCH_W)).astype(BF16)
    gd_ref[...] = _silu(seg(BRANCH_W)).astype(BF16)


def _odd_in_proj(stream, mod, g_pre, w_in):
    tt, d = stream.shape
    tm = ROW_BLOCK
    row = lambda w: pl.BlockSpec((tm, w), lambda i: (i, 0))
    widths = [(BRANCH_W, BF16), (BRANCH_W, BF16), (BRANCH_W, BF16), (SHIFT_W, F32), (BRANCH_W, BF16), (BRANCH_W, BF16)]
    return pl.pallas_call(
        _odd_in_kernel,
        grid=(tt // tm,),
        in_specs=[row(d),
                  pl.BlockSpec((1, V7X_SUBLANES, 3 * d), lambda i: (1, 0, 0)),
                  pl.BlockSpec((1, d), lambda i: (0, 0)),
                  pl.BlockSpec((d, ODD_IN), lambda i: (0, 0), pipeline_mode=pl.Buffered(1))],
        out_specs=[row(w) for w, _ in widths],
        out_shape=[jax.ShapeDtypeStruct((tt, w), dt) for w, dt in widths],
        compiler_params=_cparams(("arbitrary",), d * ODD_IN * 2 + 24 * 1024 * 1024),
        name="odd_in_proj",
    )(stream, mod, g_pre.reshape(1, d), w_in)


def _na_kernel(*refs, rows):
    nb = NA_BLOCKS_PER_STEP
    q_refs, g_refs = refs[:nb], refs[nb:2 * nb]
    k_ref, v_ref, bias_ref, o_ref = refs[2 * nb:]
    span = NA_UNION_ROWS * GRID_W
    bq = NA_BLOCK_ROWS * GRID_W
    first = lax.broadcasted_iota(jnp.int32, (bq, 2 * NA_DIM), 1) < NA_DIM
    kc, vc = k_ref[:CTX_LEN, :], v_ref[:CTX_LEN, :]
    blocks = range(nb)
    lhs, ku, vu, cls = [], [], [], []
    for b in blocks:
        qr0 = (pl.program_id(1) * nb + b) * NA_BLOCK_ROWS
        u0 = jnp.clip(qr0 - NA_WIN_ROWS // 2, 0, rows - NA_UNION_ROWS)
        cls.append(jnp.where(qr0 == 0, 0, jnp.where(qr0 == rows - NA_BLOCK_ROWS, 2, 1)))
        start = pl.multiple_of(CTX_LEN + u0 * GRID_W, GRID_W)
        ku.append(k_ref[pl.ds(start, span), :])
        vu.append(v_ref[pl.ds(start, span), :])
        q = q_refs[b][...]
        zero = jnp.zeros_like(q)
        lhs.append(jnp.concatenate([jnp.where(first, q, zero), jnp.where(first, zero, q)], axis=0))
    sw = [_dot_nt(lhs[b], ku[b]) + bias_ref[cls[b]].reshape(2 * bq, span) for b in blocks]
    sc = [_dot_nt(lhs[b], kc) for b in blocks]
    m = [jnp.maximum(jnp.max(sw[b], axis=-1, keepdims=True), jnp.max(sc[b], axis=-1, keepdims=True)) for b in blocks]
    pw = [jnp.exp(sw[b] - m[b]) for b in blocks]
    pc = [jnp.exp(sc[b] - m[b]) for b in blocks]
    l = [jnp.sum(pw[b], axis=-1, keepdims=True) + jnp.sum(pc[b], axis=-1, keepdims=True) for b in blocks]
    o = [(_dot(pw[b].astype(BF16), vu[b]) + _dot(pc[b].astype(BF16), vc)) / l[b] for b in blocks]
    for b in blocks:
        out = jnp.where(first, o[b][:bq], o[b][bq:])
        o_ref[b * bq:(b + 1) * bq, :] = (g_refs[b][...].astype(F32) * out).astype(BF16)


def _na_bias_table(rpb):
    half = NA_WIN_ROWS // 2
    cols = np.arange(GRID_W)
    c0 = np.clip(cols - NA_WIN_COLS // 2, 0, GRID_W - NA_WIN_COLS)
    valid_c = (cols[None, :] >= c0[:, None]) & (cols[None, :] < c0[:, None] + NA_WIN_COLS)
    dc = np.clip(cols[None, :] - cols[:, None] + NA_WIN_COLS - 1, 0, 2 * NA_WIN_COLS - 2)
    a = np.arange(NA_BLOCK_ROWS)[:, None]
    i = np.arange(NA_UNION_ROWS)[None, :]
    dr = np.stack([i - a, i - a - half, i - a - (NA_UNION_ROWS - NA_BLOCK_ROWS)])
    w0 = np.stack([0 * a, a, (NA_UNION_ROWS - NA_WIN_ROWS) + 0 * a])
    valid_r = (i[None] >= w0) & (i[None] < w0 + NA_WIN_ROWS)
    masked = 2 * NA_WIN_ROWS - 1
    tile_idx = np.where(valid_r, np.clip(dr + NA_WIN_ROWS - 1, 0, masked - 1), masked)
    tiles = jnp.where(jnp.asarray(valid_c)[None, None], rpb[:, :, dc], MASK_VALUE)
    tiles = jnp.concatenate([tiles, jnp.full((NA_HEADS, 1, GRID_W, GRID_W), MASK_VALUE, F32)], axis=1)
    t = tiles[:, tile_idx]
    t = t.reshape((NA_HEADS // 2, 2) + t.shape[1:])
    t = jnp.transpose(t, (0, 2, 1, 3, 5, 4, 6))
    return t.reshape(NA_HEADS // 2, 3, 2, NA_BLOCK_ROWS * GRID_W, NA_UNION_ROWS * GRID_W).astype(F32)


def _neighbourhood_attention(q, k, v, gc, rpb):
    tt = q.shape[0]
    t = tt - CTX_LEN
    rows = t // GRID_W
    nb = NA_BLOCKS_PER_STEP
    assert NA_BLOCK_ROWS == NA_WIN_ROWS // 2 and rows >= NA_UNION_ROWS and rows % (nb * NA_BLOCK_ROWS) == 0
    bias = _na_bias_table(rpb)
    pw = 2 * NA_DIM
    bq = NA_BLOCK_ROWS * GRID_W
    qoff = CTX_LEN // bq
    assert CTX_LEN % bq == 0
    blk = lambda b: pl.BlockSpec((bq, pw), lambda p, i: (qoff + i * nb + b, p))
    resident = pl.BlockSpec((tt, pw), lambda p, i: (0, p))
    return pl.pallas_call(
        functools.partial(_na_kernel, rows=rows),
        grid=(NA_HEADS // 2, rows // (nb * NA_BLOCK_ROWS)),
        in_specs=[blk(b) for b in range(nb)] + [blk(b) for b in range(nb)] + [
            resident, resident,
            pl.BlockSpec((None,) + bias.shape[1:], lambda p, i: (p, 0, 0, 0, 0))],
        out_specs=pl.BlockSpec((nb * bq, pw), lambda p, i: (i, p)),
        out_shape=jax.ShapeDtypeStruct((t, BRANCH_W), BF16),
        compiler_params=_cparams(("arbitrary", "arbitrary"), 48 * 1024 * 1024),
        name="neighbourhood_attention",
    )(*([q] * nb + [gc] * nb + [k, v, bias]))


def _rwkv_chunk_index(d, t, nch):
    nctx = CTX_LEN // RWKV_CHUNK
    return t if d == 0 else jnp.where(t < nctx, nctx - 1 - t, nch - 1 - (t - nctx))


def _rwkv_prepare(d, c, nch, main_ref, prev_ref, next_ref, mu_ref, w0_ref, w2_ref, a0_ref, a2_ref, kk_ref, ka_ref,
                  rk_ref, e_ref, et_ref, bon_ref):
    lc = RWKV_CHUNK
    nctx = CTX_LEN // lc
    sgn = 1 - 2 * d
    p = main_ref[...]
    row = lax.broadcasted_iota(jnp.int32, p.shape, 0)
    first_zero = jnp.logical_or(c == 0, c == nctx)
    last_zero = jnp.logical_or(c == nctx - 1, c == nch - 1)
    pr = jnp.where(first_zero, 0.0, prev_ref[V7X_SUBLANES - 1:V7X_SUBLANES, :])
    nx = jnp.where(last_zero, 0.0, next_ref[0:1, :])
    prev = jnp.where(row == 0, pr, pltpu.roll(p, 1, 0))
    nxt = jnp.where(row == lc - 1, nx, pltpu.roll(p, lc - 1, 0))
    z = p + (0.5 * (prev + nxt) - p) * mu_ref[...]

    bw = BRANCH_W
    r, k, v = z[:, :bw], z[:, bw:2 * bw], z[:, 2 * bw:3 * bw]
    zw = z[:, 3 * bw:3 * bw + 2 * DECAY_LORA]
    za = z[:, 3 * bw + 2 * DECAY_LORA:]
    lw = w0_ref[d] + _dot(jnp.tanh(zw).astype(BF16), w2_ref[d])
    ld = -float(np.exp(-0.5)) * _sigmoid(lw)
    asig = _sigmoid(a0_ref[d] + _dot(za.astype(BF16), a2_ref[d]))
    kk = k * kk_ref[...]
    kd = k * (1.0 + (asig - 1.0) * ka_ref[...])
    kk_sq, rkd = _head_sums([kk * kk, r * kd * rk_ref[...]], e_ref[...], et_ref[...])
    kkn = kk / jnp.maximum(jnp.sqrt(kk_sq), 1e-12)
    a_vec = -kkn
    b_vec = kkn * asig
    bon_ref[...] = rkd * v

    ti = lax.broadcasted_iota(jnp.int32, (lc, lc), 0)
    si = lax.broadcasted_iota(jnp.int32, (lc, lc), 1)
    cum = _dot_split_rhs(jnp.where(sgn * (ti - si) >= 0, 1.0, 0.0).astype(BF16), ld)
    tot = jnp.sum(ld, axis=0, keepdims=True)
    rem = jnp.exp(tot - cum)
    pinv = jnp.exp(-cum)
    rt = r * jnp.exp(cum)
    kt = kd * pinv
    bt = b_vec * pinv
    at = a_vec * jnp.exp(cum - ld)
    bh = b_vec * rem
    kh = kd * rem
    pend = jnp.exp(tot)
    return dict(at=at, rt=rt, bt=bt, kt=kt, bh=bh, kh=kh, v=v, pend=pend)


def _rwkv_kernel(main_f, prev_f, next_f, main_b, prev_b, next_b, mu_ref, w0_ref, w2_ref, a0_ref, a2_ref, kk_ref,
                 ka_ref, rk_ref, e_ref, et_ref, wkv_f, wkv_b, bon_f, bon_b, s_scr, *, nch):
    lc = RWKV_CHUNK
    hd = RWKV_DIM
    t = pl.program_id(0)

    @pl.when(t == 0)
    def _():
        s_scr[...] = jnp.zeros_like(s_scr)

    shared = (mu_ref, w0_ref, w2_ref, a0_ref, a2_ref, kk_ref, ka_ref, rk_ref, e_ref, et_ref)
    rows = [_rwkv_prepare(0, _rwkv_chunk_index(0, t, nch), nch, main_f, prev_f, next_f, *shared, bon_f),
            _rwkv_prepare(1, _rwkv_chunk_index(1, t, nch), nch, main_b, prev_b, next_b, *shared, bon_b)]
    wkv_refs = (wkv_f, wkv_b)

    pw = 2 * hd
    lane = lax.broadcasted_iota(jnp.int32, (lc, pw), 1)
    tok = lax.broadcasted_iota(jnp.int32, (lc, pw), 0)
    first = lane < hd
    fwd_diff = tok - (lane & (hd - 1))
    eye = (fwd_diff == 0).astype(F32)
    rr = lax.broadcasted_iota(jnp.int32, (pw, pw), 0)
    cc = lax.broadcasted_iota(jnp.int32, (pw, pw), 1)
    same_head = (rr < hd) == (cc < hd)
    probs = [(d, p) for d in range(2) for p in range(RWKV_HEADS // 2)]
    n = range(len(probs))
    incl = [(fwd_diff >= 0) if d == 0 else (fwd_diff <= 0) for d, _ in probs]
    strict = [(fwd_diff > 0) if d == 0 else (fwd_diff < 0) for d, _ in probs]

    def bdiag(m):
        m = m.astype(BF16)
        zero = jnp.zeros_like(m)
        return jnp.concatenate([jnp.where(first, m, zero), jnp.where(first, zero, m)], axis=0)

    def part(name):
        return [rows[d][name][:, p * pw:(p + 1) * pw] for d, p in probs]

    at_p, rt_p, v_p = part("at"), part("rt"), part("v")
    bt_p, kt_p, pend_p = part("bt"), part("kt"), part("pend")
    bh_p = [m.astype(BF16) for m in part("bh")]
    kh_p = [m.astype(BF16) for m in part("kh")]
    lhs = [jnp.concatenate([at_p[i], rt_p[i]], axis=0).astype(BF16) for i in n]
    gb = [_dot_nt(lhs[i], bdiag(bt_p[i])) for i in n]
    gk = [_dot_nt(lhs[i], bdiag(kt_p[i])) for i in n]
    aab = [jnp.where(strict[i], gb[i][:lc], 0.0) for i in n]
    arb = [jnp.where(incl[i], gb[i][lc:], 0.0) for i in n]
    aak = [jnp.where(strict[i], gk[i][:lc], 0.0) for i in n]
    ark = [jnp.where(incl[i], gk[i][lc:], 0.0) for i in n]
    vbd = [bdiag(m) for m in v_p]
    aakv = [_dot(aak[i].astype(BF16), vbd[i]) for i in n]
    s_tok = lane & (hd - 1)

    def same_block(size):
        shift = int(np.log2(size))
        return (tok >> shift) == (s_tok >> shift)

    nd = [jnp.where(same_block(RWKV_INV_BASE), m, 0.0) for m in aab]
    x = [eye + m for m in nd]
    nk = [_dot(m.astype(BF16), bdiag(m)) for m in nd]
    for _ in range(int(np.log2(RWKV_INV_BASE)) - 2):
        out = [_dot(jnp.concatenate([nk[i], x[i]], axis=0).astype(BF16), bdiag(nk[i])) for i in n]
        nk = [o[:lc] for o in out]
        x = [x[i] + out[i][lc:] for i in n]
    x = [x[i] + _dot(x[i].astype(BF16), bdiag(nk[i])) for i in n]
    size = RWKV_INV_BASE
    while size < lc:
        couple = jnp.logical_and(same_block(2 * size), jnp.logical_not(same_block(size)))
        xn = [_dot(x[i].astype(BF16), bdiag(jnp.where(couple, aab[i], 0.0))) for i in n]
        x = [x[i] + _dot(xn[i].astype(BF16), bdiag(x[i])) for i in n]
        size *= 2
    xc = [_dot(x[i].astype(BF16), jnp.concatenate([bdiag(at_p[i]), bdiag(aakv[i])], axis=1)) for i in n]
    ahat = [m[:, :pw] for m in xc]
    wmat = [m[:, pw:] for m in xc]
    rhat = [rt_p[i] + _dot(arb[i].astype(BF16), bdiag(ahat[i])) for i in n]
    y0 = [_dot(jnp.concatenate([arb[i], ark[i]], axis=1).astype(BF16),
               jnp.concatenate([bdiag(wmat[i]), vbd[i]], axis=0)) for i in n]
    mab = [jnp.where(same_head, _dot(ahat[i].T.astype(BF16), bh_p[i]), 0.0).astype(BF16) for i in n]
    gtf = [_dot(jnp.concatenate([wmat[i].T, v_p[i].T], axis=1).astype(BF16),
                jnp.concatenate([bh_p[i], kh_p[i]], axis=0)) for i in n]
    for i, (d, p) in enumerate(probs):
        s0 = s_scr[d, p]
        wkv_refs[d][:, p * pw:(p + 1) * pw] = _dot_nt(rhat[i].astype(BF16), bdiag(s0)) + y0[i]
        s_scr[d, p] = (s0 * pend_p[i] + _dot(s0.astype(BF16), mab[i])
                       + jnp.where(first, gtf[i][:lc], gtf[i][lc:]))


def _head_indicator(width, head_dim):
    idx = np.arange(width) // head_dim
    return jnp.asarray(idx[:, None] == np.arange(V7X_LANES)[None, :], BF16)


def _rwkv(rw_p, mu, w0s, w2s, a0s, a2s, k_k, k_a, r_k):
    tt, w = rw_p.shape
    lc = RWKV_CHUNK
    nch = tt // lc
    bw = BRANCH_W
    sub = V7X_SUBLANES
    zeros = jnp.zeros((DECAY_LORA, bw), F32)
    w2p = jnp.stack([jnp.concatenate([w2s[0], zeros]), jnp.concatenate([zeros, w2s[1]])]).astype(BF16)
    a2p = jnp.stack([jnp.concatenate([a2s[0], zeros]), jnp.concatenate([zeros, a2s[1]])]).astype(BF16)
    w0 = jnp.stack(w0s).reshape(2, 1, bw)
    a0 = jnp.stack(a0s).reshape(2, 1, bw)
    e_mat = _head_indicator(bw, RWKV_DIM)
    const = lambda *shape: pl.BlockSpec(shape, lambda t: tuple(0 for _ in shape))

    def chunk_specs(d):
        cidx = lambda t: _rwkv_chunk_index(d, t, nch)
        return [pl.BlockSpec((lc, w), lambda t: (cidx(t), 0)),
                pl.BlockSpec((sub, w), lambda t: (jnp.maximum(cidx(t) * (lc // sub) - 1, 0), 0)),
                pl.BlockSpec((sub, w), lambda t: (jnp.minimum((cidx(t) + 1) * (lc // sub), tt // sub - 1), 0))]

    out_spec = lambda d: pl.BlockSpec((lc, bw), lambda t: (_rwkv_chunk_index(d, t, nch), 0))
    return pl.pallas_call(
        functools.partial(_rwkv_kernel, nch=nch),
        grid=(nch,),
        in_specs=chunk_specs(0) + chunk_specs(1) + [
            const(1, w), const(2, 1, bw), const(2, 2 * DECAY_LORA, bw), const(2, 1, bw), const(2, 2 * ICLR_LORA, bw),
            const(1, bw), const(1, bw), const(1, bw), const(bw, V7X_LANES), const(V7X_LANES, bw)],
        out_specs=[out_spec(0), out_spec(1), out_spec(0), out_spec(1)],
        out_shape=[jax.ShapeDtypeStruct((tt, bw), F32)] * 4,
        scratch_shapes=[pltpu.VMEM((2, RWKV_HEADS // 2, RWKV_DIM, 2 * RWKV_DIM), F32)],
        compiler_params=_cparams(("arbitrary",), 48 * 1024 * 1024),
        name="rwkv7",
    )(rw_p, rw_p, rw_p, rw_p, rw_p, rw_p, mu.reshape(1, w), w0, w2p, a0, a2p,
      k_k.reshape(1, bw), k_a.reshape(1, bw), r_k.reshape(1, bw), e_mat, e_mat.T)


def _odd_out_kernel(na_ref, wkvf_ref, wkvb_ref, bonf_ref, bonb_ref, gd_ref, lng_ref, lnb_ref, e_ref, et_ref, w_ref,
                    res_ref, mod_ref, gpost_ref, o_ref):
    d = o_ref.shape[-1]
    e, et = e_ref[...], et_ref[...]
    inv = 1.0 / RWKV_DIM
    nrow = o_ref.shape[0] // ODD_OUT_ROW_GROUPS
    groups = [slice(g * nrow, (g + 1) * nrow) for g in range(ODD_OUT_ROW_GROUPS)]
    wkv = [wkvf_ref[g, :] + wkvb_ref[g, :] for g in groups]
    mean = [_head_sums([x], e, et)[0] * inv for x in wkv]
    xc = [x - mu for x, mu in zip(wkv, mean)]
    var = [_head_sums([x * x], e, et)[0] * inv for x in xc]
    y = [xc[i] * lax.rsqrt(var[i] + RWKV_GN_EPS) * lng_ref[...] + lnb_ref[...] + bonf_ref[g, :] + bonb_ref[g, :]
         for i, g in enumerate(groups)]
    m2 = jnp.concatenate([(gd_ref[g, :].astype(F32) * y[i]).astype(BF16) for i, g in enumerate(groups)], axis=0)
    out = _dot(na_ref[...], w_ref[:BRANCH_W, :]) + _dot(m2, w_ref[BRANCH_W:, :])
    gate = mod_ref[0][0:1, 2 * d:]
    o_ref[...] = res_ref[...] + gate * (_rms_rows(out) * gpost_ref[...])


def _odd_out(na_g, wkv_f, wkv_b, bon_f, bon_b, gd, lnx_g, lnx_b, w_out, stream, mod, g_post):
    t, bw = na_g.shape
    d = stream.shape[1]
    tm = ROW_BLOCK
    off = CTX_LEN // tm
    lat = lambda w: pl.BlockSpec((tm, w), lambda i: (off + i, 0))
    vec = lambda w: pl.BlockSpec((1, w), lambda i: (0, 0))
    e_mat = _head_indicator(bw, RWKV_DIM)
    return pl.pallas_call(
        _odd_out_kernel,
        grid=(t // tm,),
        in_specs=[pl.BlockSpec((tm, bw), lambda i: (i, 0)), lat(bw), lat(bw), lat(bw), lat(bw), lat(bw),
                  vec(bw), vec(bw),
                  pl.BlockSpec((bw, V7X_LANES), lambda i: (0, 0)),
                  pl.BlockSpec((V7X_LANES, bw), lambda i: (0, 0)),
                  pl.BlockSpec((2 * bw, d), lambda i: (0, 0)),
                  lat(d),
                  pl.BlockSpec((1, V7X_SUBLANES, 3 * d), lambda i: (1, 0, 0)),
                  vec(d)],
        out_specs=pl.BlockSpec((tm, d), lambda i: (i, 0)),
        out_shape=jax.ShapeDtypeStruct((t, d), F32),
        compiler_params=_cparams(("arbitrary",), 48 * 1024 * 1024),
        name="odd_out_proj",
    )(na_g, wkv_f, wkv_b, bon_f, bon_b, gd, lnx_g.reshape(1, bw), lnx_b.reshape(1, bw), e_mat, e_mat.T, w_out,
      stream, mod, g_post.reshape(1, d))


def _rope_tables(t):
    n_rows = t // GRID_W
    n_freq = RET_DK // 4
    inv = ROPE_THETA ** (-jnp.arange(n_freq, dtype=F32) / n_freq)
    row_ang = jnp.arange(n_rows, dtype=F32)[:, None] * inv
    col_ang = jnp.arange(GRID_W, dtype=F32)[:, None] * inv

    def table(fn):
        rows_part = jnp.broadcast_to(fn(row_ang)[:, None, :], (n_rows, GRID_W, n_freq))
        cols_part = jnp.broadcast_to(fn(col_ang)[None, :, :], (n_rows, GRID_W, n_freq))
        return jnp.repeat(jnp.concatenate([rows_part, cols_part], axis=-1).reshape(t, 2 * n_freq), 2, axis=-1)

    cos = table(jnp.cos)
    sin = table(jnp.sin) * jnp.tile(jnp.asarray([-1.0, 1.0], F32), RET_DK // 2)
    cos = jnp.concatenate([jnp.ones((CTX_LEN, RET_DK), F32), cos], axis=0)
    sin = jnp.concatenate([jnp.zeros((CTX_LEN, RET_DK), F32), sin], axis=0)
    return cos, sin


def kernel(x, c, ctx, c_ctx, w_mod, b_mod, g_pre, g_post, ev_w_in, ev_w_out, ret_decay_fwd, ret_decay_bwd, gqa_q_norm, gqa_k_norm, od_w_in, od_w_out, na_rpb, rwkv_shift_mu, rwkv_w0_fwd, rwkv_w2_fwd, rwkv_w0_bwd, rwkv_w2_bwd, rwkv_a0_fwd, rwkv_a2_fwd, rwkv_a0_bwd, rwkv_a2_bwd, rwkv_k_k, rwkv_k_a, rwkv_r_k, rwkv_lnx_g, rwkv_lnx_b):
    assert x.shape[0] == 1 and DEPTH == 2 and RET_DK == GQA_DIM
    t = x.shape[1]
    tt = t + CTX_LEN
    assert t % ROW_BLOCK == 0 and tt % ATT_KV_BLOCK == 0 and t % GRID_W == 0
    x2, ctx2 = x[0], ctx[0]
    mod = _modulation(c, c_ctx, w_mod, b_mod)
    cos_t, sin_t = _rope_tables(t)

    qa, ka, va, qb, kbt, vb, ga, gb = _even_in_proj(x2, ctx2, mod, g_pre[0], ev_w_in[0].astype(BF16), cos_t, sin_t,
                                                    gqa_q_norm[0], gqa_k_norm[0])
    ret_g = _retention(qa, ka, va, ga, ret_decay_fwd[0], ret_decay_bwd[0])
    att_lat = _gqa_attention(qb, kbt, vb, gb, q_row0=CTX_LEN, n_q=t, n_keys=tt, kv_block=ATT_KV_BLOCK, out_rows=t)
    att_ctx = _gqa_attention(qb, kbt, vb, gb, q_row0=0, n_q=CTX_LEN, n_keys=CTX_LEN, kv_block=CTX_LEN,
                             out_rows=CTX_LEN)
    stream1 = _out_proj(ret_g, att_ctx, att_lat, ev_w_out[0].astype(BF16), ctx2, x2, mod, 0, g_post[0])

    q, k, v, rw_p, gc, gd = _odd_in_proj(stream1, mod, g_pre[1], od_w_in[0].astype(BF16))
    na_g = _neighbourhood_attention(q, k, v, gc, na_rpb[0])
    rw = _rwkv(rw_p, rwkv_shift_mu[0], (rwkv_w0_fwd[0], rwkv_w0_bwd[0]), (rwkv_w2_fwd[0], rwkv_w2_bwd[0]),
               (rwkv_a0_fwd[0], rwkv_a0_bwd[0]), (rwkv_a2_fwd[0], rwkv_a2_bwd[0]),
               rwkv_k_k[0], rwkv_k_a[0], rwkv_r_k[0].reshape(-1))
    out = _odd_out(na_g, *rw, gd, rwkv_lnx_g[0], rwkv_lnx_b[0], od_w_out[0].astype(BF16), stream1, mod, g_post[1])
    return out[None]
```

```python
import functools

import jax
import jax.numpy as jnp
import numpy as np
from jax import lax
from jax.experimental import pallas as pl
from jax.experimental.pallas import tpu as pltpu

F32 = jnp.float32
BF16 = jnp.bfloat16

D_MODEL = 1024
DEPTH = 2
GRID_W = 64
CTX_LEN = 256
BRANCH_W = D_MODEL
RET_HEADS = 4
RET_DK = 128
RET_DV = BRANCH_W // RET_HEADS
RET_CHUNK = 128
GQA_HEADS = 8
GQA_KV_HEADS = 2
GQA_DIM = BRANCH_W // GQA_HEADS
GQA_GROUP = GQA_HEADS // GQA_KV_HEADS
NA_HEADS = 16
NA_DIM = BRANCH_W // NA_HEADS
NA_WIN_ROWS = 8
NA_WIN_COLS = 16
RWKV_HEADS = 16
RWKV_DIM = BRANCH_W // RWKV_HEADS
DECAY_LORA = 64
ICLR_LORA = 64
ROPE_THETA = 10000.0
NORM_EPS = 1e-6
RWKV_GN_EPS = 64e-5
SHIFT_W = 3 * BRANCH_W + 2 * DECAY_LORA + 2 * ICLR_LORA
EVEN_IN = 2 * RET_HEADS * RET_DK + BRANCH_W + GQA_HEADS * GQA_DIM + 2 * GQA_KV_HEADS * GQA_DIM + 2 * BRANCH_W
ODD_IN = 3 * BRANCH_W + SHIFT_W + 2 * BRANCH_W

V7X_LANES = 128
V7X_SUBLANES = 8
V7X_VMEM_BYTES = 64 * 1024 * 1024

ROW_BLOCK = CTX_LEN
RWKV_CHUNK = 64
RWKV_INV_BASE = 8
ATT_Q_BLOCK = 256
ATT_KV_BLOCK = 1280
LOG2_E = 1.4426950408889634
NA_BLOCK_ROWS = 4
NA_UNION_ROWS = 12
NA_BLOCKS_PER_STEP = 2
ODD_OUT_ROW_GROUPS = 2
MASK_VALUE = -1e30


def _vmem_limit(nbytes):
    return int(min(V7X_VMEM_BYTES - 4 * 1024 * 1024, max(32 * 1024 * 1024, nbytes)))


def _cparams(sem, vmem_bytes):
    return pltpu.CompilerParams(dimension_semantics=sem, vmem_limit_bytes=_vmem_limit(vmem_bytes))


def _silu(x):
    return x / (1.0 + jnp.exp(-x))


def _sigmoid(x):
    return 1.0 / (1.0 + jnp.exp(-x))


def _dot(a, b):
    return jnp.dot(a, b, preferred_element_type=F32)


def _dot_nt(a, b):
    return lax.dot_general(a, b, (((1,), (1,)), ((), ())), preferred_element_type=F32)


def _split2(x):
    hi = x.astype(BF16)
    lo = (x - hi.astype(F32)).astype(BF16)
    return hi, lo


def _dot_split_rhs(a_bf16, x):
    hi, lo = _split2(x)
    return _dot(a_bf16, hi) + _dot(a_bf16, lo)


def _head_sums(xs, e, et):
    n = xs[0].shape[0]

    def stacked_dot(vals, w):
        parts = [t for x in vals for t in _split2(x)]
        y = _dot(jnp.concatenate(parts, axis=0), w)
        return [y[2 * i * n:(2 * i + 1) * n] + y[(2 * i + 1) * n:(2 * i + 2) * n] for i in range(len(vals))]

    return stacked_dot(stacked_dot(xs, e), et)


def _rms_rows(x):
    return x * lax.rsqrt(jnp.mean(x * x, axis=-1, keepdims=True) + NORM_EPS)


def _rope(t, cos, sin_signed, even):
    nxt = pltpu.roll(t, t.shape[1] - 1, 1)
    prv = pltpu.roll(t, 1, 1)
    return t * cos + jnp.where(even, nxt, prv) * sin_signed


def _mod_kernel(cc_ref, w_ref, b_ref, o_ref):
    s = _silu(cc_ref[...])
    o_ref[0] = jnp.dot(s, w_ref[0], preferred_element_type=F32, precision=lax.Precision.HIGHEST) + b_ref[0]


def _modulation(c, c_ctx, w_mod, b_mod):
    d = c.shape[-1]
    cc = jnp.concatenate([c[:1], c_ctx[None, :], jnp.zeros((V7X_SUBLANES - 2, d), F32)], axis=0)
    return pl.pallas_call(
        _mod_kernel,
        grid=(DEPTH, 3),
        in_specs=[pl.BlockSpec((V7X_SUBLANES, d), lambda l, j: (0, 0)),
                  pl.BlockSpec((1, d, d), lambda l, j: (l, 0, j)),
                  pl.BlockSpec((1, 1, d), lambda l, j: (l, 0, j))],
        out_specs=pl.BlockSpec((1, V7X_SUBLANES, d), lambda l, j: (l, 0, j)),
        out_shape=jax.ShapeDtypeStruct((DEPTH, V7X_SUBLANES, 3 * d), F32),
        compiler_params=_cparams(("arbitrary", "arbitrary"), 40 * 1024 * 1024),
        name="modulation",
    )(cc, w_mod, b_mod.reshape(DEPTH, 1, 3 * d))


def _adaln(xb, mod, is_ctx, g_pre):
    d = xb.shape[-1]
    m = jnp.where(is_ctx, mod[1:2, :], mod[0:1, :])
    shift, scale = m[:, :d], m[:, d:2 * d]
    return (_rms_rows(xb) * g_pre) * (1.0 + scale) + shift


def _even_in_kernel(x_ref, ctx_ref, mod_ref, gpre_ref, w_ref, cos_ref, sin_ref, qn_ref, kn_ref,
                    qa_ref, ka_ref, va_ref, qb_ref, kbt_ref, vb_ref, ga_ref, gb_ref):
    is_ctx = pl.program_id(0) == 0
    xb = jnp.where(is_ctx, ctx_ref[...], x_ref[...])
    hb = _adaln(xb, mod_ref[0], is_ctx, gpre_ref[...]).astype(BF16)
    cos, sin_s = cos_ref[...], sin_ref[...]
    even = (lax.broadcasted_iota(jnp.int32, cos.shape, 1) & 1) == 0
    o = 0

    def seg(width):
        nonlocal o
        y = _dot(hb, w_ref[:, o:o + width])
        o += width
        return y

    y = seg(RET_HEADS * RET_DK)
    for h in range(RET_HEADS):
        sl = slice(h * RET_DK, (h + 1) * RET_DK)
        qa_ref[:, sl] = _rope(y[:, sl], cos, sin_s, even).astype(BF16)
    y = seg(RET_HEADS * RET_DK)
    for h in range(RET_HEADS):
        sl = slice(h * RET_DK, (h + 1) * RET_DK)
        ka_ref[:, sl] = _rope(y[:, sl] * RET_DK ** -0.5, cos, sin_s, even).astype(BF16)
    va_ref[...] = seg(BRANCH_W).astype(BF16)
    y = seg(GQA_HEADS * GQA_DIM)
    for h in range(GQA_HEADS):
        sl = slice(h * GQA_DIM, (h + 1) * GQA_DIM)
        t = _rms_rows(y[:, sl]) * qn_ref[...]
        qb_ref[:, sl] = (_rope(t, cos, sin_s, even) * (GQA_DIM ** -0.5 * LOG2_E)).astype(BF16)
    y = seg(GQA_KV_HEADS * GQA_DIM)
    for h in range(GQA_KV_HEADS):
        sl = slice(h * GQA_DIM, (h + 1) * GQA_DIM)
        t = _rope(_rms_rows(y[:, sl]) * kn_ref[...], cos, sin_s, even)
        kbt_ref[sl, :] = t.T.astype(BF16)
    y = seg(GQA_KV_HEADS * GQA_DIM)
    for h in range(GQA_KV_HEADS):
        vb_ref[:, 2 * h * GQA_DIM:(2 * h + 1) * GQA_DIM] = y[:, h * GQA_DIM:(h + 1) * GQA_DIM].astype(BF16)
        vb_ref[:, (2 * h + 1) * GQA_DIM:(2 * h + 2) * GQA_DIM] = jnp.ones((y.shape[0], GQA_DIM), BF16)
    ga_ref[...] = _silu(seg(BRANCH_W)).astype(BF16)
    gb_ref[...] = _silu(seg(BRANCH_W)).astype(BF16)


def _even_in_proj(x, ctx, mod, g_pre, w_in, cos_t, sin_t, q_norm, k_norm):
    t, d = x.shape
    tt = t + CTX_LEN
    nblk = tt // ROW_BLOCK
    tm = ROW_BLOCK
    kvw = GQA_KV_HEADS * GQA_DIM
    row = lambda w: pl.BlockSpec((tm, w), lambda i: (i, 0))
    const = lambda shape: pl.BlockSpec(shape, lambda i: tuple(0 for _ in shape))
    outs = [((tt, RET_HEADS * RET_DK), row(RET_HEADS * RET_DK)),
            ((tt, RET_HEADS * RET_DK), row(RET_HEADS * RET_DK)),
            ((tt, BRANCH_W), row(BRANCH_W)),
            ((tt, GQA_HEADS * GQA_DIM), row(GQA_HEADS * GQA_DIM)),
            ((kvw, tt), pl.BlockSpec((kvw, tm), lambda i: (0, i))),
            ((tt, 2 * kvw), row(2 * kvw)),
            ((tt, BRANCH_W), row(BRANCH_W)),
            ((tt, BRANCH_W), row(BRANCH_W))]
    return pl.pallas_call(
        _even_in_kernel,
        grid=(nblk,),
        in_specs=[pl.BlockSpec((tm, d), lambda i: (jnp.maximum(i - 1, 0), 0)),
                  const((CTX_LEN, d)),
                  pl.BlockSpec((1, V7X_SUBLANES, 3 * d), lambda i: (0, 0, 0)),
                  const((1, d)),
                  const((d, EVEN_IN)),
                  row(RET_DK), row(RET_DK),
                  const((1, GQA_DIM)), const((1, GQA_DIM))],
        out_specs=[s for _, s in outs],
        out_shape=[jax.ShapeDtypeStruct(shp, BF16) for shp, _ in outs],
        compiler_params=_cparams(("arbitrary",), 2 * d * EVEN_IN * 2 + 16 * 1024 * 1024),
        name="even_in_proj",
    )(x, ctx, mod, g_pre.reshape(1, d), w_in, cos_t, sin_t, q_norm.reshape(1, -1), k_norm.reshape(1, -1))


def _log_sigmoid(x):
    return jnp.minimum(x, 0.0) - jnp.log(1.0 + jnp.exp(-jnp.abs(x)))


def _ret_bwd_chunk(t, nchunks):
    nctx = CTX_LEN // RET_CHUNK
    return jnp.where(t < nctx, nctx - 1 - t, nchunks - 1 - (t - nctx))


def _ret_state_kernel(dec_ref, k_ref, v_ref, sb_ref, s_scr):
    c = RET_CHUNK

    @pl.when(pl.program_id(0) == 0)
    def _():
        s_scr[...] = jnp.zeros_like(s_scr)

    lg = _log_sigmoid(dec_ref[...])
    pos = lax.broadcasted_iota(jnp.int32, (c, RET_DK), 0).astype(F32)
    for h in range(RET_HEADS):
        lgb = lg[RET_HEADS + h:RET_HEADS + h + 1, :]
        s_old = s_scr[h]
        sb_ref[0, h] = s_old.astype(BF16)
        kz = k_ref[:, h * RET_DK:(h + 1) * RET_DK].astype(F32) * jnp.exp(pos * lgb)
        u = _dot(kz.T.astype(BF16), v_ref[:, h * RET_DV:(h + 1) * RET_DV])
        s_scr[h] = jnp.exp(c * lgb[:, :1]) * s_old + u


def _ret_out_kernel(dec_ref, q_ref, k_ref, v_ref, g_ref, sb_ref, o_ref, s_scr):
    c = RET_CHUNK

    @pl.when(pl.program_id(0) == 0)
    def _():
        s_scr[...] = jnp.zeros_like(s_scr)

    lg = _log_sigmoid(dec_ref[...])
    ii = lax.broadcasted_iota(jnp.int32, (c, c), 0)
    jj = lax.broadcasted_iota(jnp.int32, (c, c), 1)
    dlt = (ii - jj).astype(F32)
    pos = lax.broadcasted_iota(jnp.int32, (c, RET_DK), 0).astype(F32)
    hs = range(RET_HEADS)
    lgf = [lg[h:h + 1, :] for h in hs]
    lgb = [lg[RET_HEADS + h:RET_HEADS + h + 1, :] for h in hs]
    dec = [jnp.where(dlt > 0, jnp.exp(jnp.maximum(dlt, 0.0) * lgf[h]),
                     jnp.where(dlt < 0, jnp.exp(jnp.maximum(-dlt, 0.0) * lgb[h]), 2.0)) for h in hs]
    q = [q_ref[:, h * RET_DK:(h + 1) * RET_DK] for h in hs]
    k = [k_ref[:, h * RET_DK:(h + 1) * RET_DK] for h in hs]
    v = [v_ref[:, h * RET_DV:(h + 1) * RET_DV] for h in hs]
    s_old = [s_scr[h] for h in hs]
    p = [(_dot_nt(q[h], k[h]) * dec[h]).astype(BF16) for h in hs]
    qf = [q[h].astype(F32) for h in hs]
    ret = [_dot(p[h], v[h])
           + _dot((qf[h] * jnp.exp((pos + 1.0) * lgf[h])).astype(BF16), s_old[h].astype(BF16))
           + _dot((qf[h] * jnp.exp((c - pos) * lgb[h])).astype(BF16), sb_ref[0, h]) for h in hs]
    kzt = [(k[h].astype(F32) * jnp.exp((c - 1.0 - pos) * lgf[h])).T.astype(BF16) for h in hs]
    upd = [_dot(kzt[h], v[h]) for h in hs]
    xc = [ret[h] - jnp.mean(ret[h], axis=-1, keepdims=True) for h in hs]
    y = [xc[h] * lax.rsqrt(jnp.mean(xc[h] * xc[h], axis=-1, keepdims=True) + NORM_EPS) for h in hs]
    for h in hs:
        sl = slice(h * RET_DV, (h + 1) * RET_DV)
        o_ref[:, sl] = (g_ref[:, sl].astype(F32) * y[h]).astype(BF16)
        s_scr[h] = jnp.exp(c * lgf[h][:, :1]) * s_old[h] + upd[h]


def _retention(qa, ka, va, ga, dec_f, dec_b):
    tt = qa.shape[0]
    c = RET_CHUNK
    n = tt // c
    dec = jnp.broadcast_to(jnp.concatenate([dec_f, dec_b]).astype(F32)[:, None], (2 * RET_HEADS, V7X_LANES))
    kw, vw = RET_HEADS * RET_DK, BRANCH_W
    dec_spec = pl.BlockSpec((2 * RET_HEADS, V7X_LANES), lambda t: (0, 0))
    state_shape = (RET_HEADS, RET_DK, RET_DV)
    sb = pl.pallas_call(
        _ret_state_kernel,
        grid=(n,),
        in_specs=[dec_spec,
                  pl.BlockSpec((c, kw), lambda t: (_ret_bwd_chunk(t, n), 0)),
                  pl.BlockSpec((c, vw), lambda t: (_ret_bwd_chunk(t, n), 0))],
        out_specs=pl.BlockSpec((1,) + state_shape, lambda t: (_ret_bwd_chunk(t, n), 0, 0, 0)),
        out_shape=jax.ShapeDtypeStruct((n,) + state_shape, BF16),
        scratch_shapes=[pltpu.VMEM(state_shape, F32)],
        compiler_params=_cparams(("arbitrary",), 32 * 1024 * 1024),
        name="retention_state",
    )(dec, ka, va)
    return pl.pallas_call(
        _ret_out_kernel,
        grid=(n,),
        in_specs=[dec_spec,
                  pl.BlockSpec((c, kw), lambda t: (t, 0)),
                  pl.BlockSpec((c, kw), lambda t: (t, 0)),
                  pl.BlockSpec((c, vw), lambda t: (t, 0)),
                  pl.BlockSpec((c, vw), lambda t: (t, 0)),
                  pl.BlockSpec((1,) + state_shape, lambda t: (t, 0, 0, 0))],
        out_specs=pl.BlockSpec((c, vw), lambda t: (t, 0)),
        out_shape=jax.ShapeDtypeStruct((tt, vw), BF16),
        scratch_shapes=[pltpu.VMEM(state_shape, F32)],
        compiler_params=_cparams(("arbitrary",), 32 * 1024 * 1024),
        name="retention_out",
    )(dec, qa, ka, va, ga, sb)


def _gqa_kernel(q_ref, kt_ref, v_ref, g_ref, o_ref, s0_scr, s1_scr, *, kv_block, n_kv):
    tq = q_ref.shape[0]
    heads = [slice(h * GQA_DIM, (h + 1) * GQA_DIM) for h in range(GQA_GROUP)]
    q = jnp.concatenate([q_ref[:, sl] for sl in heads], axis=0)
    rows = GQA_GROUP * tq

    def scores(j, s_ref):
        start = pl.multiple_of(j * kv_block, V7X_LANES)
        s_ref[...] = _dot(q, kt_ref[:, pl.ds(start, kv_block)])

    def softmax_pv(j, s_ref, carry):
        m, acc = carry
        start = pl.multiple_of(j * kv_block, V7X_LANES)
        s = s_ref[...]
        m_new = jnp.maximum(m, jnp.max(s, axis=-1, keepdims=True))
        p = jnp.exp2(s - m_new).astype(BF16)
        acc = jnp.exp2(m - m_new) * acc + _dot(p, v_ref[pl.ds(start, kv_block), :])
        return m_new, acc

    def body(i, carry):
        j = 2 * i
        scores(j + 1, s1_scr)
        carry = softmax_pv(j, s0_scr, carry)
        scores(j + 2, s0_scr)
        return softmax_pv(j + 1, s1_scr, carry)

    carry = (jnp.full((rows, 1), MASK_VALUE, F32), jnp.zeros((rows, 2 * GQA_DIM), F32))
    scores(0, s0_scr)
    carry = lax.fori_loop(0, (n_kv - 1) // 2, body, carry)
    if n_kv % 2 == 0:
        scores(n_kv - 1, s1_scr)
        carry = softmax_pv(n_kv - 2, s0_scr, carry)
        _, acc = softmax_pv(n_kv - 1, s1_scr, carry)
    else:
        _, acc = softmax_pv(n_kv - 1, s0_scr, carry)
    out = acc[:, :GQA_DIM] / acc[:, GQA_DIM:]
    for h, sl in enumerate(heads):
        o_ref[:, sl] = (g_ref[:, sl].astype(F32) * out[h * tq:(h + 1) * tq]).astype(BF16)


def _gqa_attention(qb, kbt, vb, gb, *, q_row0, n_q, n_keys, kv_block, out_rows):
    gw = GQA_GROUP * GQA_DIM
    tq = ATT_Q_BLOCK
    qoff = q_row0 // tq
    return pl.pallas_call(
        functools.partial(_gqa_kernel, kv_block=kv_block, n_kv=n_keys // kv_block),
        grid=(GQA_KV_HEADS, n_q // tq),
        in_specs=[pl.BlockSpec((tq, gw), lambda g, i: (qoff + i, g)),
                  pl.BlockSpec((GQA_DIM, n_keys), lambda g, i: (g, 0)),
                  pl.BlockSpec((n_keys, 2 * GQA_DIM), lambda g, i: (0, g)),
                  pl.BlockSpec((tq, gw), lambda g, i: (qoff + i, g))],
        out_specs=pl.BlockSpec((tq, gw), lambda g, i: (i, g)),
        out_shape=jax.ShapeDtypeStruct((out_rows, GQA_HEADS * GQA_DIM), BF16),
        scratch_shapes=[pltpu.VMEM((GQA_GROUP * tq, kv_block), F32)] * 2,
        compiler_params=_cparams(("arbitrary", "arbitrary"), 56 * 1024 * 1024),
        name="gqa_attention",
    )(qb, kbt, vb, gb)


def _out_proj_kernel(m1_ref, m2c_ref, m2l_ref, w_ref, resc_ref, resl_ref, mod_ref, gpost_ref, o_ref):
    d = o_ref.shape[-1]
    is_ctx = pl.program_id(0) == 0
    m2 = jnp.where(is_ctx, m2c_ref[...], m2l_ref[...])
    y = _dot(m1_ref[...], w_ref[:BRANCH_W, :]) + _dot(m2, w_ref[BRANCH_W:, :])
    mod = mod_ref[0]
    gate = jnp.where(is_ctx, mod[1:2, 2 * d:], mod[0:1, 2 * d:])
    res = jnp.where(is_ctx, resc_ref[...], resl_ref[...])
    o_ref[...] = res + gate * (_rms_rows(y) * gpost_ref[...])


def _out_proj(m1, m2_ctx, m2_lat, w_out, res_ctx, res_lat, mod, layer, g_post):
    tt = m1.shape[0]
    d = res_lat.shape[1]
    tm = ROW_BLOCK
    row = lambda w: pl.BlockSpec((tm, w), lambda i: (i, 0))
    lat = lambda w: pl.BlockSpec((tm, w), lambda i: (jnp.maximum(i - 1, 0), 0))
    ctx = lambda w: pl.BlockSpec((CTX_LEN, w), lambda i: (0, 0))
    return pl.pallas_call(
        _out_proj_kernel,
        grid=(tt // tm,),
        in_specs=[row(BRANCH_W), ctx(BRANCH_W), lat(BRANCH_W),
                  pl.BlockSpec((2 * BRANCH_W, d), lambda i: (0, 0)),
                  ctx(d), lat(d),
                  pl.BlockSpec((1, V7X_SUBLANES, 3 * d), lambda i: (layer, 0, 0)),
                  pl.BlockSpec((1, d), lambda i: (0, 0))],
        out_specs=row(d),
        out_shape=jax.ShapeDtypeStruct((tt, d), F32),
        compiler_params=_cparams(("arbitrary",), 40 * 1024 * 1024),
        name="out_proj",
    )(m1, m2_ctx, m2_lat, w_out, res_ctx, res_lat, mod, g_post.reshape(1, d))


def _odd_in_kernel(s_ref, mod_ref, gpre_ref, w_ref, q_ref, k_ref, v_ref, rw_ref, gc_ref, gd_ref):
    is_ctx = pl.program_id(0) == 0
    hb = _adaln(s_ref[...], mod_ref[0], is_ctx, gpre_ref[...]).astype(BF16)
    o = 0

    def seg(width):
        nonlocal o
        y = _dot(hb, w_ref[:, o:o + width])
        o += width
        return y

    q_ref[...] = (seg(BRANCH_W) * NA_DIM ** -0.5).astype(BF16)
    k_ref[...] = seg(BRANCH_W).astype(BF16)
    v_ref[...] = seg(BRANCH_W).astype(BF16)
    rw_ref[...] = seg(SHIFT_W)
    gc_ref[...] = _silu(seg(BRANCH_W)).astype(BF16)
    gd_ref[...] = _silu(seg(BRANCH_W)).astype(BF16)


def _odd_in_proj(stream, mod, g_pre, w_in):
    tt, d = stream.shape
    tm = ROW_BLOCK
    row = lambda w: pl.BlockSpec((tm, w), lambda i: (i, 0))
    widths = [(BRANCH_W, BF16), (BRANCH_W, BF16), (BRANCH_W, BF16), (SHIFT_W, F32), (BRANCH_W, BF16), (BRANCH_W, BF16)]
    return pl.pallas_call(
        _odd_in_kernel,
        grid=(tt // tm,),
        in_specs=[row(d),
                  pl.BlockSpec((1, V7X_SUBLANES, 3 * d), lambda i: (1, 0, 0)),
                  pl.BlockSpec((1, d), lambda i: (0, 0)),
                  pl.BlockSpec((d, ODD_IN), lambda i: (0, 0), pipeline_mode=pl.Buffered(1))],
        out_specs=[row(w) for w, _ in widths],
        out_shape=[jax.ShapeDtypeStruct((tt, w), dt) for w, dt in widths],
        compiler_params=_cparams(("arbitrary",), d * ODD_IN * 2 + 24 * 1024 * 1024),
        name="odd_in_proj",
    )(stream, mod, g_pre.reshape(1, d), w_in)


def _na_kernel(*refs, rows):
    nb = NA_BLOCKS_PER_STEP
    q_refs, g_refs = refs[:nb], refs[nb:2 * nb]
    k_ref, v_ref, bias_ref, o_ref = refs[2 * nb:]
    span = NA_UNION_ROWS * GRID_W
    bq = NA_BLOCK_ROWS * GRID_W
    first = lax.broadcasted_iota(jnp.int32, (bq, 2 * NA_DIM), 1) < NA_DIM
    kc, vc = k_ref[:CTX_LEN, :], v_ref[:CTX_LEN, :]
    blocks = range(nb)
    lhs, ku, vu, cls = [], [], [], []
    for b in blocks:
        qr0 = (pl.program_id(1) * nb + b) * NA_BLOCK_ROWS
        u0 = jnp.clip(qr0 - NA_WIN_ROWS // 2, 0, rows - NA_UNION_ROWS)
        cls.append(jnp.where(qr0 == 0, 0, jnp.where(qr0 == rows - NA_BLOCK_ROWS, 2, 1)))
        start = pl.multiple_of(CTX_LEN + u0 * GRID_W, GRID_W)
        ku.append(k_ref[pl.ds(start, span), :])
        vu.append(v_ref[pl.ds(start, span), :])
        q = q_refs[b][...]
        zero = jnp.zeros_like(q)
        lhs.append(jnp.concatenate([jnp.where(first, q, zero), jnp.where(first, zero, q)], axis=0))
    bias = [jnp.concatenate([bias_ref[0, cls[b]], bias_ref[1, cls[b]]], axis=0) for b in blocks]
    sw = [_dot_nt(lhs[b], ku[b]) + bias[b] for b in blocks]
    sc = [_dot_nt(lhs[b], kc) for b in blocks]
    m = [jnp.maximum(jnp.max(sw[b], axis=-1, keepdims=True), jnp.max(sc[b], axis=-1, keepdims=True)) for b in blocks]
    pw = [jnp.exp(sw[b] - m[b]) for b in blocks]
    pc = [jnp.exp(sc[b] - m[b]) for b in blocks]
    l = [jnp.sum(pw[b], axis=-1, keepdims=True) + jnp.sum(pc[b], axis=-1, keepdims=True) for b in blocks]
    o = [(_dot(pw[b].astype(BF16), vu[b]) + _dot(pc[b].astype(BF16), vc)) / l[b] for b in blocks]
    for b in blocks:
        out = jnp.where(first, o[b][:bq], o[b][bq:])
        o_ref[b * bq:(b + 1) * bq, :] = (g_refs[b][...].astype(F32) * out).astype(BF16)


def _na_bias_table(rpb):
    half = NA_WIN_ROWS // 2
    cols = np.arange(GRID_W)
    c0 = np.clip(cols - NA_WIN_COLS // 2, 0, GRID_W - NA_WIN_COLS)
    valid_c = (cols[None, :] >= c0[:, None]) & (cols[None, :] < c0[:, None] + NA_WIN_COLS)
    dc = np.clip(cols[None, :] - cols[:, None] + NA_WIN_COLS - 1, 0, 2 * NA_WIN_COLS - 2)
    a = np.arange(NA_BLOCK_ROWS)[:, None]
    i = np.arange(NA_UNION_ROWS)[None, :]
    dr = np.stack([i - a, i - a - half, i - a - (NA_UNION_ROWS - NA_BLOCK_ROWS)])
    w0 = np.stack([0 * a, a, (NA_UNION_ROWS - NA_WIN_ROWS) + 0 * a])
    valid_r = (i[None] >= w0) & (i[None] < w0 + NA_WIN_ROWS)
    onehot = (np.arange(2 * NA_WIN_COLS - 1)[:, None, None] == dc[None]) & valid_c[None]
    tiles = jnp.einsum('hrd,dqk->hrqk', rpb.astype(F32), jnp.asarray(onehot, F32), precision=lax.Precision.HIGHEST)
    tiles = jnp.where(jnp.asarray(valid_c)[None, None], tiles, MASK_VALUE)
    masked_tile = jnp.full((NA_HEADS, GRID_W, GRID_W), MASK_VALUE, F32)

    def tile(c, qa, ki):
        return tiles[:, dr[c, qa, ki] + NA_WIN_ROWS - 1] if valid_r[c, qa, ki] else masked_tile

    return jnp.stack([jnp.concatenate([jnp.concatenate([tile(c, qa, ki) for ki in range(NA_UNION_ROWS)], axis=-1)
                                       for qa in range(NA_BLOCK_ROWS)], axis=-2) for c in range(3)], axis=1)


def _neighbourhood_attention(q, k, v, gc, rpb):
    tt = q.shape[0]
    t = tt - CTX_LEN
    rows = t // GRID_W
    nb = NA_BLOCKS_PER_STEP
    assert NA_BLOCK_ROWS == NA_WIN_ROWS // 2 and rows >= NA_UNION_ROWS and rows % (nb * NA_BLOCK_ROWS) == 0
    bias = _na_bias_table(rpb)
    pw = 2 * NA_DIM
    bq = NA_BLOCK_ROWS * GRID_W
    qoff = CTX_LEN // bq
    assert CTX_LEN % bq == 0
    blk = lambda b: pl.BlockSpec((bq, pw), lambda p, i: (qoff + i * nb + b, p))
    resident = pl.BlockSpec((tt, pw), lambda p, i: (0, p))
    return pl.pallas_call(
        functools.partial(_na_kernel, rows=rows),
        grid=(NA_HEADS // 2, rows // (nb * NA_BLOCK_ROWS)),
        in_specs=[blk(b) for b in range(nb)] + [blk(b) for b in range(nb)] + [
            resident, resident,
            pl.BlockSpec((2,) + bias.shape[1:], lambda p, i: (p, 0, 0, 0))],
        out_specs=pl.BlockSpec((nb * bq, pw), lambda p, i: (i, p)),
        out_shape=jax.ShapeDtypeStruct((t, BRANCH_W), BF16),
        compiler_params=_cparams(("arbitrary", "arbitrary"), 48 * 1024 * 1024),
        name="neighbourhood_attention",
    )(*([q] * nb + [gc] * nb + [k, v, bias]))


def _rwkv_chunk_index(d, t, nch):
    nctx = CTX_LEN // RWKV_CHUNK
    return t if d == 0 else jnp.where(t < nctx, nctx - 1 - t, nch - 1 - (t - nctx))


def _rwkv_prepare(d, c, nch, main_ref, prev_ref, next_ref, mu_ref, w0_ref, w2_ref, a0_ref, a2_ref, kk_ref, ka_ref,
                  rk_ref, e_ref, et_ref, bon_ref):
    lc = RWKV_CHUNK
    nctx = CTX_LEN // lc
    sgn = 1 - 2 * d
    p = main_ref[...]
    row = lax.broadcasted_iota(jnp.int32, p.shape, 0)
    first_zero = jnp.logical_or(c == 0, c == nctx)
    last_zero = jnp.logical_or(c == nctx - 1, c == nch - 1)
    pr = jnp.where(first_zero, 0.0, prev_ref[V7X_SUBLANES - 1:V7X_SUBLANES, :])
    nx = jnp.where(last_zero, 0.0, next_ref[0:1, :])
    prev = jnp.where(row == 0, pr, pltpu.roll(p, 1, 0))
    nxt = jnp.where(row == lc - 1, nx, pltpu.roll(p, lc - 1, 0))
    z = p + (0.5 * (prev + nxt) - p) * mu_ref[...]

    bw = BRANCH_W
    r, k, v = z[:, :bw], z[:, bw:2 * bw], z[:, 2 * bw:3 * bw]
    zw = z[:, 3 * bw:3 * bw + 2 * DECAY_LORA]
    za = z[:, 3 * bw + 2 * DECAY_LORA:]
    lw = w0_ref[d] + _dot(jnp.tanh(zw).astype(BF16), w2_ref[d])
    ld = -float(np.exp(-0.5)) * _sigmoid(lw)
    asig = _sigmoid(a0_ref[d] + _dot(za.astype(BF16), a2_ref[d]))
    kk = k * kk_ref[...]
    kd = k * (1.0 + (asig - 1.0) * ka_ref[...])
    kk_sq, rkd = _head_sums([kk * kk, r * kd * rk_ref[...]], e_ref[...], et_ref[...])
    kkn = kk / jnp.maximum(jnp.sqrt(kk_sq), 1e-12)
    a_vec = -kkn
    b_vec = kkn * asig
    bon_ref[...] = rkd * v

    ti = lax.broadcasted_iota(jnp.int32, (lc, lc), 0)
    si = lax.broadcasted_iota(jnp.int32, (lc, lc), 1)
    cum = _dot_split_rhs(jnp.where(sgn * (ti - si) >= 0, 1.0, 0.0).astype(BF16), ld)
    tot = jnp.sum(ld, axis=0, keepdims=True)
    rem = jnp.exp(tot - cum)
    pinv = jnp.exp(-cum)
    rt = r * jnp.exp(cum)
    kt = kd * pinv
    bt = b_vec * pinv
    at = a_vec * jnp.exp(cum - ld)
    bh = b_vec * rem
    kh = kd * rem
    pend = jnp.exp(tot)
    return dict(at=at, rt=rt, bt=bt, kt=kt, bh=bh, kh=kh, v=v, pend=pend)


def _rwkv_kernel(main_f, prev_f, next_f, main_b, prev_b, next_b, mu_ref, w0_ref, w2_ref, a0_ref, a2_ref, kk_ref,
                 ka_ref, rk_ref, e_ref, et_ref, wkv_f, wkv_b, bon_f, bon_b, s_scr, *, nch):
    lc = RWKV_CHUNK
    hd = RWKV_DIM
    t = pl.program_id(0)

    @pl.when(t == 0)
    def _():
        s_scr[...] = jnp.zeros_like(s_scr)

    shared = (mu_ref, w0_ref, w2_ref, a0_ref, a2_ref, kk_ref, ka_ref, rk_ref, e_ref, et_ref)
    rows = [_rwkv_prepare(0, _rwkv_chunk_index(0, t, nch), nch, main_f, prev_f, next_f, *shared, bon_f),
            _rwkv_prepare(1, _rwkv_chunk_index(1, t, nch), nch, main_b, prev_b, next_b, *shared, bon_b)]
    wkv_refs = (wkv_f, wkv_b)

    pw = 2 * hd
    lane = lax.broadcasted_iota(jnp.int32, (lc, pw), 1)
    tok = lax.broadcasted_iota(jnp.int32, (lc, pw), 0)
    first = lane < hd
    fwd_diff = tok - (lane & (hd - 1))
    eye = (fwd_diff == 0).astype(F32)
    rr = lax.broadcasted_iota(jnp.int32, (pw, pw), 0)
    cc = lax.broadcasted_iota(jnp.int32, (pw, pw), 1)
    same_head = (rr < hd) == (cc < hd)
    probs = [(d, p) for d in range(2) for p in range(RWKV_HEADS // 2)]
    n = range(len(probs))
    incl = [(fwd_diff >= 0) if d == 0 else (fwd_diff <= 0) for d, _ in probs]
    strict = [(fwd_diff > 0) if d == 0 else (fwd_diff < 0) for d, _ in probs]

    def bdiag(m):
        m = m.astype(BF16)
        zero = jnp.zeros_like(m)
        return jnp.concatenate([jnp.where(first, m, zero), jnp.where(first, zero, m)], axis=0)

    def part(name):
        return [rows[d][name][:, p * pw:(p + 1) * pw] for d, p in probs]

    at_p, rt_p, v_p = part("at"), part("rt"), part("v")
    bt_p, kt_p, pend_p = part("bt"), part("kt"), part("pend")
    bh_p = [m.astype(BF16) for m in part("bh")]
    kh_p = [m.astype(BF16) for m in part("kh")]
    lhs = [jnp.concatenate([at_p[i], rt_p[i]], axis=0).astype(BF16) for i in n]
    gram = [_dot_nt(lhs[i], jnp.concatenate([bdiag(bt_p[i]), bdiag(kt_p[i])], axis=0)) for i in n]
    gb = [g[:, :pw] for g in gram]
    gk = [g[:, pw:] for g in gram]
    aab = [jnp.where(strict[i], gb[i][:lc], 0.0) for i in n]
    arb = [jnp.where(incl[i], gb[i][lc:], 0.0) for i in n]
    aak = [jnp.where(strict[i], gk[i][:lc], 0.0) for i in n]
    ark = [jnp.where(incl[i], gk[i][lc:], 0.0) for i in n]
    vbd = [bdiag(m) for m in v_p]
    aakv = [_dot(aak[i].astype(BF16), vbd[i]) for i in n]
    s_tok = lane & (hd - 1)

    def same_block(size):
        shift = int(np.log2(size))
        return (tok >> shift) == (s_tok >> shift)

    nd = [jnp.where(same_block(RWKV_INV_BASE), m, 0.0) for m in aab]
    x = [eye + m for m in nd]
    nk = [_dot(m.astype(BF16), bdiag(m)) for m in nd]
    for _ in range(int(np.log2(RWKV_INV_BASE)) - 2):
        out = [_dot(jnp.concatenate([nk[i], x[i]], axis=0).astype(BF16), bdiag(nk[i])) for i in n]
        nk = [o[:lc] for o in out]
        x = [x[i] + out[i][lc:] for i in n]
    x = [x[i] + _dot(x[i].astype(BF16), bdiag(nk[i])) for i in n]
    size = RWKV_INV_BASE
    while size < lc:
        couple = jnp.logical_and(same_block(2 * size), jnp.logical_not(same_block(size)))
        xn = [_dot(x[i].astype(BF16), bdiag(jnp.where(couple, aab[i], 0.0))) for i in n]
        x = [x[i] + _dot(xn[i].astype(BF16), bdiag(x[i])) for i in n]
        size *= 2
    xc = [_dot(x[i].astype(BF16), jnp.concatenate([bdiag(at_p[i]), bdiag(aakv[i])], axis=1)) for i in n]
    ahat = [m[:, :pw] for m in xc]
    wmat = [m[:, pw:] for m in xc]
    rhat = [rt_p[i] + _dot(arb[i].astype(BF16), bdiag(ahat[i])) for i in n]
    y0 = [_dot(jnp.concatenate([arb[i], ark[i]], axis=1).astype(BF16),
               jnp.concatenate([bdiag(wmat[i]), vbd[i]], axis=0)) for i in n]
    mab = [jnp.where(same_head, _dot(ahat[i].T.astype(BF16), bh_p[i]), 0.0).astype(BF16) for i in n]
    gtf = [_dot(jnp.concatenate([wmat[i].T, v_p[i].T], axis=1).astype(BF16),
                jnp.concatenate([bh_p[i], kh_p[i]], axis=0)) for i in n]
    for i, (d, p) in enumerate(probs):
        s0 = s_scr[d, p]
        wkv_refs[d][:, p * pw:(p + 1) * pw] = _dot_nt(rhat[i].astype(BF16), bdiag(s0)) + y0[i]
        s_scr[d, p] = (s0 * pend_p[i] + _dot(s0.astype(BF16), mab[i])
                       + jnp.where(first, gtf[i][:lc], gtf[i][lc:]))


def _head_indicator(width, head_dim):
    idx = np.arange(width) // head_dim
    return jnp.asarray(idx[:, None] == np.arange(V7X_LANES)[None, :], BF16)


def _rwkv(rw_p, mu, w0s, w2s, a0s, a2s, k_k, k_a, r_k):
    tt, w = rw_p.shape
    lc = RWKV_CHUNK
    nch = tt // lc
    bw = BRANCH_W
    sub = V7X_SUBLANES
    zeros = jnp.zeros((DECAY_LORA, bw), F32)
    w2p = jnp.stack([jnp.concatenate([w2s[0], zeros]), jnp.concatenate([zeros, w2s[1]])]).astype(BF16)
    a2p = jnp.stack([jnp.concatenate([a2s[0], zeros]), jnp.concatenate([zeros, a2s[1]])]).astype(BF16)
    w0 = jnp.stack(w0s).reshape(2, 1, bw)
    a0 = jnp.stack(a0s).reshape(2, 1, bw)
    e_mat = _head_indicator(bw, RWKV_DIM)
    const = lambda *shape: pl.BlockSpec(shape, lambda t: tuple(0 for _ in shape))

    def chunk_specs(d):
        cidx = lambda t: _rwkv_chunk_index(d, t, nch)
        return [pl.BlockSpec((lc, w), lambda t: (cidx(t), 0)),
                pl.BlockSpec((sub, w), lambda t: (jnp.maximum(cidx(t) * (lc // sub) - 1, 0), 0)),
                pl.BlockSpec((sub, w), lambda t: (jnp.minimum((cidx(t) + 1) * (lc // sub), tt // sub - 1), 0))]

    out_spec = lambda d: pl.BlockSpec((lc, bw), lambda t: (_rwkv_chunk_index(d, t, nch), 0))
    return pl.pallas_call(
        functools.partial(_rwkv_kernel, nch=nch),
        grid=(nch,),
        in_specs=chunk_specs(0) + chunk_specs(1) + [
            const(1, w), const(2, 1, bw), const(2, 2 * DECAY_LORA, bw), const(2, 1, bw), const(2, 2 * ICLR_LORA, bw),
            const(1, bw), const(1, bw), const(1, bw), const(bw, V7X_LANES), const(V7X_LANES, bw)],
        out_specs=[out_spec(0), out_spec(1), out_spec(0), out_spec(1)],
        out_shape=[jax.ShapeDtypeStruct((tt, bw), F32)] * 4,
        scratch_shapes=[pltpu.VMEM((2, RWKV_HEADS // 2, RWKV_DIM, 2 * RWKV_DIM), F32)],
        compiler_params=_cparams(("arbitrary",), 48 * 1024 * 1024),
        name="rwkv7",
    )(rw_p, rw_p, rw_p, rw_p, rw_p, rw_p, mu.reshape(1, w), w0, w2p, a0, a2p,
      k_k.reshape(1, bw), k_a.reshape(1, bw), r_k.reshape(1, bw), e_mat, e_mat.T)


def _odd_out_kernel(na_ref, wkvf_ref, wkvb_ref, bonf_ref, bonb_ref, gd_ref, lng_ref, lnb_ref, e_ref, et_ref, w_ref,
                    res_ref, mod_ref, gpost_ref, o_ref):
    d = o_ref.shape[-1]
    e, et = e_ref[...], et_ref[...]
    inv = 1.0 / RWKV_DIM
    nrow = o_ref.shape[0] // ODD_OUT_ROW_GROUPS
    groups = [slice(g * nrow, (g + 1) * nrow) for g in range(ODD_OUT_ROW_GROUPS)]
    wkv = [wkvf_ref[g, :] + wkvb_ref[g, :] for g in groups]
    mean = [_head_sums([x], e, et)[0] * inv for x in wkv]
    xc = [x - mu for x, mu in zip(wkv, mean)]
    var = [_head_sums([x * x], e, et)[0] * inv for x in xc]
    y = [xc[i] * lax.rsqrt(var[i] + RWKV_GN_EPS) * lng_ref[...] + lnb_ref[...] + bonf_ref[g, :] + bonb_ref[g, :]
         for i, g in enumerate(groups)]
    m2 = jnp.concatenate([(gd_ref[g, :].astype(F32) * y[i]).astype(BF16) for i, g in enumerate(groups)], axis=0)
    out = _dot(na_ref[...], w_ref[:BRANCH_W, :]) + _dot(m2, w_ref[BRANCH_W:, :])
    gate = mod_ref[0][0:1, 2 * d:]
    o_ref[...] = res_ref[...] + gate * (_rms_rows(out) * gpost_ref[...])


def _odd_out(na_g, wkv_f, wkv_b, bon_f, bon_b, gd, lnx_g, lnx_b, w_out, stream, mod, g_post):
    t, bw = na_g.shape
    d = stream.shape[1]
    tm = ROW_BLOCK
    off = CTX_LEN // tm
    lat = lambda w: pl.BlockSpec((tm, w), lambda i: (off + i, 0))
    vec = lambda w: pl.BlockSpec((1, w), lambda i: (0, 0))
    e_mat = _head_indicator(bw, RWKV_DIM)
    return pl.pallas_call(
        _odd_out_kernel,
        grid=(t // tm,),
        in_specs=[pl.BlockSpec((tm, bw), lambda i: (i, 0)), lat(bw), lat(bw), lat(bw), lat(bw), lat(bw),
                  vec(bw), vec(bw),
                  pl.BlockSpec((bw, V7X_LANES), lambda i: (0, 0)),
                  pl.BlockSpec((V7X_LANES, bw), lambda i: (0, 0)),
                  pl.BlockSpec((2 * bw, d), lambda i: (0, 0)),
                  lat(d),
                  pl.BlockSpec((1, V7X_SUBLANES, 3 * d), lambda i: (1, 0, 0)),
                  vec(d)],
        out_specs=pl.BlockSpec((tm, d), lambda i: (i, 0)),
        out_shape=jax.ShapeDtypeStruct((t, d), F32),
        compiler_params=_cparams(("arbitrary",), 48 * 1024 * 1024),
        name="odd_out_proj",
    )(na_g, wkv_f, wkv_b, bon_f, bon_b, gd, lnx_g.reshape(1, bw), lnx_b.reshape(1, bw), e_mat, e_mat.T, w_out,
      stream, mod, g_post.reshape(1, d))


def _rope_tables(t):
    n_rows = t // GRID_W
    n_freq = RET_DK // 4
    inv = ROPE_THETA ** (-jnp.arange(n_freq, dtype=F32) / n_freq)
    row_ang = jnp.arange(n_rows, dtype=F32)[:, None] * inv
    col_ang = jnp.arange(GRID_W, dtype=F32)[:, None] * inv

    def table(fn):
        rows_part = jnp.broadcast_to(fn(row_ang)[:, None, :], (n_rows, GRID_W, n_freq))
        cols_part = jnp.broadcast_to(fn(col_ang)[None, :, :], (n_rows, GRID_W, n_freq))
        return jnp.repeat(jnp.concatenate([rows_part, cols_part], axis=-1).reshape(t, 2 * n_freq), 2, axis=-1)

    cos = table(jnp.cos)
    sin = table(jnp.sin) * jnp.tile(jnp.asarray([-1.0, 1.0], F32), RET_DK // 2)
    cos = jnp.concatenate([jnp.ones((CTX_LEN, RET_DK), F32), cos], axis=0)
    sin = jnp.concatenate([jnp.zeros((CTX_LEN, RET_DK), F32), sin], axis=0)
    return cos, sin


def kernel(x, c, ctx, c_ctx, w_mod, b_mod, g_pre, g_post, ev_w_in, ev_w_out, ret_decay_fwd, ret_decay_bwd, gqa_q_norm, gqa_k_norm, od_w_in, od_w_out, na_rpb, rwkv_shift_mu, rwkv_w0_fwd, rwkv_w2_fwd, rwkv_w0_bwd, rwkv_w2_bwd, rwkv_a0_fwd, rwkv_a2_fwd, rwkv_a0_bwd, rwkv_a2_bwd, rwkv_k_k, rwkv_k_a, rwkv_r_k, rwkv_lnx_g, rwkv_lnx_b):
    assert x.shape[0] == 1 and DEPTH == 2 and RET_DK == GQA_DIM
    t = x.shape[1]
    tt = t + CTX_LEN
    assert t % ROW_BLOCK == 0 and tt % ATT_KV_BLOCK == 0 and t % GRID_W == 0
    x2, ctx2 = x[0], ctx[0]
    mod = _modulation(c, c_ctx, w_mod, b_mod)
    cos_t, sin_t = _rope_tables(t)

    qa, ka, va, qb, kbt, vb, ga, gb = _even_in_proj(x2, ctx2, mod, g_pre[0], ev_w_in[0].astype(BF16), cos_t, sin_t,
                                                    gqa_q_norm[0], gqa_k_norm[0])
    ret_g = _retention(qa, ka, va, ga, ret_decay_fwd[0], ret_decay_bwd[0])
    att_lat = _gqa_attention(qb, kbt, vb, gb, q_row0=CTX_LEN, n_q=t, n_keys=tt, kv_block=ATT_KV_BLOCK, out_rows=t)
    att_ctx = _gqa_attention(qb, kbt, vb, gb, q_row0=0, n_q=CTX_LEN, n_keys=CTX_LEN, kv_block=CTX_LEN,
                             out_rows=CTX_LEN)
    stream1 = _out_proj(ret_g, att_ctx, att_lat, ev_w_out[0].astype(BF16), ctx2, x2, mod, 0, g_post[0])

    q, k, v, rw_p, gc, gd = _odd_in_proj(stream1, mod, g_pre[1], od_w_in[0].astype(BF16))
    na_g = _neighbourhood_attention(q, k, v, gc, na_rpb[0])
    rw = _rwkv(rw_p, rwkv_shift_mu[0], (rwkv_w0_fwd[0], rwkv_w0_bwd[0]), (rwkv_w2_fwd[0], rwkv_w2_bwd[0]),
               (rwkv_a0_fwd[0], rwkv_a0_bwd[0]), (rwkv_a2_fwd[0], rwkv_a2_bwd[0]),
               rwkv_k_k[0], rwkv_k_a[0], rwkv_r_k[0].reshape(-1))
    out = _odd_out(na_g, *rw, gd, rwkv_lnx_g[0], rwkv_lnx_b[0], od_w_out[0].astype(BF16), stream1, mod, g_post[1])
    return out[None]
```

```python
import functools

import jax
import jax.numpy as jnp
import numpy as np
from jax import lax
from jax.experimental import pallas as pl
from jax.experimental.pallas import tpu as pltpu

F32 = jnp.float32
BF16 = jnp.bfloat16

D_MODEL = 1024
DEPTH = 2
GRID_W = 64
CTX_LEN = 256
BRANCH_W = D_MODEL
RET_HEADS = 4
RET_DK = 128
RET_DV = BRANCH_W // RET_HEADS
RET_CHUNK = 128
GQA_HEADS = 8
GQA_KV_HEADS = 2
GQA_DIM = BRANCH_W // GQA_HEADS
GQA_GROUP = GQA_HEADS // GQA_KV_HEADS
NA_HEADS = 16
NA_DIM = BRANCH_W // NA_HEADS
NA_WIN_ROWS = 8
NA_WIN_COLS = 16
RWKV_HEADS = 16
RWKV_DIM = BRANCH_W // RWKV_HEADS
DECAY_LORA = 64
ICLR_LORA = 64
ROPE_THETA = 10000.0
NORM_EPS = 1e-6
RWKV_GN_EPS = 64e-5
SHIFT_W = 3 * BRANCH_W + 2 * DECAY_LORA + 2 * ICLR_LORA
EVEN_IN = 2 * RET_HEADS * RET_DK + BRANCH_W + GQA_HEADS * GQA_DIM + 2 * GQA_KV_HEADS * GQA_DIM + 2 * BRANCH_W
ODD_IN = 3 * BRANCH_W + SHIFT_W + 2 * BRANCH_W

V7X_LANES = 128
V7X_SUBLANES = 8
V7X_VMEM_BYTES = 64 * 1024 * 1024

ROW_BLOCK = CTX_LEN
RWKV_CHUNK = 64
RWKV_INV_BASE = 8
RET_CHUNKS_PER_STEP = 2
ATT_Q_BLOCK = 256
ATT_KV_BLOCK = 1280
LOG2_E = 1.4426950408889634
NA_BLOCK_ROWS = 4
NA_UNION_ROWS = 12
NA_BLOCKS_PER_STEP = 2
ODD_OUT_ROW_GROUPS = 2
MASK_VALUE = -1e30


def _vmem_limit(nbytes):
    return int(min(V7X_VMEM_BYTES - 4 * 1024 * 1024, max(32 * 1024 * 1024, nbytes)))


def _cparams(sem, vmem_bytes):
    return pltpu.CompilerParams(dimension_semantics=sem, vmem_limit_bytes=_vmem_limit(vmem_bytes))


def _silu(x):
    return x / (1.0 + jnp.exp(-x))


def _sigmoid(x):
    return 1.0 / (1.0 + jnp.exp(-x))


def _dot(a, b):
    return jnp.dot(a, b, preferred_element_type=F32)


def _dot_nt(a, b):
    return lax.dot_general(a, b, (((1,), (1,)), ((), ())), preferred_element_type=F32)


def _split2(x):
    hi = x.astype(BF16)
    lo = (x - hi.astype(F32)).astype(BF16)
    return hi, lo


def _dot_split_rhs(a_bf16, x):
    hi, lo = _split2(x)
    return _dot(a_bf16, hi) + _dot(a_bf16, lo)


def _head_sums(xs, e, et):
    n = xs[0].shape[0]

    def stacked_dot(vals, w):
        parts = [t for x in vals for t in _split2(x)]
        y = _dot(jnp.concatenate(parts, axis=0), w)
        return [y[2 * i * n:(2 * i + 1) * n] + y[(2 * i + 1) * n:(2 * i + 2) * n] for i in range(len(vals))]

    return stacked_dot(stacked_dot(xs, e), et)


def _rms_rows(x):
    return x * lax.rsqrt(jnp.mean(x * x, axis=-1, keepdims=True) + NORM_EPS)


def _rope(t, cos, sin_signed, even):
    nxt = pltpu.roll(t, t.shape[1] - 1, 1)
    prv = pltpu.roll(t, 1, 1)
    return t * cos + jnp.where(even, nxt, prv) * sin_signed


def _mod_kernel(cc_ref, w_ref, b_ref, o_ref):
    s = _silu(cc_ref[...])
    o_ref[0] = jnp.dot(s, w_ref[0], preferred_element_type=F32, precision=lax.Precision.HIGHEST) + b_ref[0]


def _modulation(c, c_ctx, w_mod, b_mod):
    d = c.shape[-1]
    cc = jnp.concatenate([c[:1], c_ctx[None, :], jnp.zeros((V7X_SUBLANES - 2, d), F32)], axis=0)
    return pl.pallas_call(
        _mod_kernel,
        grid=(DEPTH, 3),
        in_specs=[pl.BlockSpec((V7X_SUBLANES, d), lambda l, j: (0, 0)),
                  pl.BlockSpec((1, d, d), lambda l, j: (l, 0, j)),
                  pl.BlockSpec((1, 1, d), lambda l, j: (l, 0, j))],
        out_specs=pl.BlockSpec((1, V7X_SUBLANES, d), lambda l, j: (l, 0, j)),
        out_shape=jax.ShapeDtypeStruct((DEPTH, V7X_SUBLANES, 3 * d), F32),
        compiler_params=_cparams(("arbitrary", "arbitrary"), 40 * 1024 * 1024),
        name="modulation",
    )(cc, w_mod, b_mod.reshape(DEPTH, 1, 3 * d))


def _adaln(xb, mod, is_ctx, g_pre):
    d = xb.shape[-1]
    m = jnp.where(is_ctx, mod[1:2, :], mod[0:1, :])
    shift, scale = m[:, :d], m[:, d:2 * d]
    return (_rms_rows(xb) * g_pre) * (1.0 + scale) + shift


def _even_in_kernel(x_ref, ctx_ref, mod_ref, gpre_ref, w_ref, cos_ref, sin_ref, qn_ref, kn_ref,
                    qa_ref, ka_ref, va_ref, qb_ref, kbt_ref, vb_ref, ga_ref, gb_ref):
    is_ctx = pl.program_id(0) == 0
    xb = jnp.where(is_ctx, ctx_ref[...], x_ref[...])
    hb = _adaln(xb, mod_ref[0], is_ctx, gpre_ref[...]).astype(BF16)
    cos, sin_s = cos_ref[...], sin_ref[...]
    even = (lax.broadcasted_iota(jnp.int32, cos.shape, 1) & 1) == 0
    o = 0

    def seg(width):
        nonlocal o
        y = _dot(hb, w_ref[:, o:o + width])
        o += width
        return y

    y = seg(RET_HEADS * RET_DK)
    for h in range(RET_HEADS):
        sl = slice(h * RET_DK, (h + 1) * RET_DK)
        qa_ref[:, sl] = _rope(y[:, sl], cos, sin_s, even).astype(BF16)
    y = seg(RET_HEADS * RET_DK)
    for h in range(RET_HEADS):
        sl = slice(h * RET_DK, (h + 1) * RET_DK)
        ka_ref[:, sl] = _rope(y[:, sl] * RET_DK ** -0.5, cos, sin_s, even).astype(BF16)
    va_ref[...] = seg(BRANCH_W).astype(BF16)
    y = seg(GQA_HEADS * GQA_DIM)
    for h in range(GQA_HEADS):
        sl = slice(h * GQA_DIM, (h + 1) * GQA_DIM)
        t = _rms_rows(y[:, sl]) * qn_ref[...]
        qb_ref[:, sl] = (_rope(t, cos, sin_s, even) * (GQA_DIM ** -0.5 * LOG2_E)).astype(BF16)
    y = seg(GQA_KV_HEADS * GQA_DIM)
    for h in range(GQA_KV_HEADS):
        sl = slice(h * GQA_DIM, (h + 1) * GQA_DIM)
        t = _rope(_rms_rows(y[:, sl]) * kn_ref[...], cos, sin_s, even)
        kbt_ref[sl, :] = t.T.astype(BF16)
    y = seg(GQA_KV_HEADS * GQA_DIM)
    for h in range(GQA_KV_HEADS):
        vb_ref[:, 2 * h * GQA_DIM:(2 * h + 1) * GQA_DIM] = y[:, h * GQA_DIM:(h + 1) * GQA_DIM].astype(BF16)
        vb_ref[:, (2 * h + 1) * GQA_DIM:(2 * h + 2) * GQA_DIM] = jnp.ones((y.shape[0], GQA_DIM), BF16)
    ga_ref[...] = _silu(seg(BRANCH_W)).astype(BF16)
    gb_ref[...] = _silu(seg(BRANCH_W)).astype(BF16)


def _even_in_proj(x, ctx, mod, g_pre, w_in, cos_t, sin_t, q_norm, k_norm):
    t, d = x.shape
    tt = t + CTX_LEN
    nblk = tt // ROW_BLOCK
    tm = ROW_BLOCK
    kvw = GQA_KV_HEADS * GQA_DIM
    row = lambda w: pl.BlockSpec((tm, w), lambda i: (i, 0))
    const = lambda shape: pl.BlockSpec(shape, lambda i: tuple(0 for _ in shape))
    outs = [((tt, RET_HEADS * RET_DK), row(RET_HEADS * RET_DK)),
            ((tt, RET_HEADS * RET_DK), row(RET_HEADS * RET_DK)),
            ((tt, BRANCH_W), row(BRANCH_W)),
            ((tt, GQA_HEADS * GQA_DIM), row(GQA_HEADS * GQA_DIM)),
            ((kvw, tt), pl.BlockSpec((kvw, tm), lambda i: (0, i))),
            ((tt, 2 * kvw), row(2 * kvw)),
            ((tt, BRANCH_W), row(BRANCH_W)),
            ((tt, BRANCH_W), row(BRANCH_W))]
    return pl.pallas_call(
        _even_in_kernel,
        grid=(nblk,),
        in_specs=[pl.BlockSpec((tm, d), lambda i: (jnp.maximum(i - 1, 0), 0)),
                  const((CTX_LEN, d)),
                  pl.BlockSpec((1, V7X_SUBLANES, 3 * d), lambda i: (0, 0, 0)),
                  const((1, d)),
                  const((d, EVEN_IN)),
                  row(RET_DK), row(RET_DK),
                  const((1, GQA_DIM)), const((1, GQA_DIM))],
        out_specs=[s for _, s in outs],
        out_shape=[jax.ShapeDtypeStruct(shp, BF16) for shp, _ in outs],
        compiler_params=_cparams(("arbitrary",), 2 * d * EVEN_IN * 2 + 16 * 1024 * 1024),
        name="even_in_proj",
    )(x, ctx, mod, g_pre.reshape(1, d), w_in, cos_t, sin_t, q_norm.reshape(1, -1), k_norm.reshape(1, -1))


def _log_sigmoid(x):
    return jnp.minimum(x, 0.0) - jnp.log(1.0 + jnp.exp(-jnp.abs(x)))


def _ret_bwd_block(t, nblocks):
    nctx = CTX_LEN // (RET_CHUNKS_PER_STEP * RET_CHUNK)
    return jnp.where(t < nctx, nctx - 1 - t, nblocks - 1 - (t - nctx))


def _ret_state_kernel(dec_ref, k_ref, v_ref, sb_ref, s_scr):
    c = RET_CHUNK

    @pl.when(pl.program_id(0) == 0)
    def _():
        s_scr[...] = jnp.zeros_like(s_scr)

    lg = _log_sigmoid(dec_ref[...])
    pos = lax.broadcasted_iota(jnp.int32, (c, RET_DK), 0).astype(F32)
    hs = range(RET_HEADS)
    lgb = [lg[RET_HEADS + h:RET_HEADS + h + 1, :] for h in hs]
    state = [s_scr[h] for h in hs]
    for sub in reversed(range(RET_CHUNKS_PER_STEP)):
        rows = slice(sub * c, (sub + 1) * c)
        for h in hs:
            sb_ref[sub, h] = state[h].astype(BF16)
        kzt = [(k_ref[rows, h * RET_DK:(h + 1) * RET_DK].astype(F32) * jnp.exp(pos * lgb[h])).T.astype(BF16)
               for h in hs]
        state = [jnp.exp(c * lgb[h][:, :1]) * state[h] + _dot(kzt[h], v_ref[rows, h * RET_DV:(h + 1) * RET_DV])
                 for h in hs]
    for h in hs:
        s_scr[h] = state[h]


def _ret_out_kernel(dec_ref, q_ref, k_ref, v_ref, g_ref, sb_ref, o_ref, s_scr):
    c = RET_CHUNK

    @pl.when(pl.program_id(0) == 0)
    def _():
        s_scr[...] = jnp.zeros_like(s_scr)

    lg = _log_sigmoid(dec_ref[...])
    ii = lax.broadcasted_iota(jnp.int32, (c, c), 0)
    jj = lax.broadcasted_iota(jnp.int32, (c, c), 1)
    dlt = (ii - jj).astype(F32)
    pos = lax.broadcasted_iota(jnp.int32, (c, RET_DK), 0).astype(F32)
    hs = range(RET_HEADS)
    lgf = [lg[h:h + 1, :] for h in hs]
    lgb = [lg[RET_HEADS + h:RET_HEADS + h + 1, :] for h in hs]
    dec = [jnp.where(dlt > 0, jnp.exp(jnp.maximum(dlt, 0.0) * lgf[h]),
                     jnp.where(dlt < 0, jnp.exp(jnp.maximum(-dlt, 0.0) * lgb[h]), 2.0)) for h in hs]
    state = [s_scr[h] for h in hs]
    for sub in range(RET_CHUNKS_PER_STEP):
        rows = slice(sub * c, (sub + 1) * c)
        q = [q_ref[rows, h * RET_DK:(h + 1) * RET_DK] for h in hs]
        k = [k_ref[rows, h * RET_DK:(h + 1) * RET_DK] for h in hs]
        v = [v_ref[rows, h * RET_DV:(h + 1) * RET_DV] for h in hs]
        p = [(_dot_nt(q[h], k[h]) * dec[h]).astype(BF16) for h in hs]
        qf = [q[h].astype(F32) for h in hs]
        ret = [_dot(p[h], v[h])
               + _dot((qf[h] * jnp.exp((pos + 1.0) * lgf[h])).astype(BF16), state[h].astype(BF16))
               + _dot((qf[h] * jnp.exp((c - pos) * lgb[h])).astype(BF16), sb_ref[sub, h]) for h in hs]
        kzt = [(k[h].astype(F32) * jnp.exp((c - 1.0 - pos) * lgf[h])).T.astype(BF16) for h in hs]
        state = [jnp.exp(c * lgf[h][:, :1]) * state[h] + _dot(kzt[h], v[h]) for h in hs]
        xc = [ret[h] - jnp.mean(ret[h], axis=-1, keepdims=True) for h in hs]
        y = [xc[h] * lax.rsqrt(jnp.mean(xc[h] * xc[h], axis=-1, keepdims=True) + NORM_EPS) for h in hs]
        for h in hs:
            sl = slice(h * RET_DV, (h + 1) * RET_DV)
            o_ref[rows, sl] = (g_ref[rows, sl].astype(F32) * y[h]).astype(BF16)
    for h in hs:
        s_scr[h] = state[h]


def _retention(qa, ka, va, ga, dec_f, dec_b):
    tt = qa.shape[0]
    r = RET_CHUNKS_PER_STEP
    rows = r * RET_CHUNK
    assert tt % rows == 0 and CTX_LEN % rows == 0
    n = tt // rows
    dec = jnp.broadcast_to(jnp.concatenate([dec_f, dec_b]).astype(F32)[:, None], (2 * RET_HEADS, V7X_LANES))
    kw, vw = RET_HEADS * RET_DK, BRANCH_W
    dec_spec = pl.BlockSpec((2 * RET_HEADS, V7X_LANES), lambda t: (0, 0))
    state_shape = (RET_HEADS, RET_DK, RET_DV)
    sb = pl.pallas_call(
        _ret_state_kernel,
        grid=(n,),
        in_specs=[dec_spec,
                  pl.BlockSpec((rows, kw), lambda t: (_ret_bwd_block(t, n), 0)),
                  pl.BlockSpec((rows, vw), lambda t: (_ret_bwd_block(t, n), 0))],
        out_specs=pl.BlockSpec((r,) + state_shape, lambda t: (_ret_bwd_block(t, n), 0, 0, 0)),
        out_shape=jax.ShapeDtypeStruct((n * r,) + state_shape, BF16),
        scratch_shapes=[pltpu.VMEM(state_shape, F32)],
        compiler_params=_cparams(("arbitrary",), 32 * 1024 * 1024),
        name="retention_state",
    )(dec, ka, va)
    return pl.pallas_call(
        _ret_out_kernel,
        grid=(n,),
        in_specs=[dec_spec,
                  pl.BlockSpec((rows, kw), lambda t: (t, 0)),
                  pl.BlockSpec((rows, kw), lambda t: (t, 0)),
                  pl.BlockSpec((rows, vw), lambda t: (t, 0)),
                  pl.BlockSpec((rows, vw), lambda t: (t, 0)),
                  pl.BlockSpec((r,) + state_shape, lambda t: (t, 0, 0, 0))],
        out_specs=pl.BlockSpec((rows, vw), lambda t: (t, 0)),
        out_shape=jax.ShapeDtypeStruct((tt, vw), BF16),
        scratch_shapes=[pltpu.VMEM(state_shape, F32)],
        compiler_params=_cparams(("arbitrary",), 32 * 1024 * 1024),
        name="retention_out",
    )(dec, qa, ka, va, ga, sb)


def _gqa_kernel(q_ref, kt_ref, v_ref, g_ref, o_ref, s0_scr, s1_scr, *, kv_block, n_kv):
    tq = q_ref.shape[0]
    heads = [slice(h * GQA_DIM, (h + 1) * GQA_DIM) for h in range(GQA_GROUP)]
    q = jnp.concatenate([q_ref[:, sl] for sl in heads], axis=0)
    rows = GQA_GROUP * tq

    def scores(j, s_ref):
        start = pl.multiple_of(j * kv_block, V7X_LANES)
        s_ref[...] = _dot(q, kt_ref[:, pl.ds(start, kv_block)])

    def softmax_pv(j, s_ref, carry):
        m, acc = carry
        start = pl.multiple_of(j * kv_block, V7X_LANES)
        s = s_ref[...]
        m_new = jnp.maximum(m, jnp.max(s, axis=-1, keepdims=True))
        p = jnp.exp2(s - m_new).astype(BF16)
        acc = jnp.exp2(m - m_new) * acc + _dot(p, v_ref[pl.ds(start, kv_block), :])
        return m_new, acc

    def body(i, carry):
        j = 2 * i
        scores(j + 1, s1_scr)
        carry = softmax_pv(j, s0_scr, carry)
        scores(j + 2, s0_scr)
        return softmax_pv(j + 1, s1_scr, carry)

    carry = (jnp.full((rows, 1), MASK_VALUE, F32), jnp.zeros((rows, 2 * GQA_DIM), F32))
    scores(0, s0_scr)
    carry = lax.fori_loop(0, (n_kv - 1) // 2, body, carry)
    if n_kv % 2 == 0:
        scores(n_kv - 1, s1_scr)
        carry = softmax_pv(n_kv - 2, s0_scr, carry)
        _, acc = softmax_pv(n_kv - 1, s1_scr, carry)
    else:
        _, acc = softmax_pv(n_kv - 1, s0_scr, carry)
    out = acc[:, :GQA_DIM] / acc[:, GQA_DIM:]
    for h, sl in enumerate(heads):
        o_ref[:, sl] = (g_ref[:, sl].astype(F32) * out[h * tq:(h + 1) * tq]).astype(BF16)


def _gqa_attention(qb, kbt, vb, gb, *, q_row0, n_q, n_keys, kv_block, out_rows):
    gw = GQA_GROUP * GQA_DIM
    tq = ATT_Q_BLOCK
    qoff = q_row0 // tq
    return pl.pallas_call(
        functools.partial(_gqa_kernel, kv_block=kv_block, n_kv=n_keys // kv_block),
        grid=(GQA_KV_HEADS, n_q // tq),
        in_specs=[pl.BlockSpec((tq, gw), lambda g, i: (qoff + i, g)),
                  pl.BlockSpec((GQA_DIM, n_keys), lambda g, i: (g, 0)),
                  pl.BlockSpec((n_keys, 2 * GQA_DIM), lambda g, i: (0, g)),
                  pl.BlockSpec((tq, gw), lambda g, i: (qoff + i, g))],
        out_specs=pl.BlockSpec((tq, gw), lambda g, i: (i, g)),
        out_shape=jax.ShapeDtypeStruct((out_rows, GQA_HEADS * GQA_DIM), BF16),
        scratch_shapes=[pltpu.VMEM((GQA_GROUP * tq, kv_block), F32)] * 2,
        compiler_params=_cparams(("arbitrary", "arbitrary"), 56 * 1024 * 1024),
        name="gqa_attention",
    )(qb, kbt, vb, gb)


def _out_proj_kernel(m1_ref, m2c_ref, m2l_ref, w_ref, resc_ref, resl_ref, mod_ref, gpost_ref, o_ref):
    d = o_ref.shape[-1]
    is_ctx = pl.program_id(0) == 0
    m2 = jnp.where(is_ctx, m2c_ref[...], m2l_ref[...])
    y = _dot(m1_ref[...], w_ref[:BRANCH_W, :]) + _dot(m2, w_ref[BRANCH_W:, :])
    mod = mod_ref[0]
    gate = jnp.where(is_ctx, mod[1:2, 2 * d:], mod[0:1, 2 * d:])
    res = jnp.where(is_ctx, resc_ref[...], resl_ref[...])
    o_ref[...] = res + gate * (_rms_rows(y) * gpost_ref[...])


def _out_proj(m1, m2_ctx, m2_lat, w_out, res_ctx, res_lat, mod, layer, g_post):
    tt = m1.shape[0]
    d = res_lat.shape[1]
    tm = ROW_BLOCK
    row = lambda w: pl.BlockSpec((tm, w), lambda i: (i, 0))
    lat = lambda w: pl.BlockSpec((tm, w), lambda i: (jnp.maximum(i - 1, 0), 0))
    ctx = lambda w: pl.BlockSpec((CTX_LEN, w), lambda i: (0, 0))
    return pl.pallas_call(
        _out_proj_kernel,
        grid=(tt // tm,),
        in_specs=[row(BRANCH_W), ctx(BRANCH_W), lat(BRANCH_W),
                  pl.BlockSpec((2 * BRANCH_W, d), lambda i: (0, 0)),
                  ctx(d), lat(d),
                  pl.BlockSpec((1, V7X_SUBLANES, 3 * d), lambda i: (layer, 0, 0)),
                  pl.BlockSpec((1, d), lambda i: (0, 0))],
        out_specs=row(d),
        out_shape=jax.ShapeDtypeStruct((tt, d), F32),
        compiler_params=_cparams(("arbitrary",), 40 * 1024 * 1024),
        name="out_proj",
    )(m1, m2_ctx, m2_lat, w_out, res_ctx, res_lat, mod, g_post.reshape(1, d))


def _odd_in_kernel(s_ref, mod_ref, gpre_ref, w_ref, q_ref, k_ref, v_ref, rw_ref, gc_ref, gd_ref):
    is_ctx = pl.program_id(0) == 0
    hb = _adaln(s_ref[...], mod_ref[0], is_ctx, gpre_ref[...]).astype(BF16)
    o = 0

    def seg(width):
        nonlocal o
        y = _dot(hb, w_ref[:, o:o + width])
        o += width
        return y

    q_ref[...] = (seg(BRANCH_W) * NA_DIM ** -0.5).astype(BF16)
    k_ref[...] = seg(BRANCH_W).astype(BF16)
    v_ref[...] = seg(BRANCH_W).astype(BF16)
    rw_ref[...] = seg(SHIFT_W)
    gc_ref[...] = _silu(seg(BRANCH_W)).astype(BF16)
    gd_ref[...] = _silu(seg(BRANCH_W)).astype(BF16)


def _odd_in_proj(stream, mod, g_pre, w_in):
    tt, d = stream.shape
    tm = ROW_BLOCK
    row = lambda w: pl.BlockSpec((tm, w), lambda i: (i, 0))
    widths = [(BRANCH_W, BF16), (BRANCH_W, BF16), (BRANCH_W, BF16), (SHIFT_W, F32), (BRANCH_W, BF16), (BRANCH_W, BF16)]
    return pl.pallas_call(
        _odd_in_kernel,
        grid=(tt // tm,),
        in_specs=[row(d),
                  pl.BlockSpec((1, V7X_SUBLANES, 3 * d), lambda i: (1, 0, 0)),
                  pl.BlockSpec((1, d), lambda i: (0, 0)),
                  pl.BlockSpec((d, ODD_IN), lambda i: (0, 0), pipeline_mode=pl.Buffered(1))],
        out_specs=[row(w) for w, _ in widths],
        out_shape=[jax.ShapeDtypeStruct((tt, w), dt) for w, dt in widths],
        compiler_params=_cparams(("arbitrary",), d * ODD_IN * 2 + 24 * 1024 * 1024),
        name="odd_in_proj",
    )(stream, mod, g_pre.reshape(1, d), w_in)


def _na_kernel(*refs, rows):
    nb = NA_BLOCKS_PER_STEP
    q_refs, g_refs = refs[:nb], refs[nb:2 * nb]
    k_ref, v_ref, bias_ref, o_ref = refs[2 * nb:]
    span = NA_UNION_ROWS * GRID_W
    bq = NA_BLOCK_ROWS * GRID_W
    first = lax.broadcasted_iota(jnp.int32, (bq, 2 * NA_DIM), 1) < NA_DIM
    kc, vc = k_ref[:CTX_LEN, :], v_ref[:CTX_LEN, :]
    blocks = range(nb)
    lhs, ku, vu, cls = [], [], [], []
    for b in blocks:
        qr0 = (pl.program_id(1) * nb + b) * NA_BLOCK_ROWS
        u0 = jnp.clip(qr0 - NA_WIN_ROWS // 2, 0, rows - NA_UNION_ROWS)
        cls.append(jnp.where(qr0 == 0, 0, jnp.where(qr0 == rows - NA_BLOCK_ROWS, 2, 1)))
        start = pl.multiple_of(CTX_LEN + u0 * GRID_W, GRID_W)
        ku.append(k_ref[pl.ds(start, span), :])
        vu.append(v_ref[pl.ds(start, span), :])
        q = q_refs[b][...]
        zero = jnp.zeros_like(q)
        lhs.append(jnp.concatenate([jnp.where(first, q, zero), jnp.where(first, zero, q)], axis=0))
    bias = [jnp.concatenate([bias_ref[0, cls[b]], bias_ref[1, cls[b]]], axis=0) for b in blocks]
    sw = [_dot_nt(lhs[b], ku[b]) + bias[b] for b in blocks]
    sc = [_dot_nt(lhs[b], kc) for b in blocks]
    m = [jnp.maximum(jnp.max(sw[b], axis=-1, keepdims=True), jnp.max(sc[b], axis=-1, keepdims=True)) for b in blocks]
    pw = [jnp.exp(sw[b] - m[b]) for b in blocks]
    pc = [jnp.exp(sc[b] - m[b]) for b in blocks]
    l = [jnp.sum(pw[b], axis=-1, keepdims=True) + jnp.sum(pc[b], axis=-1, keepdims=True) for b in blocks]
    o = [(_dot(pw[b].astype(BF16), vu[b]) + _dot(pc[b].astype(BF16), vc)) / l[b] for b in blocks]
    for b in blocks:
        out = jnp.where(first, o[b][:bq], o[b][bq:])
        o_ref[b * bq:(b + 1) * bq, :] = (g_refs[b][...].astype(F32) * out).astype(BF16)


def _na_bias_table(rpb):
    half = NA_WIN_ROWS // 2
    cols = np.arange(GRID_W)
    c0 = np.clip(cols - NA_WIN_COLS // 2, 0, GRID_W - NA_WIN_COLS)
    valid_c = (cols[None, :] >= c0[:, None]) & (cols[None, :] < c0[:, None] + NA_WIN_COLS)
    dc = np.clip(cols[None, :] - cols[:, None] + NA_WIN_COLS - 1, 0, 2 * NA_WIN_COLS - 2)
    a = np.arange(NA_BLOCK_ROWS)[:, None]
    i = np.arange(NA_UNION_ROWS)[None, :]
    dr = np.stack([i - a, i - a - half, i - a - (NA_UNION_ROWS - NA_BLOCK_ROWS)])
    w0 = np.stack([0 * a, a, (NA_UNION_ROWS - NA_WIN_ROWS) + 0 * a])
    valid_r = (i[None] >= w0) & (i[None] < w0 + NA_WIN_ROWS)
    onehot = (np.arange(2 * NA_WIN_COLS - 1)[:, None, None] == dc[None]) & valid_c[None]
    tiles = jnp.einsum('hrd,dqk->hrqk', rpb.astype(F32), jnp.asarray(onehot, F32), precision=lax.Precision.HIGHEST)
    tiles = jnp.where(jnp.asarray(valid_c)[None, None], tiles, MASK_VALUE)
    masked_tile = jnp.full((NA_HEADS, GRID_W, GRID_W), MASK_VALUE, F32)

    def tile(c, qa, ki):
        return tiles[:, dr[c, qa, ki] + NA_WIN_ROWS - 1] if valid_r[c, qa, ki] else masked_tile

    return jnp.stack([jnp.concatenate([jnp.concatenate([tile(c, qa, ki) for ki in range(NA_UNION_ROWS)], axis=-1)
                                       for qa in range(NA_BLOCK_ROWS)], axis=-2) for c in range(3)], axis=1)


def _neighbourhood_attention(q, k, v, gc, rpb):
    tt = q.shape[0]
    t = tt - CTX_LEN
    rows = t // GRID_W
    nb = NA_BLOCKS_PER_STEP
    assert NA_BLOCK_ROWS == NA_WIN_ROWS // 2 and rows >= NA_UNION_ROWS and rows % (nb * NA_BLOCK_ROWS) == 0
    bias = _na_bias_table(rpb)
    pw = 2 * NA_DIM
    bq = NA_BLOCK_ROWS * GRID_W
    qoff = CTX_LEN // bq
    assert CTX_LEN % bq == 0
    blk = lambda b: pl.BlockSpec((bq, pw), lambda p, i: (qoff + i * nb + b, p))
    resident = pl.BlockSpec((tt, pw), lambda p, i: (0, p))
    return pl.pallas_call(
        functools.partial(_na_kernel, rows=rows),
        grid=(NA_HEADS // 2, rows // (nb * NA_BLOCK_ROWS)),
        in_specs=[blk(b) for b in range(nb)] + [blk(b) for b in range(nb)] + [
            resident, resident,
            pl.BlockSpec((2,) + bias.shape[1:], lambda p, i: (p, 0, 0, 0))],
        out_specs=pl.BlockSpec((nb * bq, pw), lambda p, i: (i, p)),
        out_shape=jax.ShapeDtypeStruct((t, BRANCH_W), BF16),
        compiler_params=_cparams(("arbitrary", "arbitrary"), 48 * 1024 * 1024),
        name="neighbourhood_attention",
    )(*([q] * nb + [gc] * nb + [k, v, bias]))


def _rwkv_chunk_index(d, t, nch):
    nctx = CTX_LEN // RWKV_CHUNK
    return t if d == 0 else jnp.where(t < nctx, nctx - 1 - t, nch - 1 - (t - nctx))


def _rwkv_prepare(d, c, nch, main_ref, prev_ref, next_ref, mu_ref, w0_ref, w2_ref, a0_ref, a2_ref, kk_ref, ka_ref,
                  rk_ref, e_ref, et_ref, bon_ref):
    lc = RWKV_CHUNK
    nctx = CTX_LEN // lc
    sgn = 1 - 2 * d
    p = main_ref[...]
    row = lax.broadcasted_iota(jnp.int32, p.shape, 0)
    first_zero = jnp.logical_or(c == 0, c == nctx)
    last_zero = jnp.logical_or(c == nctx - 1, c == nch - 1)
    pr = jnp.where(first_zero, 0.0, prev_ref[V7X_SUBLANES - 1:V7X_SUBLANES, :])
    nx = jnp.where(last_zero, 0.0, next_ref[0:1, :])
    prev = jnp.where(row == 0, pr, pltpu.roll(p, 1, 0))
    nxt = jnp.where(row == lc - 1, nx, pltpu.roll(p, lc - 1, 0))
    z = p + (0.5 * (prev + nxt) - p) * mu_ref[...]

    bw = BRANCH_W
    r, k, v = z[:, :bw], z[:, bw:2 * bw], z[:, 2 * bw:3 * bw]
    zw = z[:, 3 * bw:3 * bw + 2 * DECAY_LORA]
    za = z[:, 3 * bw + 2 * DECAY_LORA:]
    lw = w0_ref[d] + _dot(jnp.tanh(zw).astype(BF16), w2_ref[d])
    ld = -float(np.exp(-0.5)) * _sigmoid(lw)
    asig = _sigmoid(a0_ref[d] + _dot(za.astype(BF16), a2_ref[d]))
    kk = k * kk_ref[...]
    kd = k * (1.0 + (asig - 1.0) * ka_ref[...])
    kk_sq, rkd = _head_sums([kk * kk, r * kd * rk_ref[...]], e_ref[...], et_ref[...])
    kkn = kk / jnp.maximum(jnp.sqrt(kk_sq), 1e-12)
    a_vec = -kkn
    b_vec = kkn * asig
    bon_ref[...] = rkd * v

    ti = lax.broadcasted_iota(jnp.int32, (lc, lc), 0)
    si = lax.broadcasted_iota(jnp.int32, (lc, lc), 1)
    cum = _dot_split_rhs(jnp.where(sgn * (ti - si) >= 0, 1.0, 0.0).astype(BF16), ld)
    tot = jnp.sum(ld, axis=0, keepdims=True)
    rem = jnp.exp(tot - cum)
    pinv = jnp.exp(-cum)
    rt = r * jnp.exp(cum)
    kt = kd * pinv
    bt = b_vec * pinv
    at = a_vec * jnp.exp(cum - ld)
    bh = b_vec * rem
    kh = kd * rem
    pend = jnp.exp(tot)
    return dict(at=at, rt=rt, bt=bt, kt=kt, bh=bh, kh=kh, v=v, pend=pend)


def _rwkv_kernel(main_f, prev_f, next_f, main_b, prev_b, next_b, mu_ref, w0_ref, w2_ref, a0_ref, a2_ref, kk_ref,
                 ka_ref, rk_ref, e_ref, et_ref, wkv_f, wkv_b, bon_f, bon_b, s_scr, *, nch):
    lc = RWKV_CHUNK
    hd = RWKV_DIM
    t = pl.program_id(0)

    @pl.when(t == 0)
    def _():
        s_scr[...] = jnp.zeros_like(s_scr)

    shared = (mu_ref, w0_ref, w2_ref, a0_ref, a2_ref, kk_ref, ka_ref, rk_ref, e_ref, et_ref)
    rows = [_rwkv_prepare(0, _rwkv_chunk_index(0, t, nch), nch, main_f, prev_f, next_f, *shared, bon_f),
            _rwkv_prepare(1, _rwkv_chunk_index(1, t, nch), nch, main_b, prev_b, next_b, *shared, bon_b)]
    wkv_refs = (wkv_f, wkv_b)

    pw = 2 * hd
    lane = lax.broadcasted_iota(jnp.int32, (lc, pw), 1)
    tok = lax.broadcasted_iota(jnp.int32, (lc, pw), 0)
    first = lane < hd
    fwd_diff = tok - (lane & (hd - 1))
    eye = (fwd_diff == 0).astype(F32)
    rr = lax.broadcasted_iota(jnp.int32, (pw, pw), 0)
    cc = lax.broadcasted_iota(jnp.int32, (pw, pw), 1)
    same_head = (rr < hd) == (cc < hd)
    probs = [(d, p) for d in range(2) for p in range(RWKV_HEADS // 2)]
    n = range(len(probs))
    incl = [(fwd_diff >= 0) if d == 0 else (fwd_diff <= 0) for d, _ in probs]
    strict = [(fwd_diff > 0) if d == 0 else (fwd_diff < 0) for d, _ in probs]

    def bdiag(m):
        m = m.astype(BF16)
        zero = jnp.zeros_like(m)
        return jnp.concatenate([jnp.where(first, m, zero), jnp.where(first, zero, m)], axis=0)

    def part(name):
        return [rows[d][name][:, p * pw:(p + 1) * pw] for d, p in probs]

    at_p, rt_p, v_p = part("at"), part("rt"), part("v")
    bt_p, kt_p, pend_p = part("bt"), part("kt"), part("pend")
    bh_p = [m.astype(BF16) for m in part("bh")]
    kh_p = [m.astype(BF16) for m in part("kh")]
    lhs = [jnp.concatenate([at_p[i], rt_p[i]], axis=0).astype(BF16) for i in n]
    gram = [_dot_nt(lhs[i], jnp.concatenate([bdiag(bt_p[i]), bdiag(kt_p[i])], axis=0)) for i in n]
    gb = [g[:, :pw] for g in gram]
    gk = [g[:, pw:] for g in gram]
    aab = [jnp.where(strict[i], gb[i][:lc], 0.0) for i in n]
    arb = [jnp.where(incl[i], gb[i][lc:], 0.0) for i in n]
    aak = [jnp.where(strict[i], gk[i][:lc], 0.0) for i in n]
    ark = [jnp.where(incl[i], gk[i][lc:], 0.0) for i in n]
    vbd = [bdiag(m) for m in v_p]
    aakv = [_dot(aak[i].astype(BF16), vbd[i]) for i in n]
    s_tok = lane & (hd - 1)

    def same_block(size):
        shift = int(np.log2(size))
        return (tok >> shift) == (s_tok >> shift)

    nd = [jnp.where(same_block(RWKV_INV_BASE), m, 0.0) for m in aab]
    x = [eye + m for m in nd]
    nk = [_dot(m.astype(BF16), bdiag(m)) for m in nd]
    for _ in range(int(np.log2(RWKV_INV_BASE)) - 2):
        out = [_dot(jnp.concatenate([nk[i], x[i]], axis=0).astype(BF16), bdiag(nk[i])) for i in n]
        nk = [o[:lc] for o in out]
        x = [x[i] + out[i][lc:] for i in n]
    x = [x[i] + _dot(x[i].astype(BF16), bdiag(nk[i])) for i in n]
    size = RWKV_INV_BASE
    while size < lc:
        couple = jnp.logical_and(same_block(2 * size), jnp.logical_not(same_block(size)))
        xn = [_dot(x[i].astype(BF16), bdiag(jnp.where(couple, aab[i], 0.0))) for i in n]
        x = [x[i] + _dot(xn[i].astype(BF16), bdiag(x[i])) for i in n]
        size *= 2
    xc = [_dot(x[i].astype(BF16), jnp.concatenate([bdiag(at_p[i]), bdiag(aakv[i])], axis=1)) for i in n]
    ahat = [m[:, :pw] for m in xc]
    wmat = [m[:, pw:] for m in xc]
    rhat = [rt_p[i] + _dot(arb[i].astype(BF16), bdiag(ahat[i])) for i in n]
    y0 = [_dot(jnp.concatenate([arb[i], ark[i]], axis=1).astype(BF16),
               jnp.concatenate([bdiag(wmat[i]), vbd[i]], axis=0)) for i in n]
    mab = [jnp.where(same_head, _dot(ahat[i].T.astype(BF16), bh_p[i]), 0.0).astype(BF16) for i in n]
    gtf = [_dot(jnp.concatenate([wmat[i].T, v_p[i].T], axis=1).astype(BF16),
                jnp.concatenate([bh_p[i], kh_p[i]], axis=0)) for i in n]
    for i, (d, p) in enumerate(probs):
        s0 = s_scr[d, p]
        wkv_refs[d][:, p * pw:(p + 1) * pw] = _dot_nt(rhat[i].astype(BF16), bdiag(s0)) + y0[i]
        s_scr[d, p] = (s0 * pend_p[i] + _dot(s0.astype(BF16), mab[i])
                       + jnp.where(first, gtf[i][:lc], gtf[i][lc:]))


def _head_indicator(width, head_dim):
    idx = np.arange(width) // head_dim
    return jnp.asarray(idx[:, None] == np.arange(V7X_LANES)[None, :], BF16)


def _rwkv(rw_p, mu, w0s, w2s, a0s, a2s, k_k, k_a, r_k):
    tt, w = rw_p.shape
    lc = RWKV_CHUNK
    nch = tt // lc
    bw = BRANCH_W
    sub = V7X_SUBLANES
    zeros = jnp.zeros((DECAY_LORA, bw), F32)
    w2p = jnp.stack([jnp.concatenate([w2s[0], zeros]), jnp.concatenate([zeros, w2s[1]])]).astype(BF16)
    a2p = jnp.stack([jnp.concatenate([a2s[0], zeros]), jnp.concatenate([zeros, a2s[1]])]).astype(BF16)
    w0 = jnp.stack(w0s).reshape(2, 1, bw)
    a0 = jnp.stack(a0s).reshape(2, 1, bw)
    e_mat = _head_indicator(bw, RWKV_DIM)
    const = lambda *shape: pl.BlockSpec(shape, lambda t: tuple(0 for _ in shape))

    def chunk_specs(d):
        cidx = lambda t: _rwkv_chunk_index(d, t, nch)
        return [pl.BlockSpec((lc, w), lambda t: (cidx(t), 0)),
                pl.BlockSpec((sub, w), lambda t: (jnp.maximum(cidx(t) * (lc // sub) - 1, 0), 0)),
                pl.BlockSpec((sub, w), lambda t: (jnp.minimum((cidx(t) + 1) * (lc // sub), tt // sub - 1), 0))]

    out_spec = lambda d: pl.BlockSpec((lc, bw), lambda t: (_rwkv_chunk_index(d, t, nch), 0))
    return pl.pallas_call(
        functools.partial(_rwkv_kernel, nch=nch),
        grid=(nch,),
        in_specs=chunk_specs(0) + chunk_specs(1) + [
            const(1, w), const(2, 1, bw), const(2, 2 * DECAY_LORA, bw), const(2, 1, bw), const(2, 2 * ICLR_LORA, bw),
            const(1, bw), const(1, bw), const(1, bw), const(bw, V7X_LANES), const(V7X_LANES, bw)],
        out_specs=[out_spec(0), out_spec(1), out_spec(0), out_spec(1)],
        out_shape=[jax.ShapeDtypeStruct((tt, bw), F32)] * 4,
        scratch_shapes=[pltpu.VMEM((2, RWKV_HEADS // 2, RWKV_DIM, 2 * RWKV_DIM), F32)],
        compiler_params=_cparams(("arbitrary",), 48 * 1024 * 1024),
        name="rwkv7",
    )(rw_p, rw_p, rw_p, rw_p, rw_p, rw_p, mu.reshape(1, w), w0, w2p, a0, a2p,
      k_k.reshape(1, bw), k_a.reshape(1, bw), r_k.reshape(1, bw), e_mat, e_mat.T)


def _odd_out_kernel(na_ref, wkvf_ref, wkvb_ref, bonf_ref, bonb_ref, gd_ref, lng_ref, lnb_ref, e_ref, et_ref, w_ref,
                    res_ref, mod_ref, gpost_ref, o_ref):
    d = o_ref.shape[-1]
    e, et = e_ref[...], et_ref[...]
    inv = 1.0 / RWKV_DIM
    nrow = o_ref.shape[0] // ODD_OUT_ROW_GROUPS
    groups = [slice(g * nrow, (g + 1) * nrow) for g in range(ODD_OUT_ROW_GROUPS)]
    wkv = [wkvf_ref[g, :] + wkvb_ref[g, :] for g in groups]
    mean = [_head_sums([x], e, et)[0] * inv for x in wkv]
    xc = [x - mu for x, mu in zip(wkv, mean)]
    var = [_head_sums([x * x], e, et)[0] * inv for x in xc]
    y = [xc[i] * lax.rsqrt(var[i] + RWKV_GN_EPS) * lng_ref[...] + lnb_ref[...] + bonf_ref[g, :] + bonb_ref[g, :]
         for i, g in enumerate(groups)]
    m2 = jnp.concatenate([(gd_ref[g, :].astype(F32) * y[i]).astype(BF16) for i, g in enumerate(groups)], axis=0)
    out = _dot(na_ref[...], w_ref[:BRANCH_W, :]) + _dot(m2, w_ref[BRANCH_W:, :])
    gate = mod_ref[0][0:1, 2 * d:]
    o_ref[...] = res_ref[...] + gate * (_rms_rows(out) * gpost_ref[...])


def _odd_out(na_g, wkv_f, wkv_b, bon_f, bon_b, gd, lnx_g, lnx_b, w_out, stream, mod, g_post):
    t, bw = na_g.shape
    d = stream.shape[1]
    tm = ROW_BLOCK
    off = CTX_LEN // tm
    lat = lambda w: pl.BlockSpec((tm, w), lambda i: (off + i, 0))
    vec = lambda w: pl.BlockSpec((1, w), lambda i: (0, 0))
    e_mat = _head_indicator(bw, RWKV_DIM)
    return pl.pallas_call(
        _odd_out_kernel,
        grid=(t // tm,),
        in_specs=[pl.BlockSpec((tm, bw), lambda i: (i, 0)), lat(bw), lat(bw), lat(bw), lat(bw), lat(bw),
                  vec(bw), vec(bw),
                  pl.BlockSpec((bw, V7X_LANES), lambda i: (0, 0)),
                  pl.BlockSpec((V7X_LANES, bw), lambda i: (0, 0)),
                  pl.BlockSpec((2 * bw, d), lambda i: (0, 0)),
                  lat(d),
                  pl.BlockSpec((1, V7X_SUBLANES, 3 * d), lambda i: (1, 0, 0)),
                  vec(d)],
        out_specs=pl.BlockSpec((tm, d), lambda i: (i, 0)),
        out_shape=jax.ShapeDtypeStruct((t, d), F32),
        compiler_params=_cparams(("arbitrary",), 48 * 1024 * 1024),
        name="odd_out_proj",
    )(na_g, wkv_f, wkv_b, bon_f, bon_b, gd, lnx_g.reshape(1, bw), lnx_b.reshape(1, bw), e_mat, e_mat.T, w_out,
      stream, mod, g_post.reshape(1, d))


def _rope_tables(t):
    n_rows = t // GRID_W
    n_freq = RET_DK // 4
    inv = ROPE_THETA ** (-jnp.arange(n_freq, dtype=F32) / n_freq)
    row_ang = jnp.arange(n_rows, dtype=F32)[:, None] * inv
    col_ang = jnp.arange(GRID_W, dtype=F32)[:, None] * inv

    def table(fn):
        rows_part = jnp.broadcast_to(fn(row_ang)[:, None, :], (n_rows, GRID_W, n_freq))
        cols_part = jnp.broadcast_to(fn(col_ang)[None, :, :], (n_rows, GRID_W, n_freq))
        return jnp.repeat(jnp.concatenate([rows_part, cols_part], axis=-1).reshape(t, 2 * n_freq), 2, axis=-1)

    cos = table(jnp.cos)
    sin = table(jnp.sin) * jnp.tile(jnp.asarray([-1.0, 1.0], F32), RET_DK // 2)
    cos = jnp.concatenate([jnp.ones((CTX_LEN, RET_DK), F32), cos], axis=0)
    sin = jnp.concatenate([jnp.zeros((CTX_LEN, RET_DK), F32), sin], axis=0)
    return cos, sin


def kernel(x, c, ctx, c_ctx, w_mod, b_mod, g_pre, g_post, ev_w_in, ev_w_out, ret_decay_fwd, ret_decay_bwd, gqa_q_norm, gqa_k_norm, od_w_in, od_w_out, na_rpb, rwkv_shift_mu, rwkv_w0_fwd, rwkv_w2_fwd, rwkv_w0_bwd, rwkv_w2_bwd, rwkv_a0_fwd, rwkv_a2_fwd, rwkv_a0_bwd, rwkv_a2_bwd, rwkv_k_k, rwkv_k_a, rwkv_r_k, rwkv_lnx_g, rwkv_lnx_b):
    assert x.shape[0] == 1 and DEPTH == 2 and RET_DK == GQA_DIM
    t = x.shape[1]
    tt = t + CTX_LEN
    assert t % ROW_BLOCK == 0 and tt % ATT_KV_BLOCK == 0 and t % GRID_W == 0
    x2, ctx2 = x[0], ctx[0]
    mod = _modulation(c, c_ctx, w_mod, b_mod)
    cos_t, sin_t = _rope_tables(t)

    qa, ka, va, qb, kbt, vb, ga, gb = _even_in_proj(x2, ctx2, mod, g_pre[0], ev_w_in[0].astype(BF16), cos_t, sin_t,
                                                    gqa_q_norm[0], gqa_k_norm[0])
    ret_g = _retention(qa, ka, va, ga, ret_decay_fwd[0], ret_decay_bwd[0])
    att_lat = _gqa_attention(qb, kbt, vb, gb, q_row0=CTX_LEN, n_q=t, n_keys=tt, kv_block=ATT_KV_BLOCK, out_rows=t)
    att_ctx = _gqa_attention(qb, kbt, vb, gb, q_row0=0, n_q=CTX_LEN, n_keys=CTX_LEN, kv_block=CTX_LEN,
                             out_rows=CTX_LEN)
    stream1 = _out_proj(ret_g, att_ctx, att_lat, ev_w_out[0].astype(BF16), ctx2, x2, mod, 0, g_post[0])

    q, k, v, rw_p, gc, gd = _odd_in_proj(stream1, mod, g_pre[1], od_w_in[0].astype(BF16))
    na_g = _neighbourhood_attention(q, k, v, gc, na_rpb[0])
    rw = _rwkv(rw_p, rwkv_shift_mu[0], (rwkv_w0_fwd[0], rwkv_w0_bwd[0]), (rwkv_w2_fwd[0], rwkv_w2_bwd[0]),
               (rwkv_a0_fwd[0], rwkv_a0_bwd[0]), (rwkv_a2_fwd[0], rwkv_a2_bwd[0]),
               rwkv_k_k[0], rwkv_k_a[0], rwkv_r_k[0].reshape(-1))
    out = _odd_out(na_g, *rw, gd, rwkv_lnx_g[0], rwkv_lnx_b[0], od_w_out[0].astype(BF16), stream1, mod, g_post[1])
    return out[None]
```

```python
import functools

import jax
import jax.numpy as jnp
import numpy as np
from jax import lax
from jax.experimental import pallas as pl
from jax.experimental.pallas import tpu as pltpu

F32 = jnp.float32
BF16 = jnp.bfloat16

D_MODEL = 1024
DEPTH = 2
GRID_W = 64
CTX_LEN = 256
BRANCH_W = D_MODEL
RET_HEADS = 4
RET_DK = 128
RET_DV = BRANCH_W // RET_HEADS
RET_CHUNK = 128
GQA_HEADS = 8
GQA_KV_HEADS = 2
GQA_DIM = BRANCH_W // GQA_HEADS
GQA_GROUP = GQA_HEADS // GQA_KV_HEADS
NA_HEADS = 16
NA_DIM = BRANCH_W // NA_HEADS
NA_WIN_ROWS = 8
NA_WIN_COLS = 16
RWKV_HEADS = 16
RWKV_DIM = BRANCH_W // RWKV_HEADS
DECAY_LORA = 64
ICLR_LORA = 64
ROPE_THETA = 10000.0
NORM_EPS = 1e-6
RWKV_GN_EPS = 64e-5
SHIFT_W = 3 * BRANCH_W + 2 * DECAY_LORA + 2 * ICLR_LORA
EVEN_IN = 2 * RET_HEADS * RET_DK + BRANCH_W + GQA_HEADS * GQA_DIM + 2 * GQA_KV_HEADS * GQA_DIM + 2 * BRANCH_W
ODD_IN = 3 * BRANCH_W + SHIFT_W + 2 * BRANCH_W

V7X_LANES = 128
V7X_SUBLANES = 8
V7X_VMEM_BYTES = 64 * 1024 * 1024

ROW_BLOCK = CTX_LEN
RWKV_CHUNK = 64
RWKV_INV_BASE = 8
RET_CHUNKS_PER_STEP = 2
ATT_Q_BLOCK = 256
ATT_KV_BLOCK = 1280
ATT_LOOP_UNROLL = 3
LOG2_E = 1.4426950408889634
NA_BLOCK_ROWS = 4
NA_UNION_ROWS = 12
NA_BLOCKS_PER_STEP = 2
ODD_OUT_ROW_GROUPS = 2
MASK_VALUE = -1e30


def _vmem_limit(nbytes):
    return int(min(V7X_VMEM_BYTES - 4 * 1024 * 1024, max(32 * 1024 * 1024, nbytes)))


def _cparams(sem, vmem_bytes):
    return pltpu.CompilerParams(dimension_semantics=sem, vmem_limit_bytes=_vmem_limit(vmem_bytes))


def _silu(x):
    return x / (1.0 + jnp.exp(-x))


def _sigmoid(x):
    return 1.0 / (1.0 + jnp.exp(-x))


def _dot(a, b):
    return jnp.dot(a, b, preferred_element_type=F32)


def _dot_nt(a, b):
    return lax.dot_general(a, b, (((1,), (1,)), ((), ())), preferred_element_type=F32)


def _split2(x):
    hi = x.astype(BF16)
    lo = (x - hi.astype(F32)).astype(BF16)
    return hi, lo


def _dot_split_rhs(a_bf16, x):
    hi, lo = _split2(x)
    return _dot(a_bf16, hi) + _dot(a_bf16, lo)


def _head_sums(xs, e, et):
    n = xs[0].shape[0]

    def stacked_dot(vals, w):
        parts = [t for x in vals for t in _split2(x)]
        y = _dot(jnp.concatenate(parts, axis=0), w)
        return [y[2 * i * n:(2 * i + 1) * n] + y[(2 * i + 1) * n:(2 * i + 2) * n] for i in range(len(vals))]

    return stacked_dot(stacked_dot(xs, e), et)


def _rms_rows(x):
    return x * lax.rsqrt(jnp.mean(x * x, axis=-1, keepdims=True) + NORM_EPS)


def _rope(t, cos, sin_signed, even):
    nxt = pltpu.roll(t, t.shape[1] - 1, 1)
    prv = pltpu.roll(t, 1, 1)
    return t * cos + jnp.where(even, nxt, prv) * sin_signed


def _mod_kernel(cc_ref, w_ref, b_ref, o_ref):
    s = _silu(cc_ref[...])
    o_ref[0] = jnp.dot(s, w_ref[0], preferred_element_type=F32, precision=lax.Precision.HIGHEST) + b_ref[0]


def _modulation(c, c_ctx, w_mod, b_mod):
    d = c.shape[-1]
    cc = jnp.concatenate([c[:1], c_ctx[None, :], jnp.zeros((V7X_SUBLANES - 2, d), F32)], axis=0)
    return pl.pallas_call(
        _mod_kernel,
        grid=(DEPTH, 3),
        in_specs=[pl.BlockSpec((V7X_SUBLANES, d), lambda l, j: (0, 0)),
                  pl.BlockSpec((1, d, d), lambda l, j: (l, 0, j)),
                  pl.BlockSpec((1, 1, d), lambda l, j: (l, 0, j))],
        out_specs=pl.BlockSpec((1, V7X_SUBLANES, d), lambda l, j: (l, 0, j)),
        out_shape=jax.ShapeDtypeStruct((DEPTH, V7X_SUBLANES, 3 * d), F32),
        compiler_params=_cparams(("arbitrary", "arbitrary"), 40 * 1024 * 1024),
        name="modulation",
    )(cc, w_mod, b_mod.reshape(DEPTH, 1, 3 * d))


def _adaln(xb, mod, is_ctx, g_pre):
    d = xb.shape[-1]
    m = jnp.where(is_ctx, mod[1:2, :], mod[0:1, :])
    shift, scale = m[:, :d], m[:, d:2 * d]
    return (_rms_rows(xb) * g_pre) * (1.0 + scale) + shift


def _even_in_kernel(x_ref, ctx_ref, mod_ref, gpre_ref, w_ref, cos_ref, sin_ref, qn_ref, kn_ref,
                    qa_ref, ka_ref, va_ref, qb_ref, kbt_ref, vb_ref, ga_ref, gb_ref):
    is_ctx = pl.program_id(0) == 0
    xb = jnp.where(is_ctx, ctx_ref[...], x_ref[...])
    hb = _adaln(xb, mod_ref[0], is_ctx, gpre_ref[...]).astype(BF16)
    cos, sin_s = cos_ref[...], sin_ref[...]
    even = (lax.broadcasted_iota(jnp.int32, cos.shape, 1) & 1) == 0
    o = 0

    def seg(width):
        nonlocal o
        y = _dot(hb, w_ref[:, o:o + width])
        o += width
        return y

    y = seg(RET_HEADS * RET_DK)
    for h in range(RET_HEADS):
        sl = slice(h * RET_DK, (h + 1) * RET_DK)
        qa_ref[:, sl] = _rope(y[:, sl], cos, sin_s, even).astype(BF16)
    y = seg(RET_HEADS * RET_DK)
    for h in range(RET_HEADS):
        sl = slice(h * RET_DK, (h + 1) * RET_DK)
        ka_ref[:, sl] = _rope(y[:, sl] * RET_DK ** -0.5, cos, sin_s, even).astype(BF16)
    va_ref[...] = seg(BRANCH_W).astype(BF16)
    y = seg(GQA_HEADS * GQA_DIM)
    for h in range(GQA_HEADS):
        sl = slice(h * GQA_DIM, (h + 1) * GQA_DIM)
        t = _rms_rows(y[:, sl]) * qn_ref[...]
        qb_ref[:, sl] = (_rope(t, cos, sin_s, even) * (GQA_DIM ** -0.5 * LOG2_E)).astype(BF16)
    y = seg(GQA_KV_HEADS * GQA_DIM)
    for h in range(GQA_KV_HEADS):
        sl = slice(h * GQA_DIM, (h + 1) * GQA_DIM)
        t = _rope(_rms_rows(y[:, sl]) * kn_ref[...], cos, sin_s, even)
        kbt_ref[sl, :] = t.T.astype(BF16)
    y = seg(GQA_KV_HEADS * GQA_DIM)
    for h in range(GQA_KV_HEADS):
        vb_ref[:, 2 * h * GQA_DIM:(2 * h + 1) * GQA_DIM] = y[:, h * GQA_DIM:(h + 1) * GQA_DIM].astype(BF16)
        vb_ref[:, (2 * h + 1) * GQA_DIM:(2 * h + 2) * GQA_DIM] = jnp.ones((y.shape[0], GQA_DIM), BF16)
    ga_ref[...] = _silu(seg(BRANCH_W)).astype(BF16)
    gb_ref[...] = _silu(seg(BRANCH_W)).astype(BF16)


def _even_in_proj(x, ctx, mod, g_pre, w_in, cos_t, sin_t, q_norm, k_norm):
    t, d = x.shape
    tt = t + CTX_LEN
    nblk = tt // ROW_BLOCK
    tm = ROW_BLOCK
    kvw = GQA_KV_HEADS * GQA_DIM
    row = lambda w: pl.BlockSpec((tm, w), lambda i: (i, 0))
    const = lambda shape: pl.BlockSpec(shape, lambda i: tuple(0 for _ in shape))
    outs = [((tt, RET_HEADS * RET_DK), row(RET_HEADS * RET_DK)),
            ((tt, RET_HEADS * RET_DK), row(RET_HEADS * RET_DK)),
            ((tt, BRANCH_W), row(BRANCH_W)),
            ((tt, GQA_HEADS * GQA_DIM), row(GQA_HEADS * GQA_DIM)),
            ((kvw, tt), pl.BlockSpec((kvw, tm), lambda i: (0, i))),
            ((tt, 2 * kvw), row(2 * kvw)),
            ((tt, BRANCH_W), row(BRANCH_W)),
            ((tt, BRANCH_W), row(BRANCH_W))]
    return pl.pallas_call(
        _even_in_kernel,
        grid=(nblk,),
        in_specs=[pl.BlockSpec((tm, d), lambda i: (jnp.maximum(i - 1, 0), 0)),
                  const((CTX_LEN, d)),
                  pl.BlockSpec((1, V7X_SUBLANES, 3 * d), lambda i: (0, 0, 0)),
                  const((1, d)),
                  const((d, EVEN_IN)),
                  row(RET_DK), row(RET_DK),
                  const((1, GQA_DIM)), const((1, GQA_DIM))],
        out_specs=[s for _, s in outs],
        out_shape=[jax.ShapeDtypeStruct(shp, BF16) for shp, _ in outs],
        compiler_params=_cparams(("arbitrary",), 2 * d * EVEN_IN * 2 + 16 * 1024 * 1024),
        name="even_in_proj",
    )(x, ctx, mod, g_pre.reshape(1, d), w_in, cos_t, sin_t, q_norm.reshape(1, -1), k_norm.reshape(1, -1))


def _log_sigmoid(x):
    return jnp.minimum(x, 0.0) - jnp.log(1.0 + jnp.exp(-jnp.abs(x)))


def _ret_bwd_block(t, nblocks):
    nctx = CTX_LEN // (RET_CHUNKS_PER_STEP * RET_CHUNK)
    return jnp.where(t < nctx, nctx - 1 - t, nblocks - 1 - (t - nctx))


def _ret_state_kernel(dec_ref, k_ref, v_ref, sb_ref, s_scr):
    c = RET_CHUNK

    @pl.when(pl.program_id(0) == 0)
    def _():
        s_scr[...] = jnp.zeros_like(s_scr)

    lg = _log_sigmoid(dec_ref[...])
    pos = lax.broadcasted_iota(jnp.int32, (c, RET_DK), 0).astype(F32)
    hs = range(RET_HEADS)
    lgb = [lg[RET_HEADS + h:RET_HEADS + h + 1, :] for h in hs]
    state = [s_scr[h] for h in hs]
    for sub in reversed(range(RET_CHUNKS_PER_STEP)):
        rows = slice(sub * c, (sub + 1) * c)
        for h in hs:
            sb_ref[sub, h] = state[h].astype(BF16)
        kzt = [(k_ref[rows, h * RET_DK:(h + 1) * RET_DK].astype(F32) * jnp.exp(pos * lgb[h])).T.astype(BF16)
               for h in hs]
        state = [jnp.exp(c * lgb[h][:, :1]) * state[h] + _dot(kzt[h], v_ref[rows, h * RET_DV:(h + 1) * RET_DV])
                 for h in hs]
    for h in hs:
        s_scr[h] = state[h]


def _ret_out_kernel(dec_ref, q_ref, k_ref, v_ref, g_ref, sb_ref, o_ref, s_scr):
    c = RET_CHUNK

    @pl.when(pl.program_id(0) == 0)
    def _():
        s_scr[...] = jnp.zeros_like(s_scr)

    lg = _log_sigmoid(dec_ref[...])
    ii = lax.broadcasted_iota(jnp.int32, (c, c), 0)
    jj = lax.broadcasted_iota(jnp.int32, (c, c), 1)
    dlt = (ii - jj).astype(F32)
    pos = lax.broadcasted_iota(jnp.int32, (c, RET_DK), 0).astype(F32)
    hs = range(RET_HEADS)
    lgf = [lg[h:h + 1, :] for h in hs]
    lgb = [lg[RET_HEADS + h:RET_HEADS + h + 1, :] for h in hs]
    dec = [jnp.where(dlt > 0, jnp.exp(jnp.maximum(dlt, 0.0) * lgf[h]),
                     jnp.where(dlt < 0, jnp.exp(jnp.maximum(-dlt, 0.0) * lgb[h]), 2.0)) for h in hs]
    state = [s_scr[h] for h in hs]
    for sub in range(RET_CHUNKS_PER_STEP):
        rows = slice(sub * c, (sub + 1) * c)
        q = [q_ref[rows, h * RET_DK:(h + 1) * RET_DK] for h in hs]
        k = [k_ref[rows, h * RET_DK:(h + 1) * RET_DK] for h in hs]
        v = [v_ref[rows, h * RET_DV:(h + 1) * RET_DV] for h in hs]
        p = [(_dot_nt(q[h], k[h]) * dec[h]).astype(BF16) for h in hs]
        qf = [q[h].astype(F32) for h in hs]
        ret = [_dot(p[h], v[h])
               + _dot((qf[h] * jnp.exp((pos + 1.0) * lgf[h])).astype(BF16), state[h].astype(BF16))
               + _dot((qf[h] * jnp.exp((c - pos) * lgb[h])).astype(BF16), sb_ref[sub, h]) for h in hs]
        kzt = [(k[h].astype(F32) * jnp.exp((c - 1.0 - pos) * lgf[h])).T.astype(BF16) for h in hs]
        state = [jnp.exp(c * lgf[h][:, :1]) * state[h] + _dot(kzt[h], v[h]) for h in hs]
        xc = [ret[h] - jnp.mean(ret[h], axis=-1, keepdims=True) for h in hs]
        y = [xc[h] * lax.rsqrt(jnp.mean(xc[h] * xc[h], axis=-1, keepdims=True) + NORM_EPS) for h in hs]
        for h in hs:
            sl = slice(h * RET_DV, (h + 1) * RET_DV)
            o_ref[rows, sl] = (g_ref[rows, sl].astype(F32) * y[h]).astype(BF16)
    for h in hs:
        s_scr[h] = state[h]


def _retention(qa, ka, va, ga, dec_f, dec_b):
    tt = qa.shape[0]
    r = RET_CHUNKS_PER_STEP
    rows = r * RET_CHUNK
    assert tt % rows == 0 and CTX_LEN % rows == 0
    n = tt // rows
    dec = jnp.broadcast_to(jnp.concatenate([dec_f, dec_b]).astype(F32)[:, None], (2 * RET_HEADS, V7X_LANES))
    kw, vw = RET_HEADS * RET_DK, BRANCH_W
    dec_spec = pl.BlockSpec((2 * RET_HEADS, V7X_LANES), lambda t: (0, 0))
    state_shape = (RET_HEADS, RET_DK, RET_DV)
    sb = pl.pallas_call(
        _ret_state_kernel,
        grid=(n,),
        in_specs=[dec_spec,
                  pl.BlockSpec((rows, kw), lambda t: (_ret_bwd_block(t, n), 0)),
                  pl.BlockSpec((rows, vw), lambda t: (_ret_bwd_block(t, n), 0))],
        out_specs=pl.BlockSpec((r,) + state_shape, lambda t: (_ret_bwd_block(t, n), 0, 0, 0)),
        out_shape=jax.ShapeDtypeStruct((n * r,) + state_shape, BF16),
        scratch_shapes=[pltpu.VMEM(state_shape, F32)],
        compiler_params=_cparams(("arbitrary",), 32 * 1024 * 1024),
        name="retention_state",
    )(dec, ka, va)
    return pl.pallas_call(
        _ret_out_kernel,
        grid=(n,),
        in_specs=[dec_spec,
                  pl.BlockSpec((rows, kw), lambda t: (t, 0)),
                  pl.BlockSpec((rows, kw), lambda t: (t, 0)),
                  pl.BlockSpec((rows, vw), lambda t: (t, 0)),
                  pl.BlockSpec((rows, vw), lambda t: (t, 0)),
                  pl.BlockSpec((r,) + state_shape, lambda t: (t, 0, 0, 0))],
        out_specs=pl.BlockSpec((rows, vw), lambda t: (t, 0)),
        out_shape=jax.ShapeDtypeStruct((tt, vw), BF16),
        scratch_shapes=[pltpu.VMEM(state_shape, F32)],
        compiler_params=_cparams(("arbitrary",), 32 * 1024 * 1024),
        name="retention_out",
    )(dec, qa, ka, va, ga, sb)


def _gqa_kernel(q_ref, kt_ref, v_ref, g_ref, o_ref, s0_scr, s1_scr, *, kv_block, n_kv):
    tq = q_ref.shape[0]
    heads = [slice(h * GQA_DIM, (h + 1) * GQA_DIM) for h in range(GQA_GROUP)]
    q = jnp.concatenate([q_ref[:, sl] for sl in heads], axis=0)
    rows = GQA_GROUP * tq

    def scores(j, s_ref):
        start = pl.multiple_of(j * kv_block, V7X_LANES)
        s_ref[...] = _dot(q, kt_ref[:, pl.ds(start, kv_block)])

    def softmax_pv(j, s_ref, carry):
        m, acc = carry
        start = pl.multiple_of(j * kv_block, V7X_LANES)
        s = s_ref[...]
        m_new = jnp.maximum(m, jnp.max(s, axis=-1, keepdims=True))
        p = jnp.exp2(s - m_new).astype(BF16)
        acc = jnp.exp2(m - m_new) * acc + _dot(p, v_ref[pl.ds(start, kv_block), :])
        return m_new, acc

    def body(i, carry):
        j = 2 * i
        scores(j + 1, s1_scr)
        carry = softmax_pv(j, s0_scr, carry)
        scores(j + 2, s0_scr)
        return softmax_pv(j + 1, s1_scr, carry)

    carry = (jnp.full((rows, 1), MASK_VALUE, F32), jnp.zeros((rows, 2 * GQA_DIM), F32))
    scores(0, s0_scr)
    carry = lax.fori_loop(0, (n_kv - 1) // 2, body, carry, unroll=ATT_LOOP_UNROLL)
    if n_kv % 2 == 0:
        scores(n_kv - 1, s1_scr)
        carry = softmax_pv(n_kv - 2, s0_scr, carry)
        _, acc = softmax_pv(n_kv - 1, s1_scr, carry)
    else:
        _, acc = softmax_pv(n_kv - 1, s0_scr, carry)
    out = acc[:, :GQA_DIM] / acc[:, GQA_DIM:]
    for h, sl in enumerate(heads):
        o_ref[:, sl] = (g_ref[:, sl].astype(F32) * out[h * tq:(h + 1) * tq]).astype(BF16)


def _gqa_attention(qb, kbt, vb, gb, *, q_row0, n_q, n_keys, kv_block, out_rows):
    gw = GQA_GROUP * GQA_DIM
    tq = ATT_Q_BLOCK
    qoff = q_row0 // tq
    return pl.pallas_call(
        functools.partial(_gqa_kernel, kv_block=kv_block, n_kv=n_keys // kv_block),
        grid=(GQA_KV_HEADS, n_q // tq),
        in_specs=[pl.BlockSpec((tq, gw), lambda g, i: (qoff + i, g)),
                  pl.BlockSpec((GQA_DIM, n_keys), lambda g, i: (g, 0)),
                  pl.BlockSpec((n_keys, 2 * GQA_DIM), lambda g, i: (0, g)),
                  pl.BlockSpec((tq, gw), lambda g, i: (qoff + i, g))],
        out_specs=pl.BlockSpec((tq, gw), lambda g, i: (i, g)),
        out_shape=jax.ShapeDtypeStruct((out_rows, GQA_HEADS * GQA_DIM), BF16),
        scratch_shapes=[pltpu.VMEM((GQA_GROUP * tq, kv_block), F32)] * 2,
        compiler_params=_cparams(("arbitrary", "arbitrary"), 56 * 1024 * 1024),
        name="gqa_attention",
    )(qb, kbt, vb, gb)


def _out_proj_kernel(m1_ref, m2c_ref, m2l_ref, w_ref, resc_ref, resl_ref, mod_ref, gpost_ref, o_ref):
    d = o_ref.shape[-1]
    is_ctx = pl.program_id(0) == 0
    m2 = jnp.where(is_ctx, m2c_ref[...], m2l_ref[...])
    y = _dot(m1_ref[...], w_ref[:BRANCH_W, :]) + _dot(m2, w_ref[BRANCH_W:, :])
    mod = mod_ref[0]
    gate = jnp.where(is_ctx, mod[1:2, 2 * d:], mod[0:1, 2 * d:])
    res = jnp.where(is_ctx, resc_ref[...], resl_ref[...])
    o_ref[...] = res + gate * (_rms_rows(y) * gpost_ref[...])


def _out_proj(m1, m2_ctx, m2_lat, w_out, res_ctx, res_lat, mod, layer, g_post):
    tt = m1.shape[0]
    d = res_lat.shape[1]
    tm = ROW_BLOCK
    row = lambda w: pl.BlockSpec((tm, w), lambda i: (i, 0))
    lat = lambda w: pl.BlockSpec((tm, w), lambda i: (jnp.maximum(i - 1, 0), 0))
    ctx = lambda w: pl.BlockSpec((CTX_LEN, w), lambda i: (0, 0))
    return pl.pallas_call(
        _out_proj_kernel,
        grid=(tt // tm,),
        in_specs=[row(BRANCH_W), ctx(BRANCH_W), lat(BRANCH_W),
                  pl.BlockSpec((2 * BRANCH_W, d), lambda i: (0, 0)),
                  ctx(d), lat(d),
                  pl.BlockSpec((1, V7X_SUBLANES, 3 * d), lambda i: (layer, 0, 0)),
                  pl.BlockSpec((1, d), lambda i: (0, 0))],
        out_specs=row(d),
        out_shape=jax.ShapeDtypeStruct((tt, d), F32),
        compiler_params=_cparams(("arbitrary",), 40 * 1024 * 1024),
        name="out_proj",
    )(m1, m2_ctx, m2_lat, w_out, res_ctx, res_lat, mod, g_post.reshape(1, d))


def _odd_in_kernel(s_ref, mod_ref, gpre_ref, w_ref, q_ref, k_ref, v_ref, rw_ref, gc_ref, gd_ref):
    is_ctx = pl.program_id(0) == 0
    hb = _adaln(s_ref[...], mod_ref[0], is_ctx, gpre_ref[...]).astype(BF16)
    o = 0

    def seg(width):
        nonlocal o
        y = _dot(hb, w_ref[:, o:o + width])
        o += width
        return y

    q_ref[...] = (seg(BRANCH_W) * NA_DIM ** -0.5).astype(BF16)
    k_ref[...] = seg(BRANCH_W).astype(BF16)
    v_ref[...] = seg(BRANCH_W).astype(BF16)
    rw_ref[...] = seg(SHIFT_W)
    gc_ref[...] = _silu(seg(BRANCH_W)).astype(BF16)
    gd_ref[...] = _silu(seg(BRANCH_W)).astype(BF16)


def _odd_in_proj(stream, mod, g_pre, w_in):
    tt, d = stream.shape
    tm = ROW_BLOCK
    row = lambda w: pl.BlockSpec((tm, w), lambda i: (i, 0))
    widths = [(BRANCH_W, BF16), (BRANCH_W, BF16), (BRANCH_W, BF16), (SHIFT_W, F32), (BRANCH_W, BF16), (BRANCH_W, BF16)]
    return pl.pallas_call(
        _odd_in_kernel,
        grid=(tt // tm,),
        in_specs=[row(d),
                  pl.BlockSpec((1, V7X_SUBLANES, 3 * d), lambda i: (1, 0, 0)),
                  pl.BlockSpec((1, d), lambda i: (0, 0)),
                  pl.BlockSpec((d, ODD_IN), lambda i: (0, 0), pipeline_mode=pl.Buffered(1))],
        out_specs=[row(w) for w, _ in widths],
        out_shape=[jax.ShapeDtypeStruct((tt, w), dt) for w, dt in widths],
        compiler_params=_cparams(("arbitrary",), d * ODD_IN * 2 + 24 * 1024 * 1024),
        name="odd_in_proj",
    )(stream, mod, g_pre.reshape(1, d), w_in)


def _na_kernel(*refs, rows):
    nb = NA_BLOCKS_PER_STEP
    q_refs, g_refs = refs[:nb], refs[nb:2 * nb]
    k_ref, v_ref, bias_ref, o_ref = refs[2 * nb:]
    span = NA_UNION_ROWS * GRID_W
    bq = NA_BLOCK_ROWS * GRID_W
    first = lax.broadcasted_iota(jnp.int32, (bq, 2 * NA_DIM), 1) < NA_DIM
    kc, vc = k_ref[:CTX_LEN, :], v_ref[:CTX_LEN, :]
    blocks = range(nb)
    lhs, ku, vu, cls = [], [], [], []
    for b in blocks:
        qr0 = (pl.program_id(1) * nb + b) * NA_BLOCK_ROWS
        u0 = jnp.clip(qr0 - NA_WIN_ROWS // 2, 0, rows - NA_UNION_ROWS)
        cls.append(jnp.where(qr0 == 0, 0, jnp.where(qr0 == rows - NA_BLOCK_ROWS, 2, 1)))
        start = pl.multiple_of(CTX_LEN + u0 * GRID_W, GRID_W)
        ku.append(k_ref[pl.ds(start, span), :])
        vu.append(v_ref[pl.ds(start, span), :])
        q = q_refs[b][...]
        zero = jnp.zeros_like(q)
        lhs.append(jnp.concatenate([jnp.where(first, q, zero), jnp.where(first, zero, q)], axis=0))
    bias = [jnp.concatenate([bias_ref[0, cls[b]], bias_ref[1, cls[b]]], axis=0) for b in blocks]
    sw = [_dot_nt(lhs[b], ku[b]) + bias[b] for b in blocks]
    sc = [_dot_nt(lhs[b], kc) for b in blocks]
    m = [jnp.maximum(jnp.max(sw[b], axis=-1, keepdims=True), jnp.max(sc[b], axis=-1, keepdims=True)) for b in blocks]
    pw = [jnp.exp(sw[b] - m[b]) for b in blocks]
    pc = [jnp.exp(sc[b] - m[b]) for b in blocks]
    l = [jnp.sum(pw[b], axis=-1, keepdims=True) + jnp.sum(pc[b], axis=-1, keepdims=True) for b in blocks]
    o = [(_dot(pw[b].astype(BF16), vu[b]) + _dot(pc[b].astype(BF16), vc)) / l[b] for b in blocks]
    for b in blocks:
        out = jnp.where(first, o[b][:bq], o[b][bq:])
        o_ref[b * bq:(b + 1) * bq, :] = (g_refs[b][...].astype(F32) * out).astype(BF16)


def _na_bias_table(rpb):
    half = NA_WIN_ROWS // 2
    cols = np.arange(GRID_W)
    c0 = np.clip(cols - NA_WIN_COLS // 2, 0, GRID_W - NA_WIN_COLS)
    valid_c = (cols[None, :] >= c0[:, None]) & (cols[None, :] < c0[:, None] + NA_WIN_COLS)
    dc = np.clip(cols[None, :] - cols[:, None] + NA_WIN_COLS - 1, 0, 2 * NA_WIN_COLS - 2)
    a = np.arange(NA_BLOCK_ROWS)[:, None]
    i = np.arange(NA_UNION_ROWS)[None, :]
    dr = np.stack([i - a, i - a - half, i - a - (NA_UNION_ROWS - NA_BLOCK_ROWS)])
    w0 = np.stack([0 * a, a, (NA_UNION_ROWS - NA_WIN_ROWS) + 0 * a])
    valid_r = (i[None] >= w0) & (i[None] < w0 + NA_WIN_ROWS)
    onehot = (np.arange(2 * NA_WIN_COLS - 1)[:, None, None] == dc[None]) & valid_c[None]
    tiles = jnp.einsum('hrd,dqk->hrqk', rpb.astype(F32), jnp.asarray(onehot, F32), precision=lax.Precision.HIGHEST)
    tiles = jnp.where(jnp.asarray(valid_c)[None, None], tiles, MASK_VALUE)
    masked_tile = jnp.full((NA_HEADS, GRID_W, GRID_W), MASK_VALUE, F32)

    def tile(c, qa, ki):
        return tiles[:, dr[c, qa, ki] + NA_WIN_ROWS - 1] if valid_r[c, qa, ki] else masked_tile

    return jnp.stack([jnp.concatenate([jnp.concatenate([tile(c, qa, ki) for ki in range(NA_UNION_ROWS)], axis=-1)
                                       for qa in range(NA_BLOCK_ROWS)], axis=-2) for c in range(3)], axis=1)


def _neighbourhood_attention(q, k, v, gc, rpb):
    tt = q.shape[0]
    t = tt - CTX_LEN
    rows = t // GRID_W
    nb = NA_BLOCKS_PER_STEP
    assert NA_BLOCK_ROWS == NA_WIN_ROWS // 2 and rows >= NA_UNION_ROWS and rows % (nb * NA_BLOCK_ROWS) == 0
    bias = _na_bias_table(rpb)
    pw = 2 * NA_DIM
    bq = NA_BLOCK_ROWS * GRID_W
    qoff = CTX_LEN // bq
    assert CTX_LEN % bq == 0
    blk = lambda b: pl.BlockSpec((bq, pw), lambda p, i: (qoff + i * nb + b, p))
    resident = pl.BlockSpec((tt, pw), lambda p, i: (0, p))
    return pl.pallas_call(
        functools.partial(_na_kernel, rows=rows),
        grid=(NA_HEADS // 2, rows // (nb * NA_BLOCK_ROWS)),
        in_specs=[blk(b) for b in range(nb)] + [blk(b) for b in range(nb)] + [
            resident, resident,
            pl.BlockSpec((2,) + bias.shape[1:], lambda p, i: (p, 0, 0, 0))],
        out_specs=pl.BlockSpec((nb * bq, pw), lambda p, i: (i, p)),
        out_shape=jax.ShapeDtypeStruct((t, BRANCH_W), BF16),
        compiler_params=_cparams(("arbitrary", "arbitrary"), 48 * 1024 * 1024),
        name="neighbourhood_attention",
    )(*([q] * nb + [gc] * nb + [k, v, bias]))


def _rwkv_chunk_index(d, t, nch):
    nctx = CTX_LEN // RWKV_CHUNK
    return t if d == 0 else jnp.where(t < nctx, nctx - 1 - t, nch - 1 - (t - nctx))


def _rwkv_prepare(d, c, nch, main_ref, prev_ref, next_ref, mu_ref, w0_ref, w2_ref, a0_ref, a2_ref, kk_ref, ka_ref,
                  rk_ref, e_ref, et_ref, bon_ref):
    lc = RWKV_CHUNK
    nctx = CTX_LEN // lc
    sgn = 1 - 2 * d
    p = main_ref[...]
    row = lax.broadcasted_iota(jnp.int32, p.shape, 0)
    first_zero = jnp.logical_or(c == 0, c == nctx)
    last_zero = jnp.logical_or(c == nctx - 1, c == nch - 1)
    pr = jnp.where(first_zero, 0.0, prev_ref[V7X_SUBLANES - 1:V7X_SUBLANES, :])
    nx = jnp.where(last_zero, 0.0, next_ref[0:1, :])
    prev = jnp.where(row == 0, pr, pltpu.roll(p, 1, 0))
    nxt = jnp.where(row == lc - 1, nx, pltpu.roll(p, lc - 1, 0))
    z = p + (0.5 * (prev + nxt) - p) * mu_ref[...]

    bw = BRANCH_W
    r, k, v = z[:, :bw], z[:, bw:2 * bw], z[:, 2 * bw:3 * bw]
    zw = z[:, 3 * bw:3 * bw + 2 * DECAY_LORA]
    za = z[:, 3 * bw + 2 * DECAY_LORA:]
    lw = w0_ref[d] + _dot(jnp.tanh(zw).astype(BF16), w2_ref[d])
    ld = -float(np.exp(-0.5)) * _sigmoid(lw)
    asig = _sigmoid(a0_ref[d] + _dot(za.astype(BF16), a2_ref[d]))
    kk = k * kk_ref[...]
    kd = k * (1.0 + (asig - 1.0) * ka_ref[...])
    kk_sq, rkd = _head_sums([kk * kk, r * kd * rk_ref[...]], e_ref[...], et_ref[...])
    kkn = kk / jnp.maximum(jnp.sqrt(kk_sq), 1e-12)
    a_vec = -kkn
    b_vec = kkn * asig
    bon_ref[...] = rkd * v

    ti = lax.broadcasted_iota(jnp.int32, (lc, lc), 0)
    si = lax.broadcasted_iota(jnp.int32, (lc, lc), 1)
    cum = _dot_split_rhs(jnp.where(sgn * (ti - si) >= 0, 1.0, 0.0).astype(BF16), ld)
    tot = jnp.sum(ld, axis=0, keepdims=True)
    rem = jnp.exp(tot - cum)
    pinv = jnp.exp(-cum)
    rt = r * jnp.exp(cum)
    kt = kd * pinv
    bt = b_vec * pinv
    at = a_vec * jnp.exp(cum - ld)
    bh = b_vec * rem
    kh = kd * rem
    pend = jnp.exp(tot)
    return dict(at=at, rt=rt, bt=bt, kt=kt, bh=bh, kh=kh, v=v, pend=pend)


def _rwkv_kernel(main_f, prev_f, next_f, main_b, prev_b, next_b, mu_ref, w0_ref, w2_ref, a0_ref, a2_ref, kk_ref,
                 ka_ref, rk_ref, e_ref, et_ref, wkv_f, wkv_b, bon_f, bon_b, s_scr, *, nch):
    lc = RWKV_CHUNK
    hd = RWKV_DIM
    t = pl.program_id(0)

    @pl.when(t == 0)
    def _():
        s_scr[...] = jnp.zeros_like(s_scr)

    shared = (mu_ref, w0_ref, w2_ref, a0_ref, a2_ref, kk_ref, ka_ref, rk_ref, e_ref, et_ref)
    rows = [_rwkv_prepare(0, _rwkv_chunk_index(0, t, nch), nch, main_f, prev_f, next_f, *shared, bon_f),
            _rwkv_prepare(1, _rwkv_chunk_index(1, t, nch), nch, main_b, prev_b, next_b, *shared, bon_b)]
    wkv_refs = (wkv_f, wkv_b)

    pw = 2 * hd
    lane = lax.broadcasted_iota(jnp.int32, (lc, pw), 1)
    tok = lax.broadcasted_iota(jnp.int32, (lc, pw), 0)
    first = lane < hd
    fwd_diff = tok - (lane & (hd - 1))
    eye = (fwd_diff == 0).astype(F32)
    rr = lax.broadcasted_iota(jnp.int32, (pw, pw), 0)
    cc = lax.broadcasted_iota(jnp.int32, (pw, pw), 1)
    same_head = (rr < hd) == (cc < hd)
    probs = [(d, p) for d in range(2) for p in range(RWKV_HEADS // 2)]
    n = range(len(probs))
    incl = [(fwd_diff >= 0) if d == 0 else (fwd_diff <= 0) for d, _ in probs]
    strict = [(fwd_diff > 0) if d == 0 else (fwd_diff < 0) for d, _ in probs]

    def bdiag(m):
        m = m.astype(BF16)
        zero = jnp.zeros_like(m)
        return jnp.concatenate([jnp.where(first, m, zero), jnp.where(first, zero, m)], axis=0)

    def part(name):
        return [rows[d][name][:, p * pw:(p + 1) * pw] for d, p in probs]

    at_p, rt_p, v_p = part("at"), part("rt"), part("v")
    bt_p, kt_p, pend_p = part("bt"), part("kt"), part("pend")
    bh_p = [m.astype(BF16) for m in part("bh")]
    kh_p = [m.astype(BF16) for m in part("kh")]
    lhs = [jnp.concatenate([at_p[i], rt_p[i]], axis=0).astype(BF16) for i in n]
    gram = [_dot_nt(lhs[i], jnp.concatenate([bdiag(bt_p[i]), bdiag(kt_p[i])], axis=0)) for i in n]
    gb = [g[:, :pw] for g in gram]
    gk = [g[:, pw:] for g in gram]
    aab = [jnp.where(strict[i], gb[i][:lc], 0.0) for i in n]
    arb = [jnp.where(incl[i], gb[i][lc:], 0.0) for i in n]
    aak = [jnp.where(strict[i], gk[i][:lc], 0.0) for i in n]
    ark = [jnp.where(incl[i], gk[i][lc:], 0.0) for i in n]
    vbd = [bdiag(m) for m in v_p]
    aakv = [_dot(aak[i].astype(BF16), vbd[i]) for i in n]
    s_tok = lane & (hd - 1)

    def same_block(size):
        shift = int(np.log2(size))
        return (tok >> shift) == (s_tok >> shift)

    nd = [jnp.where(same_block(RWKV_INV_BASE), m, 0.0) for m in aab]
    x = [eye + m for m in nd]
    nk = [_dot(m.astype(BF16), bdiag(m)) for m in nd]
    for _ in range(int(np.log2(RWKV_INV_BASE)) - 2):
        out = [_dot(jnp.concatenate([nk[i], x[i]], axis=0).astype(BF16), bdiag(nk[i])) for i in n]
        nk = [o[:lc] for o in out]
        x = [x[i] + out[i][lc:] for i in n]
    x = [x[i] + _dot(x[i].astype(BF16), bdiag(nk[i])) for i in n]
    size = RWKV_INV_BASE
    while size < lc:
        couple = jnp.logical_and(same_block(2 * size), jnp.logical_not(same_block(size)))
        xn = [_dot(x[i].astype(BF16), bdiag(jnp.where(couple, aab[i], 0.0))) for i in n]
        x = [x[i] + _dot(xn[i].astype(BF16), bdiag(x[i])) for i in n]
        size *= 2
    xc = [_dot(x[i].astype(BF16), jnp.concatenate([bdiag(at_p[i]), bdiag(aakv[i])], axis=1)) for i in n]
    ahat = [m[:, :pw] for m in xc]
    wmat = [m[:, pw:] for m in xc]
    rhat = [rt_p[i] + _dot(arb[i].astype(BF16), bdiag(ahat[i])) for i in n]
    y0 = [_dot(jnp.concatenate([arb[i], ark[i]], axis=1).astype(BF16),
               jnp.concatenate([bdiag(wmat[i]), vbd[i]], axis=0)) for i in n]
    mab = [jnp.where(same_head, _dot(ahat[i].T.astype(BF16), bh_p[i]), 0.0).astype(BF16) for i in n]
    gtf = [_dot(jnp.concatenate([wmat[i].T, v_p[i].T], axis=1).astype(BF16),
                jnp.concatenate([bh_p[i], kh_p[i]], axis=0)) for i in n]
    for i, (d, p) in enumerate(probs):
        s0 = s_scr[d, p]
        wkv_refs[d][:, p * pw:(p + 1) * pw] = _dot_nt(rhat[i].astype(BF16), bdiag(s0)) + y0[i]
        s_scr[d, p] = (s0 * pend_p[i] + _dot(s0.astype(BF16), mab[i])
                       + jnp.where(first, gtf[i][:lc], gtf[i][lc:]))


def _head_indicator(width, head_dim):
    idx = np.arange(width) // head_dim
    return jnp.asarray(idx[:, None] == np.arange(V7X_LANES)[None, :], BF16)


def _rwkv(rw_p, mu, w0s, w2s, a0s, a2s, k_k, k_a, r_k):
    tt, w = rw_p.shape
    lc = RWKV_CHUNK
    nch = tt // lc
    bw = BRANCH_W
    sub = V7X_SUBLANES
    zeros = jnp.zeros((DECAY_LORA, bw), F32)
    w2p = jnp.stack([jnp.concatenate([w2s[0], zeros]), jnp.concatenate([zeros, w2s[1]])]).astype(BF16)
    a2p = jnp.stack([jnp.concatenate([a2s[0], zeros]), jnp.concatenate([zeros, a2s[1]])]).astype(BF16)
    w0 = jnp.stack(w0s).reshape(2, 1, bw)
    a0 = jnp.stack(a0s).reshape(2, 1, bw)
    e_mat = _head_indicator(bw, RWKV_DIM)
    const = lambda *shape: pl.BlockSpec(shape, lambda t: tuple(0 for _ in shape))

    def chunk_specs(d):
        cidx = lambda t: _rwkv_chunk_index(d, t, nch)
        return [pl.BlockSpec((lc, w), lambda t: (cidx(t), 0)),
                pl.BlockSpec((sub, w), lambda t: (jnp.maximum(cidx(t) * (lc // sub) - 1, 0), 0)),
                pl.BlockSpec((sub, w), lambda t: (jnp.minimum((cidx(t) + 1) * (lc // sub), tt // sub - 1), 0))]

    out_spec = lambda d: pl.BlockSpec((lc, bw), lambda t: (_rwkv_chunk_index(d, t, nch), 0))
    return pl.pallas_call(
        functools.partial(_rwkv_kernel, nch=nch),
        grid=(nch,),
        in_specs=chunk_specs(0) + chunk_specs(1) + [
            const(1, w), const(2, 1, bw), const(2, 2 * DECAY_LORA, bw), const(2, 1, bw), const(2, 2 * ICLR_LORA, bw),
            const(1, bw), const(1, bw), const(1, bw), const(bw, V7X_LANES), const(V7X_LANES, bw)],
        out_specs=[out_spec(0), out_spec(1), out_spec(0), out_spec(1)],
        out_shape=[jax.ShapeDtypeStruct((tt, bw), F32)] * 4,
        scratch_shapes=[pltpu.VMEM((2, RWKV_HEADS // 2, RWKV_DIM, 2 * RWKV_DIM), F32)],
        compiler_params=_cparams(("arbitrary",), 48 * 1024 * 1024),
        name="rwkv7",
    )(rw_p, rw_p, rw_p, rw_p, rw_p, rw_p, mu.reshape(1, w), w0, w2p, a0, a2p,
      k_k.reshape(1, bw), k_a.reshape(1, bw), r_k.reshape(1, bw), e_mat, e_mat.T)


def _odd_out_kernel(na_ref, wkvf_ref, wkvb_ref, bonf_ref, bonb_ref, gd_ref, lng_ref, lnb_ref, e_ref, et_ref, w_ref,
                    res_ref, mod_ref, gpost_ref, o_ref):
    d = o_ref.shape[-1]
    e, et = e_ref[...], et_ref[...]
    inv = 1.0 / RWKV_DIM
    nrow = o_ref.shape[0] // ODD_OUT_ROW_GROUPS
    groups = [slice(g * nrow, (g + 1) * nrow) for g in range(ODD_OUT_ROW_GROUPS)]
    wkv = [wkvf_ref[g, :] + wkvb_ref[g, :] for g in groups]
    mean = [_head_sums([x], e, et)[0] * inv for x in wkv]
    xc = [x - mu for x, mu in zip(wkv, mean)]
    var = [_head_sums([x * x], e, et)[0] * inv for x in xc]
    y = [xc[i] * lax.rsqrt(var[i] + RWKV_GN_EPS) * lng_ref[...] + lnb_ref[...] + bonf_ref[g, :] + bonb_ref[g, :]
         for i, g in enumerate(groups)]
    m2 = jnp.concatenate([(gd_ref[g, :].astype(F32) * y[i]).astype(BF16) for i, g in enumerate(groups)], axis=0)
    out = _dot(na_ref[...], w_ref[:BRANCH_W, :]) + _dot(m2, w_ref[BRANCH_W:, :])
    gate = mod_ref[0][0:1, 2 * d:]
    o_ref[...] = res_ref[...] + gate * (_rms_rows(out) * gpost_ref[...])


def _odd_out(na_g, wkv_f, wkv_b, bon_f, bon_b, gd, lnx_g, lnx_b, w_out, stream, mod, g_post):
    t, bw = na_g.shape
    d = stream.shape[1]
    tm = ROW_BLOCK
    off = CTX_LEN // tm
    lat = lambda w: pl.BlockSpec((tm, w), lambda i: (off + i, 0))
    vec = lambda w: pl.BlockSpec((1, w), lambda i: (0, 0))
    e_mat = _head_indicator(bw, RWKV_DIM)
    return pl.pallas_call(
        _odd_out_kernel,
        grid=(t // tm,),
        in_specs=[pl.BlockSpec((tm, bw), lambda i: (i, 0)), lat(bw), lat(bw), lat(bw), lat(bw), lat(bw),
                  vec(bw), vec(bw),
                  pl.BlockSpec((bw, V7X_LANES), lambda i: (0, 0)),
                  pl.BlockSpec((V7X_LANES, bw), lambda i: (0, 0)),
                  pl.BlockSpec((2 * bw, d), lambda i: (0, 0)),
                  lat(d),
                  pl.BlockSpec((1, V7X_SUBLANES, 3 * d), lambda i: (1, 0, 0)),
                  vec(d)],
        out_specs=pl.BlockSpec((tm, d), lambda i: (i, 0)),
        out_shape=jax.ShapeDtypeStruct((t, d), F32),
        compiler_params=_cparams(("arbitrary",), 48 * 1024 * 1024),
        name="odd_out_proj",
    )(na_g, wkv_f, wkv_b, bon_f, bon_b, gd, lnx_g.reshape(1, bw), lnx_b.reshape(1, bw), e_mat, e_mat.T, w_out,
      stream, mod, g_post.reshape(1, d))


def _rope_tables(t):
    n_rows = t // GRID_W
    n_freq = RET_DK // 4
    inv = ROPE_THETA ** (-jnp.arange(n_freq, dtype=F32) / n_freq)
    row_ang = jnp.arange(n_rows, dtype=F32)[:, None] * inv
    col_ang = jnp.arange(GRID_W, dtype=F32)[:, None] * inv

    def table(fn):
        rows_part = jnp.broadcast_to(fn(row_ang)[:, None, :], (n_rows, GRID_W, n_freq))
        cols_part = jnp.broadcast_to(fn(col_ang)[None, :, :], (n_rows, GRID_W, n_freq))
        return jnp.repeat(jnp.concatenate([rows_part, cols_part], axis=-1).reshape(t, 2 * n_freq), 2, axis=-1)

    cos = table(jnp.cos)
    sin = table(jnp.sin) * jnp.tile(jnp.asarray([-1.0, 1.0], F32), RET_DK // 2)
    cos = jnp.concatenate([jnp.ones((CTX_LEN, RET_DK), F32), cos], axis=0)
    sin = jnp.concatenate([jnp.zeros((CTX_LEN, RET_DK), F32), sin], axis=0)
    return cos, sin


def kernel(x, c, ctx, c_ctx, w_mod, b_mod, g_pre, g_post, ev_w_in, ev_w_out, ret_decay_fwd, ret_decay_bwd, gqa_q_norm, gqa_k_norm, od_w_in, od_w_out, na_rpb, rwkv_shift_mu, rwkv_w0_fwd, rwkv_w2_fwd, rwkv_w0_bwd, rwkv_w2_bwd, rwkv_a0_fwd, rwkv_a2_fwd, rwkv_a0_bwd, rwkv_a2_bwd, rwkv_k_k, rwkv_k_a, rwkv_r_k, rwkv_lnx_g, rwkv_lnx_b):
    assert x.shape[0] == 1 and DEPTH == 2 and RET_DK == GQA_DIM
    t = x.shape[1]
    tt = t + CTX_LEN
    assert t % ROW_BLOCK == 0 and tt % ATT_KV_BLOCK == 0 and t % GRID_W == 0
    x2, ctx2 = x[0], ctx[0]
    mod = _modulation(c, c_ctx, w_mod, b_mod)
    cos_t, sin_t = _rope_tables(t)

    qa, ka, va, qb, kbt, vb, ga, gb = _even_in_proj(x2, ctx2, mod, g_pre[0], ev_w_in[0].astype(BF16), cos_t, sin_t,
                                                    gqa_q_norm[0], gqa_k_norm[0])
    ret_g = _retention(qa, ka, va, ga, ret_decay_fwd[0], ret_decay_bwd[0])
    att_lat = _gqa_attention(qb, kbt, vb, gb, q_row0=CTX_LEN, n_q=t, n_keys=tt, kv_block=ATT_KV_BLOCK, out_rows=t)
    att_ctx = _gqa_attention(qb, kbt, vb, gb, q_row0=0, n_q=CTX_LEN, n_keys=CTX_LEN, kv_block=CTX_LEN,
                             out_rows=CTX_LEN)
    stream1 = _out_proj(ret_g, att_ctx, att_lat, ev_w_out[0].astype(BF16), ctx2, x2, mod, 0, g_post[0])

    q, k, v, rw_p, gc, gd = _odd_in_proj(stream1, mod, g_pre[1], od_w_in[0].astype(BF16))
    na_g = _neighbourhood_attention(q, k, v, gc, na_rpb[0])
    rw = _rwkv(rw_p, rwkv_shift_mu[0], (rwkv_w0_fwd[0], rwkv_w0_bwd[0]), (rwkv_w2_fwd[0], rwkv_w2_bwd[0]),
               (rwkv_a0_fwd[0], rwkv_a0_bwd[0]), (rwkv_a2_fwd[0], rwkv_a2_bwd[0]),
               rwkv_k_k[0], rwkv_k_a[0], rwkv_r_k[0].reshape(-1))
    out = _odd_out(na_g, *rw, gd, rwkv_lnx_g[0], rwkv_lnx_b[0], od_w_out[0].astype(BF16), stream1, mod, g_post[1])
    return out[None]
```

```python
import functools

import jax
import jax.numpy as jnp
import numpy as np
from jax import lax
from jax.experimental import pallas as pl
from jax.experimental.pallas import tpu as pltpu

F32 = jnp.float32
BF16 = jnp.bfloat16

D_MODEL = 1024
DEPTH = 2
GRID_W = 64
CTX_LEN = 256
BRANCH_W = D_MODEL
RET_HEADS = 4
RET_DK = 128
RET_DV = BRANCH_W // RET_HEADS
RET_CHUNK = 128
GQA_HEADS = 8
GQA_KV_HEADS = 2
GQA_DIM = BRANCH_W // GQA_HEADS
GQA_GROUP = GQA_HEADS // GQA_KV_HEADS
NA_HEADS = 16
NA_DIM = BRANCH_W // NA_HEADS
NA_WIN_ROWS = 8
NA_WIN_COLS = 16
RWKV_HEADS = 16
RWKV_DIM = BRANCH_W // RWKV_HEADS
DECAY_LORA = 64
ICLR_LORA = 64
ROPE_THETA = 10000.0
NORM_EPS = 1e-6
RWKV_GN_EPS = 64e-5
SHIFT_W = 3 * BRANCH_W + 2 * DECAY_LORA + 2 * ICLR_LORA
EVEN_IN = 2 * RET_HEADS * RET_DK + BRANCH_W + GQA_HEADS * GQA_DIM + 2 * GQA_KV_HEADS * GQA_DIM + 2 * BRANCH_W
ODD_IN = 3 * BRANCH_W + SHIFT_W + 2 * BRANCH_W

V7X_LANES = 128
V7X_SUBLANES = 8
V7X_VMEM_BYTES = 64 * 1024 * 1024

ROW_BLOCK = CTX_LEN
RWKV_CHUNK = 64
RWKV_CHUNKS_PER_STEP = 2
RWKV_INV_BASE = 8
RET_CHUNKS_PER_STEP = 2
ATT_Q_BLOCK = 256
ATT_KV_BLOCK = 1280
ATT_LOOP_UNROLL = 6
LOG2_E = 1.4426950408889634
NA_BLOCK_ROWS = 4
NA_UNION_ROWS = 12
NA_BLOCKS_PER_STEP = 2
ODD_OUT_ROW_GROUPS = 2
MASK_VALUE = -1e30


def _vmem_limit(nbytes):
    return int(min(V7X_VMEM_BYTES - 4 * 1024 * 1024, max(32 * 1024 * 1024, nbytes)))


def _cparams(sem, vmem_bytes):
    return pltpu.CompilerParams(dimension_semantics=sem, vmem_limit_bytes=_vmem_limit(vmem_bytes))


def _silu(x):
    return x / (1.0 + jnp.exp(-x))


def _sigmoid(x):
    return 1.0 / (1.0 + jnp.exp(-x))


def _dot(a, b):
    return jnp.dot(a, b, preferred_element_type=F32)


def _dot_nt(a, b):
    return lax.dot_general(a, b, (((1,), (1,)), ((), ())), preferred_element_type=F32)


def _split2(x):
    hi = x.astype(BF16)
    lo = (x - hi.astype(F32)).astype(BF16)
    return hi, lo


def _dot_split_rhs(a_bf16, x):
    hi, lo = _split2(x)
    return _dot(a_bf16, hi) + _dot(a_bf16, lo)


def _head_sums(xs, e, et):
    n = xs[0].shape[0]

    def stacked_dot(vals, w):
        parts = [t for x in vals for t in _split2(x)]
        y = _dot(jnp.concatenate(parts, axis=0), w)
        return [y[2 * i * n:(2 * i + 1) * n] + y[(2 * i + 1) * n:(2 * i + 2) * n] for i in range(len(vals))]

    return stacked_dot(stacked_dot(xs, e), et)


def _rms_rows(x):
    return x * lax.rsqrt(jnp.mean(x * x, axis=-1, keepdims=True) + NORM_EPS)


def _rope(t, cos, sin_signed, even):
    nxt = pltpu.roll(t, t.shape[1] - 1, 1)
    prv = pltpu.roll(t, 1, 1)
    return t * cos + jnp.where(even, nxt, prv) * sin_signed


def _mod_kernel(cc_ref, w_ref, b_ref, o_ref):
    s = _silu(cc_ref[...])
    o_ref[0] = jnp.dot(s, w_ref[0], preferred_element_type=F32, precision=lax.Precision.HIGHEST) + b_ref[0]


def _modulation(c, c_ctx, w_mod, b_mod):
    d = c.shape[-1]
    cc = jnp.concatenate([c[:1], c_ctx[None, :], jnp.zeros((V7X_SUBLANES - 2, d), F32)], axis=0)
    return pl.pallas_call(
        _mod_kernel,
        grid=(DEPTH, 3),
        in_specs=[pl.BlockSpec((V7X_SUBLANES, d), lambda l, j: (0, 0)),
                  pl.BlockSpec((1, d, d), lambda l, j: (l, 0, j)),
                  pl.BlockSpec((1, 1, d), lambda l, j: (l, 0, j))],
        out_specs=pl.BlockSpec((1, V7X_SUBLANES, d), lambda l, j: (l, 0, j)),
        out_shape=jax.ShapeDtypeStruct((DEPTH, V7X_SUBLANES, 3 * d), F32),
        compiler_params=_cparams(("arbitrary", "arbitrary"), 40 * 1024 * 1024),
        name="modulation",
    )(cc, w_mod, b_mod.reshape(DEPTH, 1, 3 * d))


def _adaln(xb, mod, is_ctx, g_pre):
    d = xb.shape[-1]
    m = jnp.where(is_ctx, mod[1:2, :], mod[0:1, :])
    shift, scale = m[:, :d], m[:, d:2 * d]
    return (_rms_rows(xb) * g_pre) * (1.0 + scale) + shift


def _even_in_kernel(x_ref, ctx_ref, mod_ref, gpre_ref, w_ref, cos_ref, sin_ref, qn_ref, kn_ref,
                    qa_ref, ka_ref, va_ref, qb_ref, kbt_ref, vb_ref, ga_ref, gb_ref):
    is_ctx = pl.program_id(0) == 0
    xb = jnp.where(is_ctx, ctx_ref[...], x_ref[...])
    hb = _adaln(xb, mod_ref[0], is_ctx, gpre_ref[...]).astype(BF16)
    cos, sin_s = cos_ref[...], sin_ref[...]
    even = (lax.broadcasted_iota(jnp.int32, cos.shape, 1) & 1) == 0
    o = 0

    def seg(width):
        nonlocal o
        y = _dot(hb, w_ref[:, o:o + width])
        o += width
        return y

    y = seg(RET_HEADS * RET_DK)
    for h in range(RET_HEADS):
        sl = slice(h * RET_DK, (h + 1) * RET_DK)
        qa_ref[:, sl] = _rope(y[:, sl], cos, sin_s, even).astype(BF16)
    y = seg(RET_HEADS * RET_DK)
    for h in range(RET_HEADS):
        sl = slice(h * RET_DK, (h + 1) * RET_DK)
        ka_ref[:, sl] = _rope(y[:, sl] * RET_DK ** -0.5, cos, sin_s, even).astype(BF16)
    va_ref[...] = seg(BRANCH_W).astype(BF16)
    y = seg(GQA_HEADS * GQA_DIM)
    for h in range(GQA_HEADS):
        sl = slice(h * GQA_DIM, (h + 1) * GQA_DIM)
        t = _rms_rows(y[:, sl]) * qn_ref[...]
        qb_ref[:, sl] = (_rope(t, cos, sin_s, even) * (GQA_DIM ** -0.5 * LOG2_E)).astype(BF16)
    y = seg(GQA_KV_HEADS * GQA_DIM)
    for h in range(GQA_KV_HEADS):
        sl = slice(h * GQA_DIM, (h + 1) * GQA_DIM)
        t = _rope(_rms_rows(y[:, sl]) * kn_ref[...], cos, sin_s, even)
        kbt_ref[sl, :] = t.T.astype(BF16)
    y = seg(GQA_KV_HEADS * GQA_DIM)
    for h in range(GQA_KV_HEADS):
        vb_ref[:, 2 * h * GQA_DIM:(2 * h + 1) * GQA_DIM] = y[:, h * GQA_DIM:(h + 1) * GQA_DIM].astype(BF16)
        vb_ref[:, (2 * h + 1) * GQA_DIM:(2 * h + 2) * GQA_DIM] = jnp.ones((y.shape[0], GQA_DIM), BF16)
    ga_ref[...] = _silu(seg(BRANCH_W)).astype(BF16)
    gb_ref[...] = _silu(seg(BRANCH_W)).astype(BF16)


def _even_in_proj(x, ctx, mod, g_pre, w_in, cos_t, sin_t, q_norm, k_norm):
    t, d = x.shape
    tt = t + CTX_LEN
    nblk = tt // ROW_BLOCK
    tm = ROW_BLOCK
    kvw = GQA_KV_HEADS * GQA_DIM
    row = lambda w: pl.BlockSpec((tm, w), lambda i: (i, 0))
    const = lambda shape: pl.BlockSpec(shape, lambda i: tuple(0 for _ in shape))
    outs = [((tt, RET_HEADS * RET_DK), row(RET_HEADS * RET_DK)),
            ((tt, RET_HEADS * RET_DK), row(RET_HEADS * RET_DK)),
            ((tt, BRANCH_W), row(BRANCH_W)),
            ((tt, GQA_HEADS * GQA_DIM), row(GQA_HEADS * GQA_DIM)),
            ((kvw, tt), pl.BlockSpec((kvw, tm), lambda i: (0, i))),
            ((tt, 2 * kvw), row(2 * kvw)),
            ((tt, BRANCH_W), row(BRANCH_W)),
            ((tt, BRANCH_W), row(BRANCH_W))]
    return pl.pallas_call(
        _even_in_kernel,
        grid=(nblk,),
        in_specs=[pl.BlockSpec((tm, d), lambda i: (jnp.maximum(i - 1, 0), 0)),
                  const((CTX_LEN, d)),
                  pl.BlockSpec((1, V7X_SUBLANES, 3 * d), lambda i: (0, 0, 0)),
                  const((1, d)),
                  const((d, EVEN_IN)),
                  row(RET_DK), row(RET_DK),
                  const((1, GQA_DIM)), const((1, GQA_DIM))],
        out_specs=[s for _, s in outs],
        out_shape=[jax.ShapeDtypeStruct(shp, BF16) for shp, _ in outs],
        compiler_params=_cparams(("arbitrary",), 2 * d * EVEN_IN * 2 + 16 * 1024 * 1024),
        name="even_in_proj",
    )(x, ctx, mod, g_pre.reshape(1, d), w_in, cos_t, sin_t, q_norm.reshape(1, -1), k_norm.reshape(1, -1))


def _log_sigmoid(x):
    return jnp.minimum(x, 0.0) - jnp.log(1.0 + jnp.exp(-jnp.abs(x)))


def _ret_bwd_block(t, nblocks):
    nctx = CTX_LEN // (RET_CHUNKS_PER_STEP * RET_CHUNK)
    return jnp.where(t < nctx, nctx - 1 - t, nblocks - 1 - (t - nctx))


def _ret_state_kernel(dec_ref, k_ref, v_ref, sb_ref, s_scr):
    c = RET_CHUNK

    @pl.when(pl.program_id(0) == 0)
    def _():
        s_scr[...] = jnp.zeros_like(s_scr)

    lg = _log_sigmoid(dec_ref[...])
    pos = lax.broadcasted_iota(jnp.int32, (c, RET_DK), 0).astype(F32)
    hs = range(RET_HEADS)
    lgb = [lg[RET_HEADS + h:RET_HEADS + h + 1, :] for h in hs]
    state = [s_scr[h] for h in hs]
    for sub in reversed(range(RET_CHUNKS_PER_STEP)):
        rows = slice(sub * c, (sub + 1) * c)
        for h in hs:
            sb_ref[sub, h] = state[h].astype(BF16)
        kzt = [(k_ref[rows, h * RET_DK:(h + 1) * RET_DK].astype(F32) * jnp.exp(pos * lgb[h])).T.astype(BF16)
               for h in hs]
        state = [jnp.exp(c * lgb[h][:, :1]) * state[h] + _dot(kzt[h], v_ref[rows, h * RET_DV:(h + 1) * RET_DV])
                 for h in hs]
    for h in hs:
        s_scr[h] = state[h]


def _ret_out_kernel(dec_ref, q_ref, k_ref, v_ref, g_ref, sb_ref, o_ref, s_scr):
    c = RET_CHUNK

    @pl.when(pl.program_id(0) == 0)
    def _():
        s_scr[...] = jnp.zeros_like(s_scr)

    lg = _log_sigmoid(dec_ref[...])
    ii = lax.broadcasted_iota(jnp.int32, (c, c), 0)
    jj = lax.broadcasted_iota(jnp.int32, (c, c), 1)
    dlt = (ii - jj).astype(F32)
    pos = lax.broadcasted_iota(jnp.int32, (c, RET_DK), 0).astype(F32)
    hs = range(RET_HEADS)
    lgf = [lg[h:h + 1, :] for h in hs]
    lgb = [lg[RET_HEADS + h:RET_HEADS + h + 1, :] for h in hs]
    dec = [jnp.where(dlt > 0, jnp.exp(jnp.maximum(dlt, 0.0) * lgf[h]),
                     jnp.where(dlt < 0, jnp.exp(jnp.maximum(-dlt, 0.0) * lgb[h]), 2.0)) for h in hs]
    state = [s_scr[h] for h in hs]
    for sub in range(RET_CHUNKS_PER_STEP):
        rows = slice(sub * c, (sub + 1) * c)
        q = [q_ref[rows, h * RET_DK:(h + 1) * RET_DK] for h in hs]
        k = [k_ref[rows, h * RET_DK:(h + 1) * RET_DK] for h in hs]
        v = [v_ref[rows, h * RET_DV:(h + 1) * RET_DV] for h in hs]
        p = [(_dot_nt(q[h], k[h]) * dec[h]).astype(BF16) for h in hs]
        qf = [q[h].astype(F32) for h in hs]
        ret = [_dot(p[h], v[h])
               + _dot((qf[h] * jnp.exp((pos + 1.0) * lgf[h])).astype(BF16), state[h].astype(BF16))
               + _dot((qf[h] * jnp.exp((c - pos) * lgb[h])).astype(BF16), sb_ref[sub, h]) for h in hs]
        kzt = [(k[h].astype(F32) * jnp.exp((c - 1.0 - pos) * lgf[h])).T.astype(BF16) for h in hs]
        state = [jnp.exp(c * lgf[h][:, :1]) * state[h] + _dot(kzt[h], v[h]) for h in hs]
        xc = [ret[h] - jnp.mean(ret[h], axis=-1, keepdims=True) for h in hs]
        y = [xc[h] * lax.rsqrt(jnp.mean(xc[h] * xc[h], axis=-1, keepdims=True) + NORM_EPS) for h in hs]
        for h in hs:
            sl = slice(h * RET_DV, (h + 1) * RET_DV)
            o_ref[rows, sl] = (g_ref[rows, sl].astype(F32) * y[h]).astype(BF16)
    for h in hs:
        s_scr[h] = state[h]


def _retention(qa, ka, va, ga, dec_f, dec_b):
    tt = qa.shape[0]
    r = RET_CHUNKS_PER_STEP
    rows = r * RET_CHUNK
    assert tt % rows == 0 and CTX_LEN % rows == 0
    n = tt // rows
    dec = jnp.broadcast_to(jnp.concatenate([dec_f, dec_b]).astype(F32)[:, None], (2 * RET_HEADS, V7X_LANES))
    kw, vw = RET_HEADS * RET_DK, BRANCH_W
    dec_spec = pl.BlockSpec((2 * RET_HEADS, V7X_LANES), lambda t: (0, 0))
    state_shape = (RET_HEADS, RET_DK, RET_DV)
    sb = pl.pallas_call(
        _ret_state_kernel,
        grid=(n,),
        in_specs=[dec_spec,
                  pl.BlockSpec((rows, kw), lambda t: (_ret_bwd_block(t, n), 0)),
                  pl.BlockSpec((rows, vw), lambda t: (_ret_bwd_block(t, n), 0))],
        out_specs=pl.BlockSpec((r,) + state_shape, lambda t: (_ret_bwd_block(t, n), 0, 0, 0)),
        out_shape=jax.ShapeDtypeStruct((n * r,) + state_shape, BF16),
        scratch_shapes=[pltpu.VMEM(state_shape, F32)],
        compiler_params=_cparams(("arbitrary",), 32 * 1024 * 1024),
        name="retention_state",
    )(dec, ka, va)
    return pl.pallas_call(
        _ret_out_kernel,
        grid=(n,),
        in_specs=[dec_spec,
                  pl.BlockSpec((rows, kw), lambda t: (t, 0)),
                  pl.BlockSpec((rows, kw), lambda t: (t, 0)),
                  pl.BlockSpec((rows, vw), lambda t: (t, 0)),
                  pl.BlockSpec((rows, vw), lambda t: (t, 0)),
                  pl.BlockSpec((r,) + state_shape, lambda t: (t, 0, 0, 0))],
        out_specs=pl.BlockSpec((rows, vw), lambda t: (t, 0)),
        out_shape=jax.ShapeDtypeStruct((tt, vw), BF16),
        scratch_shapes=[pltpu.VMEM(state_shape, F32)],
        compiler_params=_cparams(("arbitrary",), 32 * 1024 * 1024),
        name="retention_out",
    )(dec, qa, ka, va, ga, sb)


def _gqa_kernel(q_ref, kt_ref, v_ref, g_ref, o_ref, s0_scr, s1_scr, *, kv_block, n_kv):
    tq = q_ref.shape[0]
    heads = [slice(h * GQA_DIM, (h + 1) * GQA_DIM) for h in range(GQA_GROUP)]
    q = jnp.concatenate([q_ref[:, sl] for sl in heads], axis=0)
    rows = GQA_GROUP * tq

    def scores(j, s_ref):
        start = pl.multiple_of(j * kv_block, V7X_LANES)
        s_ref[...] = _dot(q, kt_ref[:, pl.ds(start, kv_block)])

    def softmax_pv(j, s_ref, carry):
        m, acc = carry
        start = pl.multiple_of(j * kv_block, V7X_LANES)
        s = s_ref[...]
        m_new = jnp.maximum(m, jnp.max(s, axis=-1, keepdims=True))
        p = jnp.exp2(s - m_new).astype(BF16)
        acc = jnp.exp2(m - m_new) * acc + _dot(p, v_ref[pl.ds(start, kv_block), :])
        return m_new, acc

    def body(i, carry):
        j = 2 * i
        scores(j + 1, s1_scr)
        carry = softmax_pv(j, s0_scr, carry)
        scores(j + 2, s0_scr)
        return softmax_pv(j + 1, s1_scr, carry)

    carry = (jnp.full((rows, 1), MASK_VALUE, F32), jnp.zeros((rows, 2 * GQA_DIM), F32))
    scores(0, s0_scr)
    carry = lax.fori_loop(0, (n_kv - 1) // 2, body, carry, unroll=ATT_LOOP_UNROLL)
    if n_kv % 2 == 0:
        scores(n_kv - 1, s1_scr)
        carry = softmax_pv(n_kv - 2, s0_scr, carry)
        _, acc = softmax_pv(n_kv - 1, s1_scr, carry)
    else:
        _, acc = softmax_pv(n_kv - 1, s0_scr, carry)
    out = acc[:, :GQA_DIM] / acc[:, GQA_DIM:]
    for h, sl in enumerate(heads):
        o_ref[:, sl] = (g_ref[:, sl].astype(F32) * out[h * tq:(h + 1) * tq]).astype(BF16)


def _gqa_attention(qb, kbt, vb, gb, *, q_row0, n_q, n_keys, kv_block, out_rows):
    gw = GQA_GROUP * GQA_DIM
    tq = ATT_Q_BLOCK
    qoff = q_row0 // tq
    return pl.pallas_call(
        functools.partial(_gqa_kernel, kv_block=kv_block, n_kv=n_keys // kv_block),
        grid=(GQA_KV_HEADS, n_q // tq),
        in_specs=[pl.BlockSpec((tq, gw), lambda g, i: (qoff + i, g)),
                  pl.BlockSpec((GQA_DIM, n_keys), lambda g, i: (g, 0)),
                  pl.BlockSpec((n_keys, 2 * GQA_DIM), lambda g, i: (0, g)),
                  pl.BlockSpec((tq, gw), lambda g, i: (qoff + i, g))],
        out_specs=pl.BlockSpec((tq, gw), lambda g, i: (i, g)),
        out_shape=jax.ShapeDtypeStruct((out_rows, GQA_HEADS * GQA_DIM), BF16),
        scratch_shapes=[pltpu.VMEM((GQA_GROUP * tq, kv_block), F32)] * 2,
        compiler_params=_cparams(("arbitrary", "arbitrary"), 56 * 1024 * 1024),
        name="gqa_attention",
    )(qb, kbt, vb, gb)


def _out_proj_kernel(m1_ref, m2c_ref, m2l_ref, w_ref, resc_ref, resl_ref, mod_ref, gpost_ref, o_ref):
    d = o_ref.shape[-1]
    is_ctx = pl.program_id(0) == 0
    m2 = jnp.where(is_ctx, m2c_ref[...], m2l_ref[...])
    y = _dot(m1_ref[...], w_ref[:BRANCH_W, :]) + _dot(m2, w_ref[BRANCH_W:, :])
    mod = mod_ref[0]
    gate = jnp.where(is_ctx, mod[1:2, 2 * d:], mod[0:1, 2 * d:])
    res = jnp.where(is_ctx, resc_ref[...], resl_ref[...])
    o_ref[...] = res + gate * (_rms_rows(y) * gpost_ref[...])


def _out_proj(m1, m2_ctx, m2_lat, w_out, res_ctx, res_lat, mod, layer, g_post):
    tt = m1.shape[0]
    d = res_lat.shape[1]
    tm = ROW_BLOCK
    row = lambda w: pl.BlockSpec((tm, w), lambda i: (i, 0))
    lat = lambda w: pl.BlockSpec((tm, w), lambda i: (jnp.maximum(i - 1, 0), 0))
    ctx = lambda w: pl.BlockSpec((CTX_LEN, w), lambda i: (0, 0))
    return pl.pallas_call(
        _out_proj_kernel,
        grid=(tt // tm,),
        in_specs=[row(BRANCH_W), ctx(BRANCH_W), lat(BRANCH_W),
                  pl.BlockSpec((2 * BRANCH_W, d), lambda i: (0, 0)),
                  ctx(d), lat(d),
                  pl.BlockSpec((1, V7X_SUBLANES, 3 * d), lambda i: (layer, 0, 0)),
                  pl.BlockSpec((1, d), lambda i: (0, 0))],
        out_specs=row(d),
        out_shape=jax.ShapeDtypeStruct((tt, d), F32),
        compiler_params=_cparams(("arbitrary",), 40 * 1024 * 1024),
        name="out_proj",
    )(m1, m2_ctx, m2_lat, w_out, res_ctx, res_lat, mod, g_post.reshape(1, d))


def _odd_in_kernel(s_ref, mod_ref, gpre_ref, w_ref, q_ref, k_ref, v_ref, rw_ref, gc_ref, gd_ref):
    is_ctx = pl.program_id(0) == 0
    hb = _adaln(s_ref[...], mod_ref[0], is_ctx, gpre_ref[...]).astype(BF16)
    o = 0

    def seg(width):
        nonlocal o
        y = _dot(hb, w_ref[:, o:o + width])
        o += width
        return y

    q_ref[...] = (seg(BRANCH_W) * NA_DIM ** -0.5).astype(BF16)
    k_ref[...] = seg(BRANCH_W).astype(BF16)
    v_ref[...] = seg(BRANCH_W).astype(BF16)
    rw_ref[...] = seg(SHIFT_W)
    gc_ref[...] = _silu(seg(BRANCH_W)).astype(BF16)
    gd_ref[...] = _silu(seg(BRANCH_W)).astype(BF16)


def _odd_in_proj(stream, mod, g_pre, w_in):
    tt, d = stream.shape
    tm = ROW_BLOCK
    row = lambda w: pl.BlockSpec((tm, w), lambda i: (i, 0))
    widths = [(BRANCH_W, BF16), (BRANCH_W, BF16), (BRANCH_W, BF16), (SHIFT_W, F32), (BRANCH_W, BF16), (BRANCH_W, BF16)]
    return pl.pallas_call(
        _odd_in_kernel,
        grid=(tt // tm,),
        in_specs=[row(d),
                  pl.BlockSpec((1, V7X_SUBLANES, 3 * d), lambda i: (1, 0, 0)),
                  pl.BlockSpec((1, d), lambda i: (0, 0)),
                  pl.BlockSpec((d, ODD_IN), lambda i: (0, 0), pipeline_mode=pl.Buffered(1))],
        out_specs=[row(w) for w, _ in widths],
        out_shape=[jax.ShapeDtypeStruct((tt, w), dt) for w, dt in widths],
        compiler_params=_cparams(("arbitrary",), d * ODD_IN * 2 + 24 * 1024 * 1024),
        name="odd_in_proj",
    )(stream, mod, g_pre.reshape(1, d), w_in)


def _na_kernel(*refs, rows):
    nb = NA_BLOCKS_PER_STEP
    q_refs, g_refs = refs[:nb], refs[nb:2 * nb]
    k_ref, v_ref, bias_ref, o_ref = refs[2 * nb:]
    span = NA_UNION_ROWS * GRID_W
    bq = NA_BLOCK_ROWS * GRID_W
    first = lax.broadcasted_iota(jnp.int32, (bq, 2 * NA_DIM), 1) < NA_DIM
    kc, vc = k_ref[:CTX_LEN, :], v_ref[:CTX_LEN, :]
    blocks = range(nb)
    lhs, ku, vu, cls = [], [], [], []
    for b in blocks:
        qr0 = (pl.program_id(1) * nb + b) * NA_BLOCK_ROWS
        u0 = jnp.clip(qr0 - NA_WIN_ROWS // 2, 0, rows - NA_UNION_ROWS)
        cls.append(jnp.where(qr0 == 0, 0, jnp.where(qr0 == rows - NA_BLOCK_ROWS, 2, 1)))
        start = pl.multiple_of(CTX_LEN + u0 * GRID_W, GRID_W)
        ku.append(k_ref[pl.ds(start, span), :])
        vu.append(v_ref[pl.ds(start, span), :])
        q = q_refs[b][...]
        zero = jnp.zeros_like(q)
        lhs.append(jnp.concatenate([jnp.where(first, q, zero), jnp.where(first, zero, q)], axis=0))
    bias = [jnp.concatenate([bias_ref[0, cls[b]], bias_ref[1, cls[b]]], axis=0) for b in blocks]
    sw = [_dot_nt(lhs[b], ku[b]) + bias[b] for b in blocks]
    sc = [_dot_nt(lhs[b], kc) for b in blocks]
    m = [jnp.maximum(jnp.max(sw[b], axis=-1, keepdims=True), jnp.max(sc[b], axis=-1, keepdims=True)) for b in blocks]
    pw = [jnp.exp(sw[b] - m[b]) for b in blocks]
    pc = [jnp.exp(sc[b] - m[b]) for b in blocks]
    l = [jnp.sum(pw[b], axis=-1, keepdims=True) + jnp.sum(pc[b], axis=-1, keepdims=True) for b in blocks]
    o = [(_dot(pw[b].astype(BF16), vu[b]) + _dot(pc[b].astype(BF16), vc)) / l[b] for b in blocks]
    for b in blocks:
        out = jnp.where(first, o[b][:bq], o[b][bq:])
        o_ref[b * bq:(b + 1) * bq, :] = (g_refs[b][...].astype(F32) * out).astype(BF16)


def _na_bias_table(rpb):
    half = NA_WIN_ROWS // 2
    cols = np.arange(GRID_W)
    c0 = np.clip(cols - NA_WIN_COLS // 2, 0, GRID_W - NA_WIN_COLS)
    valid_c = (cols[None, :] >= c0[:, None]) & (cols[None, :] < c0[:, None] + NA_WIN_COLS)
    dc = np.clip(cols[None, :] - cols[:, None] + NA_WIN_COLS - 1, 0, 2 * NA_WIN_COLS - 2)
    a = np.arange(NA_BLOCK_ROWS)[:, None]
    i = np.arange(NA_UNION_ROWS)[None, :]
    dr = np.stack([i - a, i - a - half, i - a - (NA_UNION_ROWS - NA_BLOCK_ROWS)])
    w0 = np.stack([0 * a, a, (NA_UNION_ROWS - NA_WIN_ROWS) + 0 * a])
    valid_r = (i[None] >= w0) & (i[None] < w0 + NA_WIN_ROWS)
    onehot = (np.arange(2 * NA_WIN_COLS - 1)[:, None, None] == dc[None]) & valid_c[None]
    tiles = jnp.einsum('hrd,dqk->hrqk', rpb.astype(F32), jnp.asarray(onehot, F32), precision=lax.Precision.HIGHEST)
    tiles = jnp.where(jnp.asarray(valid_c)[None, None], tiles, MASK_VALUE)
    masked_tile = jnp.full((NA_HEADS, GRID_W, GRID_W), MASK_VALUE, F32)

    def tile(c, qa, ki):
        return tiles[:, dr[c, qa, ki] + NA_WIN_ROWS - 1] if valid_r[c, qa, ki] else masked_tile

    return jnp.stack([jnp.concatenate([jnp.concatenate([tile(c, qa, ki) for ki in range(NA_UNION_ROWS)], axis=-1)
                                       for qa in range(NA_BLOCK_ROWS)], axis=-2) for c in range(3)], axis=1)


def _neighbourhood_attention(q, k, v, gc, rpb):
    tt = q.shape[0]
    t = tt - CTX_LEN
    rows = t // GRID_W
    nb = NA_BLOCKS_PER_STEP
    assert NA_BLOCK_ROWS == NA_WIN_ROWS // 2 and rows >= NA_UNION_ROWS and rows % (nb * NA_BLOCK_ROWS) == 0
    bias = _na_bias_table(rpb)
    pw = 2 * NA_DIM
    bq = NA_BLOCK_ROWS * GRID_W
    qoff = CTX_LEN // bq
    assert CTX_LEN % bq == 0
    blk = lambda b: pl.BlockSpec((bq, pw), lambda p, i: (qoff + i * nb + b, p))
    resident = pl.BlockSpec((tt, pw), lambda p, i: (0, p))
    return pl.pallas_call(
        functools.partial(_na_kernel, rows=rows),
        grid=(NA_HEADS // 2, rows // (nb * NA_BLOCK_ROWS)),
        in_specs=[blk(b) for b in range(nb)] + [blk(b) for b in range(nb)] + [
            resident, resident,
            pl.BlockSpec((2,) + bias.shape[1:], lambda p, i: (p, 0, 0, 0))],
        out_specs=pl.BlockSpec((nb * bq, pw), lambda p, i: (i, p)),
        out_shape=jax.ShapeDtypeStruct((t, BRANCH_W), BF16),
        compiler_params=_cparams(("arbitrary", "arbitrary"), 48 * 1024 * 1024),
        name="neighbourhood_attention",
    )(*([q] * nb + [gc] * nb + [k, v, bias]))


def _rwkv_block_index(d, t, nblk):
    nctx = CTX_LEN // (RWKV_CHUNKS_PER_STEP * RWKV_CHUNK)
    return t if d == 0 else jnp.where(t < nctx, nctx - 1 - t, nblk - 1 - (t - nctx))


def _rwkv_prepare(d, c, nch, p, prev_row, next_row, mu_ref, w0_ref, w2_ref, a0_ref, a2_ref, kk_ref, ka_ref,
                  rk_ref, e_ref, et_ref, bon_ref, out_rows):
    lc = RWKV_CHUNK
    nctx = CTX_LEN // lc
    sgn = 1 - 2 * d
    row = lax.broadcasted_iota(jnp.int32, p.shape, 0)
    first_zero = jnp.logical_or(c == 0, c == nctx)
    last_zero = jnp.logical_or(c == nctx - 1, c == nch - 1)
    pr = jnp.where(first_zero, 0.0, prev_row)
    nx = jnp.where(last_zero, 0.0, next_row)
    prev = jnp.where(row == 0, pr, pltpu.roll(p, 1, 0))
    nxt = jnp.where(row == lc - 1, nx, pltpu.roll(p, lc - 1, 0))
    z = p + (0.5 * (prev + nxt) - p) * mu_ref[...]

    bw = BRANCH_W
    r, k, v = z[:, :bw], z[:, bw:2 * bw], z[:, 2 * bw:3 * bw]
    zw = z[:, 3 * bw:3 * bw + 2 * DECAY_LORA]
    za = z[:, 3 * bw + 2 * DECAY_LORA:]
    lw = w0_ref[d] + _dot(jnp.tanh(zw).astype(BF16), w2_ref[d])
    ld = -float(np.exp(-0.5)) * _sigmoid(lw)
    asig = _sigmoid(a0_ref[d] + _dot(za.astype(BF16), a2_ref[d]))
    kk = k * kk_ref[...]
    kd = k * (1.0 + (asig - 1.0) * ka_ref[...])
    kk_sq, rkd = _head_sums([kk * kk, r * kd * rk_ref[...]], e_ref[...], et_ref[...])
    kkn = kk / jnp.maximum(jnp.sqrt(kk_sq), 1e-12)
    a_vec = -kkn
    b_vec = kkn * asig
    bon_ref[out_rows, :] = rkd * v

    ti = lax.broadcasted_iota(jnp.int32, (lc, lc), 0)
    si = lax.broadcasted_iota(jnp.int32, (lc, lc), 1)
    cum = _dot_split_rhs(jnp.where(sgn * (ti - si) >= 0, 1.0, 0.0).astype(BF16), ld)
    tot = jnp.sum(ld, axis=0, keepdims=True)
    rem = jnp.exp(tot - cum)
    pinv = jnp.exp(-cum)
    rt = r * jnp.exp(cum)
    kt = kd * pinv
    bt = b_vec * pinv
    at = a_vec * jnp.exp(cum - ld)
    bh = b_vec * rem
    kh = kd * rem
    pend = jnp.exp(tot)
    return dict(at=at, rt=rt, bt=bt, kt=kt, bh=bh, kh=kh, v=v, pend=pend)


def _rwkv_kernel(main_f, prev_f, next_f, main_b, prev_b, next_b, mu_ref, w0_ref, w2_ref, a0_ref, a2_ref, kk_ref,
                 ka_ref, rk_ref, e_ref, et_ref, wkv_f, wkv_b, bon_f, bon_b, s_scr, *, nch):
    lc = RWKV_CHUNK
    hd = RWKV_DIM
    t = pl.program_id(0)

    @pl.when(t == 0)
    def _():
        s_scr[...] = jnp.zeros_like(s_scr)

    shared = (mu_ref, w0_ref, w2_ref, a0_ref, a2_ref, kk_ref, ka_ref, rk_ref, e_ref, et_ref)
    nsub = RWKV_CHUNKS_PER_STEP
    blocks = ((main_f, prev_f, next_f, bon_f), (main_b, prev_b, next_b, bon_b))
    wkv_refs = (wkv_f, wkv_b)

    def sub_rows(k, d):
        sub = k if d == 0 else nsub - 1 - k
        return sub, slice(sub * lc, (sub + 1) * lc)

    rows = {}
    for k in range(nsub):
        for d, (main, prev, nxt, bon) in enumerate(blocks):
            sub, rs = sub_rows(k, d)
            c = _rwkv_block_index(d, t, nch // nsub) * nsub + sub
            prev_row = main[sub * lc - 1:sub * lc, :] if sub > 0 else prev[V7X_SUBLANES - 1:V7X_SUBLANES, :]
            next_row = main[(sub + 1) * lc:(sub + 1) * lc + 1, :] if sub < nsub - 1 else nxt[0:1, :]
            rows[k, d] = _rwkv_prepare(d, c, nch, main[rs, :], prev_row, next_row, *shared, bon, rs)

    pw = 2 * hd
    lane = lax.broadcasted_iota(jnp.int32, (lc, pw), 1)
    tok = lax.broadcasted_iota(jnp.int32, (lc, pw), 0)
    first = lane < hd
    fwd_diff = tok - (lane & (hd - 1))
    eye = (fwd_diff == 0).astype(F32)
    rr = lax.broadcasted_iota(jnp.int32, (pw, pw), 0)
    cc = lax.broadcasted_iota(jnp.int32, (pw, pw), 1)
    same_head = (rr < hd) == (cc < hd)
    probs = [(k, d, p) for k in range(nsub) for d in range(2) for p in range(RWKV_HEADS // 2)]
    n = range(len(probs))
    incl = [(fwd_diff >= 0) if d == 0 else (fwd_diff <= 0) for _, d, _ in probs]
    strict = [(fwd_diff > 0) if d == 0 else (fwd_diff < 0) for _, d, _ in probs]

    def bdiag(m):
        m = m.astype(BF16)
        zero = jnp.zeros_like(m)
        return jnp.concatenate([jnp.where(first, m, zero), jnp.where(first, zero, m)], axis=0)

    def part(name):
        return [rows[k, d][name][:, p * pw:(p + 1) * pw] for k, d, p in probs]

    at_p, rt_p, v_p = part("at"), part("rt"), part("v")
    bt_p, kt_p, pend_p = part("bt"), part("kt"), part("pend")
    bh_p = [m.astype(BF16) for m in part("bh")]
    kh_p = [m.astype(BF16) for m in part("kh")]
    lhs = [jnp.concatenate([at_p[i], rt_p[i]], axis=0).astype(BF16) for i in n]
    gram = [_dot_nt(lhs[i], jnp.concatenate([bdiag(bt_p[i]), bdiag(kt_p[i])], axis=0)) for i in n]
    gb = [g[:, :pw] for g in gram]
    gk = [g[:, pw:] for g in gram]
    aab = [jnp.where(strict[i], gb[i][:lc], 0.0) for i in n]
    arb = [jnp.where(incl[i], gb[i][lc:], 0.0) for i in n]
    aak = [jnp.where(strict[i], gk[i][:lc], 0.0) for i in n]
    ark = [jnp.where(incl[i], gk[i][lc:], 0.0) for i in n]
    vbd = [bdiag(m) for m in v_p]
    aakv = [_dot(aak[i].astype(BF16), vbd[i]) for i in n]
    s_tok = lane & (hd - 1)

    def same_block(size):
        shift = int(np.log2(size))
        return (tok >> shift) == (s_tok >> shift)

    nd = [jnp.where(same_block(RWKV_INV_BASE), m, 0.0) for m in aab]
    x = [eye + m for m in nd]
    nk = [_dot(m.astype(BF16), bdiag(m)) for m in nd]
    for _ in range(int(np.log2(RWKV_INV_BASE)) - 2):
        out = [_dot(jnp.concatenate([nk[i], x[i]], axis=0).astype(BF16), bdiag(nk[i])) for i in n]
        nk = [o[:lc] for o in out]
        x = [x[i] + out[i][lc:] for i in n]
    x = [x[i] + _dot(x[i].astype(BF16), bdiag(nk[i])) for i in n]
    size = RWKV_INV_BASE
    while size < lc:
        couple = jnp.logical_and(same_block(2 * size), jnp.logical_not(same_block(size)))
        xn = [_dot(x[i].astype(BF16), bdiag(jnp.where(couple, aab[i], 0.0))) for i in n]
        x = [x[i] + _dot(xn[i].astype(BF16), bdiag(x[i])) for i in n]
        size *= 2
    xc = [_dot(x[i].astype(BF16), jnp.concatenate([bdiag(at_p[i]), bdiag(aakv[i])], axis=1)) for i in n]
    ahat = [m[:, :pw] for m in xc]
    wmat = [m[:, pw:] for m in xc]
    rhat = [rt_p[i] + _dot(arb[i].astype(BF16), bdiag(ahat[i])) for i in n]
    y0 = [_dot(jnp.concatenate([arb[i], ark[i]], axis=1).astype(BF16),
               jnp.concatenate([bdiag(wmat[i]), vbd[i]], axis=0)) for i in n]
    mab = [jnp.where(same_head, _dot(ahat[i].T.astype(BF16), bh_p[i]), 0.0).astype(BF16) for i in n]
    gtf = [_dot(jnp.concatenate([wmat[i].T, v_p[i].T], axis=1).astype(BF16),
                jnp.concatenate([bh_p[i], kh_p[i]], axis=0)) for i in n]
    for i, (k, d, p) in enumerate(probs):
        s0 = s_scr[d, p]
        wkv_refs[d][sub_rows(k, d)[1], p * pw:(p + 1) * pw] = _dot_nt(rhat[i].astype(BF16), bdiag(s0)) + y0[i]
        s_scr[d, p] = (s0 * pend_p[i] + _dot(s0.astype(BF16), mab[i])
                       + jnp.where(first, gtf[i][:lc], gtf[i][lc:]))


def _head_indicator(width, head_dim):
    idx = np.arange(width) // head_dim
    return jnp.asarray(idx[:, None] == np.arange(V7X_LANES)[None, :], BF16)


def _rwkv(rw_p, mu, w0s, w2s, a0s, a2s, k_k, k_a, r_k):
    tt, w = rw_p.shape
    lc = RWKV_CHUNK
    nch = tt // lc
    bw = BRANCH_W
    sub = V7X_SUBLANES
    zeros = jnp.zeros((DECAY_LORA, bw), F32)
    w2p = jnp.stack([jnp.concatenate([w2s[0], zeros]), jnp.concatenate([zeros, w2s[1]])]).astype(BF16)
    a2p = jnp.stack([jnp.concatenate([a2s[0], zeros]), jnp.concatenate([zeros, a2s[1]])]).astype(BF16)
    w0 = jnp.stack(w0s).reshape(2, 1, bw)
    a0 = jnp.stack(a0s).reshape(2, 1, bw)
    e_mat = _head_indicator(bw, RWKV_DIM)
    const = lambda *shape: pl.BlockSpec(shape, lambda t: tuple(0 for _ in shape))

    rows = RWKV_CHUNKS_PER_STEP * lc
    nblk = tt // rows
    assert tt % rows == 0 and CTX_LEN % rows == 0

    def block_specs(d):
        bidx = lambda t: _rwkv_block_index(d, t, nblk)
        return [pl.BlockSpec((rows, w), lambda t: (bidx(t), 0)),
                pl.BlockSpec((sub, w), lambda t: (jnp.maximum(bidx(t) * (rows // sub) - 1, 0), 0)),
                pl.BlockSpec((sub, w), lambda t: (jnp.minimum((bidx(t) + 1) * (rows // sub), tt // sub - 1), 0))]

    out_spec = lambda d: pl.BlockSpec((rows, bw), lambda t: (_rwkv_block_index(d, t, nblk), 0))
    return pl.pallas_call(
        functools.partial(_rwkv_kernel, nch=nch),
        grid=(nblk,),
        in_specs=block_specs(0) + block_specs(1) + [
            const(1, w), const(2, 1, bw), const(2, 2 * DECAY_LORA, bw), const(2, 1, bw), const(2, 2 * ICLR_LORA, bw),
            const(1, bw), const(1, bw), const(1, bw), const(bw, V7X_LANES), const(V7X_LANES, bw)],
        out_specs=[out_spec(0), out_spec(1), out_spec(0), out_spec(1)],
        out_shape=[jax.ShapeDtypeStruct((tt, bw), F32)] * 4,
        scratch_shapes=[pltpu.VMEM((2, RWKV_HEADS // 2, RWKV_DIM, 2 * RWKV_DIM), F32)],
        compiler_params=_cparams(("arbitrary",), 48 * 1024 * 1024),
        name="rwkv7",
    )(rw_p, rw_p, rw_p, rw_p, rw_p, rw_p, mu.reshape(1, w), w0, w2p, a0, a2p,
      k_k.reshape(1, bw), k_a.reshape(1, bw), r_k.reshape(1, bw), e_mat, e_mat.T)


def _odd_out_kernel(na_ref, wkvf_ref, wkvb_ref, bonf_ref, bonb_ref, gd_ref, lng_ref, lnb_ref, e_ref, et_ref, w_ref,
                    res_ref, mod_ref, gpost_ref, o_ref):
    d = o_ref.shape[-1]
    e, et = e_ref[...], et_ref[...]
    inv = 1.0 / RWKV_DIM
    nrow = o_ref.shape[0] // ODD_OUT_ROW_GROUPS
    groups = [slice(g * nrow, (g + 1) * nrow) for g in range(ODD_OUT_ROW_GROUPS)]
    wkv = [wkvf_ref[g, :] + wkvb_ref[g, :] for g in groups]
    mean = [_head_sums([x], e, et)[0] * inv for x in wkv]
    xc = [x - mu for x, mu in zip(wkv, mean)]
    var = [_head_sums([x * x], e, et)[0] * inv for x in xc]
    y = [xc[i] * lax.rsqrt(var[i] + RWKV_GN_EPS) * lng_ref[...] + lnb_ref[...] + bonf_ref[g, :] + bonb_ref[g, :]
         for i, g in enumerate(groups)]
    m2 = jnp.concatenate([(gd_ref[g, :].astype(F32) * y[i]).astype(BF16) for i, g in enumerate(groups)], axis=0)
    out = _dot(na_ref[...], w_ref[:BRANCH_W, :]) + _dot(m2, w_ref[BRANCH_W:, :])
    gate = mod_ref[0][0:1, 2 * d:]
    o_ref[...] = res_ref[...] + gate * (_rms_rows(out) * gpost_ref[...])


def _odd_out(na_g, wkv_f, wkv_b, bon_f, bon_b, gd, lnx_g, lnx_b, w_out, stream, mod, g_post):
    t, bw = na_g.shape
    d = stream.shape[1]
    tm = ROW_BLOCK
    off = CTX_LEN // tm
    lat = lambda w: pl.BlockSpec((tm, w), lambda i: (off + i, 0))
    vec = lambda w: pl.BlockSpec((1, w), lambda i: (0, 0))
    e_mat = _head_indicator(bw, RWKV_DIM)
    return pl.pallas_call(
        _odd_out_kernel,
        grid=(t // tm,),
        in_specs=[pl.BlockSpec((tm, bw), lambda i: (i, 0)), lat(bw), lat(bw), lat(bw), lat(bw), lat(bw),
                  vec(bw), vec(bw),
                  pl.BlockSpec((bw, V7X_LANES), lambda i: (0, 0)),
                  pl.BlockSpec((V7X_LANES, bw), lambda i: (0, 0)),
                  pl.BlockSpec((2 * bw, d), lambda i: (0, 0)),
                  lat(d),
                  pl.BlockSpec((1, V7X_SUBLANES, 3 * d), lambda i: (1, 0, 0)),
                  vec(d)],
        out_specs=pl.BlockSpec((tm, d), lambda i: (i, 0)),
        out_shape=jax.ShapeDtypeStruct((t, d), F32),
        compiler_params=_cparams(("arbitrary",), 48 * 1024 * 1024),
        name="odd_out_proj",
    )(na_g, wkv_f, wkv_b, bon_f, bon_b, gd, lnx_g.reshape(1, bw), lnx_b.reshape(1, bw), e_mat, e_mat.T, w_out,
      stream, mod, g_post.reshape(1, d))


def _rope_tables(t):
    n_rows = t // GRID_W
    n_freq = RET_DK // 4
    inv = ROPE_THETA ** (-jnp.arange(n_freq, dtype=F32) / n_freq)
    row_ang = jnp.arange(n_rows, dtype=F32)[:, None] * inv
    col_ang = jnp.arange(GRID_W, dtype=F32)[:, None] * inv

    def table(fn):
        rows_part = jnp.broadcast_to(fn(row_ang)[:, None, :], (n_rows, GRID_W, n_freq))
        cols_part = jnp.broadcast_to(fn(col_ang)[None, :, :], (n_rows, GRID_W, n_freq))
        return jnp.repeat(jnp.concatenate([rows_part, cols_part], axis=-1).reshape(t, 2 * n_freq), 2, axis=-1)

    cos = table(jnp.cos)
    sin = table(jnp.sin) * jnp.tile(jnp.asarray([-1.0, 1.0], F32), RET_DK // 2)
    cos = jnp.concatenate([jnp.ones((CTX_LEN, RET_DK), F32), cos], axis=0)
    sin = jnp.concatenate([jnp.zeros((CTX_LEN, RET_DK), F32), sin], axis=0)
    return cos, sin


def kernel(x, c, ctx, c_ctx, w_mod, b_mod, g_pre, g_post, ev_w_in, ev_w_out, ret_decay_fwd, ret_decay_bwd, gqa_q_norm, gqa_k_norm, od_w_in, od_w_out, na_rpb, rwkv_shift_mu, rwkv_w0_fwd, rwkv_w2_fwd, rwkv_w0_bwd, rwkv_w2_bwd, rwkv_a0_fwd, rwkv_a2_fwd, rwkv_a0_bwd, rwkv_a2_bwd, rwkv_k_k, rwkv_k_a, rwkv_r_k, rwkv_lnx_g, rwkv_lnx_b):
    assert x.shape[0] == 1 and DEPTH == 2 and RET_DK == GQA_DIM
    t = x.shape[1]
    tt = t + CTX_LEN
    assert t % ROW_BLOCK == 0 and tt % ATT_KV_BLOCK == 0 and t % GRID_W == 0
    x2, ctx2 = x[0], ctx[0]
    mod = _modulation(c, c_ctx, w_mod, b_mod)
    cos_t, sin_t = _rope_tables(t)

    qa, ka, va, qb, kbt, vb, ga, gb = _even_in_proj(x2, ctx2, mod, g_pre[0], ev_w_in[0].astype(BF16), cos_t, sin_t,
                                                    gqa_q_norm[0], gqa_k_norm[0])
    ret_g = _retention(qa, ka, va, ga, ret_decay_fwd[0], ret_decay_bwd[0])
    att_lat = _gqa_attention(qb, kbt, vb, gb, q_row0=CTX_LEN, n_q=t, n_keys=tt, kv_block=ATT_KV_BLOCK, out_rows=t)
    att_ctx = _gqa_attention(qb, kbt, vb, gb, q_row0=0, n_q=CTX_LEN, n_keys=CTX_LEN, kv_block=CTX_LEN,
                             out_rows=CTX_LEN)
    stream1 = _out_proj(ret_g, att_ctx, att_lat, ev_w_out[0].astype(BF16), ctx2, x2, mod, 0, g_post[0])

    q, k, v, rw_p, gc, gd = _odd_in_proj(stream1, mod, g_pre[1], od_w_in[0].astype(BF16))
    na_g = _neighbourhood_attention(q, k, v, gc, na_rpb[0])
    rw = _rwkv(rw_p, rwkv_shift_mu[0], (rwkv_w0_fwd[0], rwkv_w0_bwd[0]), (rwkv_w2_fwd[0], rwkv_w2_bwd[0]),
               (rwkv_a0_fwd[0], rwkv_a0_bwd[0]), (rwkv_a2_fwd[0], rwkv_a2_bwd[0]),
               rwkv_k_k[0], rwkv_k_a[0], rwkv_r_k[0].reshape(-1))
    out = _odd_out(na_g, *rw, gd, rwkv_lnx_g[0], rwkv_lnx_b[0], od_w_out[0].astype(BF16), stream1, mod, g_post[1])
    return out[None]
```

```python
import functools

import jax
import jax.numpy as jnp
import numpy as np
from jax import lax
from jax.experimental import pallas as pl
from jax.experimental.pallas import tpu as pltpu

F32 = jnp.float32
BF16 = jnp.bfloat16

D_MODEL = 1024
DEPTH = 2
GRID_W = 64
CTX_LEN = 256
BRANCH_W = D_MODEL
RET_HEADS = 4
RET_DK = 128
RET_DV = BRANCH_W // RET_HEADS
RET_CHUNK = 128
GQA_HEADS = 8
GQA_KV_HEADS = 2
GQA_DIM = BRANCH_W // GQA_HEADS
GQA_GROUP = GQA_HEADS // GQA_KV_HEADS
NA_HEADS = 16
NA_DIM = BRANCH_W // NA_HEADS
NA_WIN_ROWS = 8
NA_WIN_COLS = 16
RWKV_HEADS = 16
RWKV_DIM = BRANCH_W // RWKV_HEADS
DECAY_LORA = 64
ICLR_LORA = 64
ROPE_THETA = 10000.0
NORM_EPS = 1e-6
RWKV_GN_EPS = 64e-5
SHIFT_W = 3 * BRANCH_W + 2 * DECAY_LORA + 2 * ICLR_LORA
EVEN_IN = 2 * RET_HEADS * RET_DK + BRANCH_W + GQA_HEADS * GQA_DIM + 2 * GQA_KV_HEADS * GQA_DIM + 2 * BRANCH_W
ODD_IN = 3 * BRANCH_W + SHIFT_W + 2 * BRANCH_W

V7X_LANES = 128
V7X_SUBLANES = 8
V7X_VMEM_BYTES = 64 * 1024 * 1024

ROW_BLOCK = CTX_LEN
RWKV_CHUNK = 64
RWKV_CHUNKS_PER_STEP = 2
RWKV_INV_BASE = 8
RET_CHUNKS_PER_STEP = 2
ATT_Q_BLOCK = 256
ATT_KV_BLOCK = 1280
LOG2_E = 1.4426950408889634
NA_BLOCK_ROWS = 4
NA_UNION_ROWS = 12
NA_BLOCKS_PER_STEP = 2
ODD_OUT_ROW_GROUPS = 2
MASK_VALUE = -1e30


def _vmem_limit(nbytes):
    return int(min(V7X_VMEM_BYTES - 4 * 1024 * 1024, max(32 * 1024 * 1024, nbytes)))


def _cparams(sem, vmem_bytes):
    return pltpu.CompilerParams(dimension_semantics=sem, vmem_limit_bytes=_vmem_limit(vmem_bytes))


def _silu(x):
    return x / (1.0 + jnp.exp(-x))


def _sigmoid(x):
    return 1.0 / (1.0 + jnp.exp(-x))


def _dot(a, b):
    return jnp.dot(a, b, preferred_element_type=F32)


def _dot_nt(a, b):
    return lax.dot_general(a, b, (((1,), (1,)), ((), ())), preferred_element_type=F32)


def _split2(x):
    hi = x.astype(BF16)
    lo = (x - hi.astype(F32)).astype(BF16)
    return hi, lo


def _dot_split_rhs(a_bf16, x):
    hi, lo = _split2(x)
    return _dot(a_bf16, hi) + _dot(a_bf16, lo)


def _head_sums(xs, e, et):
    n = xs[0].shape[0]

    def stacked_dot(vals, w):
        parts = [t for x in vals for t in _split2(x)]
        y = _dot(jnp.concatenate(parts, axis=0), w)
        return [y[2 * i * n:(2 * i + 1) * n] + y[(2 * i + 1) * n:(2 * i + 2) * n] for i in range(len(vals))]

    return stacked_dot(stacked_dot(xs, e), et)


def _rms_rows(x):
    return x * lax.rsqrt(jnp.mean(x * x, axis=-1, keepdims=True) + NORM_EPS)


def _rope(t, cos, sin_signed, even):
    nxt = pltpu.roll(t, t.shape[1] - 1, 1)
    prv = pltpu.roll(t, 1, 1)
    return t * cos + jnp.where(even, nxt, prv) * sin_signed


def _mod_kernel(cc_ref, w_ref, b_ref, o_ref):
    s = _silu(cc_ref[...])
    o_ref[0] = jnp.dot(s, w_ref[0], preferred_element_type=F32, precision=lax.Precision.HIGHEST) + b_ref[0]


def _modulation(c, c_ctx, w_mod, b_mod):
    d = c.shape[-1]
    cc = jnp.concatenate([c[:1], c_ctx[None, :], jnp.zeros((V7X_SUBLANES - 2, d), F32)], axis=0)
    return pl.pallas_call(
        _mod_kernel,
        grid=(DEPTH, 3),
        in_specs=[pl.BlockSpec((V7X_SUBLANES, d), lambda l, j: (0, 0)),
                  pl.BlockSpec((1, d, d), lambda l, j: (l, 0, j)),
                  pl.BlockSpec((1, 1, d), lambda l, j: (l, 0, j))],
        out_specs=pl.BlockSpec((1, V7X_SUBLANES, d), lambda l, j: (l, 0, j)),
        out_shape=jax.ShapeDtypeStruct((DEPTH, V7X_SUBLANES, 3 * d), F32),
        compiler_params=_cparams(("arbitrary", "arbitrary"), 40 * 1024 * 1024),
        name="modulation",
    )(cc, w_mod, b_mod.reshape(DEPTH, 1, 3 * d))


def _adaln(xb, mod, is_ctx, g_pre):
    d = xb.shape[-1]
    m = jnp.where(is_ctx, mod[1:2, :], mod[0:1, :])
    shift, scale = m[:, :d], m[:, d:2 * d]
    return (_rms_rows(xb) * g_pre) * (1.0 + scale) + shift


def _even_in_kernel(x_ref, ctx_ref, mod_ref, gpre_ref, w_ref, cos_ref, sin_ref, qn_ref, kn_ref,
                    qa_ref, ka_ref, va_ref, qb_ref, kbt_ref, vb_ref, ga_ref, gb_ref):
    is_ctx = pl.program_id(0) == 0
    xb = jnp.where(is_ctx, ctx_ref[...], x_ref[...])
    hb = _adaln(xb, mod_ref[0], is_ctx, gpre_ref[...]).astype(BF16)
    cos, sin_s = cos_ref[...], sin_ref[...]
    even = (lax.broadcasted_iota(jnp.int32, cos.shape, 1) & 1) == 0
    o = 0

    def seg(width):
        nonlocal o
        y = _dot(hb, w_ref[:, o:o + width])
        o += width
        return y

    y = seg(RET_HEADS * RET_DK)
    for h in range(RET_HEADS):
        sl = slice(h * RET_DK, (h + 1) * RET_DK)
        qa_ref[:, sl] = _rope(y[:, sl], cos, sin_s, even).astype(BF16)
    y = seg(RET_HEADS * RET_DK)
    for h in range(RET_HEADS):
        sl = slice(h * RET_DK, (h + 1) * RET_DK)
        ka_ref[:, sl] = _rope(y[:, sl] * RET_DK ** -0.5, cos, sin_s, even).astype(BF16)
    va_ref[...] = seg(BRANCH_W).astype(BF16)
    y = seg(GQA_HEADS * GQA_DIM)
    for h in range(GQA_HEADS):
        sl = slice(h * GQA_DIM, (h + 1) * GQA_DIM)
        t = _rms_rows(y[:, sl]) * qn_ref[...]
        qb_ref[:, sl] = (_rope(t, cos, sin_s, even) * (GQA_DIM ** -0.5 * LOG2_E)).astype(BF16)
    y = seg(GQA_KV_HEADS * GQA_DIM)
    for h in range(GQA_KV_HEADS):
        sl = slice(h * GQA_DIM, (h + 1) * GQA_DIM)
        t = _rope(_rms_rows(y[:, sl]) * kn_ref[...], cos, sin_s, even)
        kbt_ref[sl, :] = t.T.astype(BF16)
    y = seg(GQA_KV_HEADS * GQA_DIM)
    for h in range(GQA_KV_HEADS):
        vb_ref[:, 2 * h * GQA_DIM:(2 * h + 1) * GQA_DIM] = y[:, h * GQA_DIM:(h + 1) * GQA_DIM].astype(BF16)
        vb_ref[:, (2 * h + 1) * GQA_DIM:(2 * h + 2) * GQA_DIM] = jnp.ones((y.shape[0], GQA_DIM), BF16)
    ga_ref[...] = _silu(seg(BRANCH_W)).astype(BF16)
    gb_ref[...] = _silu(seg(BRANCH_W)).astype(BF16)


def _even_in_proj(x, ctx, mod, g_pre, w_in, cos_t, sin_t, q_norm, k_norm):
    t, d = x.shape
    tt = t + CTX_LEN
    nblk = tt // ROW_BLOCK
    tm = ROW_BLOCK
    kvw = GQA_KV_HEADS * GQA_DIM
    row = lambda w: pl.BlockSpec((tm, w), lambda i: (i, 0))
    const = lambda shape: pl.BlockSpec(shape, lambda i: tuple(0 for _ in shape))
    outs = [((tt, RET_HEADS * RET_DK), row(RET_HEADS * RET_DK)),
            ((tt, RET_HEADS * RET_DK), row(RET_HEADS * RET_DK)),
            ((tt, BRANCH_W), row(BRANCH_W)),
            ((tt, GQA_HEADS * GQA_DIM), row(GQA_HEADS * GQA_DIM)),
            ((kvw, tt), pl.BlockSpec((kvw, tm), lambda i: (0, i))),
            ((tt, 2 * kvw), row(2 * kvw)),
            ((tt, BRANCH_W), row(BRANCH_W)),
            ((tt, BRANCH_W), row(BRANCH_W))]
    return pl.pallas_call(
        _even_in_kernel,
        grid=(nblk,),
        in_specs=[pl.BlockSpec((tm, d), lambda i: (jnp.maximum(i - 1, 0), 0)),
                  const((CTX_LEN, d)),
                  pl.BlockSpec((1, V7X_SUBLANES, 3 * d), lambda i: (0, 0, 0)),
                  const((1, d)),
                  const((d, EVEN_IN)),
                  row(RET_DK), row(RET_DK),
                  const((1, GQA_DIM)), const((1, GQA_DIM))],
        out_specs=[s for _, s in outs],
        out_shape=[jax.ShapeDtypeStruct(shp, BF16) for shp, _ in outs],
        compiler_params=_cparams(("arbitrary",), 2 * d * EVEN_IN * 2 + 16 * 1024 * 1024),
        name="even_in_proj",
    )(x, ctx, mod, g_pre.reshape(1, d), w_in, cos_t, sin_t, q_norm.reshape(1, -1), k_norm.reshape(1, -1))


def _log_sigmoid(x):
    return jnp.minimum(x, 0.0) - jnp.log(1.0 + jnp.exp(-jnp.abs(x)))


def _ret_bwd_block(t, nblocks):
    nctx = CTX_LEN // (RET_CHUNKS_PER_STEP * RET_CHUNK)
    return jnp.where(t < nctx, nctx - 1 - t, nblocks - 1 - (t - nctx))


def _ret_state_kernel(dec_ref, k_ref, v_ref, sb_ref, s_scr):
    c = RET_CHUNK

    @pl.when(pl.program_id(0) == 0)
    def _():
        s_scr[...] = jnp.zeros_like(s_scr)

    lg = _log_sigmoid(dec_ref[...])
    pos = lax.broadcasted_iota(jnp.int32, (c, RET_DK), 0).astype(F32)
    hs = range(RET_HEADS)
    lgb = [lg[RET_HEADS + h:RET_HEADS + h + 1, :] for h in hs]
    state = [s_scr[h] for h in hs]
    for sub in reversed(range(RET_CHUNKS_PER_STEP)):
        rows = slice(sub * c, (sub + 1) * c)
        for h in hs:
            sb_ref[sub, h] = state[h].astype(BF16)
        kzt = [(k_ref[rows, h * RET_DK:(h + 1) * RET_DK].astype(F32) * jnp.exp(pos * lgb[h])).T.astype(BF16)
               for h in hs]
        state = [jnp.exp(c * lgb[h][:, :1]) * state[h] + _dot(kzt[h], v_ref[rows, h * RET_DV:(h + 1) * RET_DV])
                 for h in hs]
    for h in hs:
        s_scr[h] = state[h]


def _ret_out_kernel(dec_ref, q_ref, k_ref, v_ref, g_ref, sb_ref, o_ref, s_scr):
    c = RET_CHUNK

    @pl.when(pl.program_id(0) == 0)
    def _():
        s_scr[...] = jnp.zeros_like(s_scr)

    lg = _log_sigmoid(dec_ref[...])
    ii = lax.broadcasted_iota(jnp.int32, (c, c), 0)
    jj = lax.broadcasted_iota(jnp.int32, (c, c), 1)
    dlt = (ii - jj).astype(F32)
    pos = lax.broadcasted_iota(jnp.int32, (c, RET_DK), 0).astype(F32)
    hs = range(RET_HEADS)
    lgf = [lg[h:h + 1, :] for h in hs]
    lgb = [lg[RET_HEADS + h:RET_HEADS + h + 1, :] for h in hs]
    dec = [jnp.where(dlt > 0, jnp.exp(jnp.maximum(dlt, 0.0) * lgf[h]),
                     jnp.where(dlt < 0, jnp.exp(jnp.maximum(-dlt, 0.0) * lgb[h]), 2.0)) for h in hs]
    state = [s_scr[h] for h in hs]
    for sub in range(RET_CHUNKS_PER_STEP):
        rows = slice(sub * c, (sub + 1) * c)
        q = [q_ref[rows, h * RET_DK:(h + 1) * RET_DK] for h in hs]
        k = [k_ref[rows, h * RET_DK:(h + 1) * RET_DK] for h in hs]
        v = [v_ref[rows, h * RET_DV:(h + 1) * RET_DV] for h in hs]
        p = [(_dot_nt(q[h], k[h]) * dec[h]).astype(BF16) for h in hs]
        qf = [q[h].astype(F32) for h in hs]
        ret = [_dot(p[h], v[h])
               + _dot((qf[h] * jnp.exp((pos + 1.0) * lgf[h])).astype(BF16), state[h].astype(BF16))
               + _dot((qf[h] * jnp.exp((c - pos) * lgb[h])).astype(BF16), sb_ref[sub, h]) for h in hs]
        kzt = [(k[h].astype(F32) * jnp.exp((c - 1.0 - pos) * lgf[h])).T.astype(BF16) for h in hs]
        state = [jnp.exp(c * lgf[h][:, :1]) * state[h] + _dot(kzt[h], v[h]) for h in hs]
        xc = [ret[h] - jnp.mean(ret[h], axis=-1, keepdims=True) for h in hs]
        y = [xc[h] * lax.rsqrt(jnp.mean(xc[h] * xc[h], axis=-1, keepdims=True) + NORM_EPS) for h in hs]
        for h in hs:
            sl = slice(h * RET_DV, (h + 1) * RET_DV)
            o_ref[rows, sl] = (g_ref[rows, sl].astype(F32) * y[h]).astype(BF16)
    for h in hs:
        s_scr[h] = state[h]


def _retention(qa, ka, va, ga, dec_f, dec_b):
    tt = qa.shape[0]
    r = RET_CHUNKS_PER_STEP
    rows = r * RET_CHUNK
    assert tt % rows == 0 and CTX_LEN % rows == 0
    n = tt // rows
    dec = jnp.broadcast_to(jnp.concatenate([dec_f, dec_b]).astype(F32)[:, None], (2 * RET_HEADS, V7X_LANES))
    kw, vw = RET_HEADS * RET_DK, BRANCH_W
    dec_spec = pl.BlockSpec((2 * RET_HEADS, V7X_LANES), lambda t: (0, 0))
    state_shape = (RET_HEADS, RET_DK, RET_DV)
    sb = pl.pallas_call(
        _ret_state_kernel,
        grid=(n,),
        in_specs=[dec_spec,
                  pl.BlockSpec((rows, kw), lambda t: (_ret_bwd_block(t, n), 0)),
                  pl.BlockSpec((rows, vw), lambda t: (_ret_bwd_block(t, n), 0))],
        out_specs=pl.BlockSpec((r,) + state_shape, lambda t: (_ret_bwd_block(t, n), 0, 0, 0)),
        out_shape=jax.ShapeDtypeStruct((n * r,) + state_shape, BF16),
        scratch_shapes=[pltpu.VMEM(state_shape, F32)],
        compiler_params=_cparams(("arbitrary",), 32 * 1024 * 1024),
        name="retention_state",
    )(dec, ka, va)
    return pl.pallas_call(
        _ret_out_kernel,
        grid=(n,),
        in_specs=[dec_spec,
                  pl.BlockSpec((rows, kw), lambda t: (t, 0)),
                  pl.BlockSpec((rows, kw), lambda t: (t, 0)),
                  pl.BlockSpec((rows, vw), lambda t: (t, 0)),
                  pl.BlockSpec((rows, vw), lambda t: (t, 0)),
                  pl.BlockSpec((r,) + state_shape, lambda t: (t, 0, 0, 0))],
        out_specs=pl.BlockSpec((rows, vw), lambda t: (t, 0)),
        out_shape=jax.ShapeDtypeStruct((tt, vw), BF16),
        scratch_shapes=[pltpu.VMEM(state_shape, F32)],
        compiler_params=_cparams(("arbitrary",), 32 * 1024 * 1024),
        name="retention_out",
    )(dec, qa, ka, va, ga, sb)


def _gqa_kernel(q_ref, kt_ref, v_ref, g_ref, o_ref, s0_scr, s1_scr, *, kv_block, n_kv):
    tq = q_ref.shape[0]
    heads = [slice(h * GQA_DIM, (h + 1) * GQA_DIM) for h in range(GQA_GROUP)]
    q = jnp.concatenate([q_ref[:, sl] for sl in heads], axis=0)
    rows = GQA_GROUP * tq

    def scores(j, s_ref):
        start = pl.multiple_of(j * kv_block, V7X_LANES)
        s_ref[...] = _dot(q, kt_ref[:, pl.ds(start, kv_block)])

    def softmax_pv(j, s_ref, carry):
        m, acc = carry
        start = pl.multiple_of(j * kv_block, V7X_LANES)
        s = s_ref[...]
        m_new = jnp.maximum(m, jnp.max(s, axis=-1, keepdims=True))
        p = jnp.exp2(s - m_new).astype(BF16)
        acc = jnp.exp2(m - m_new) * acc + _dot(p, v_ref[pl.ds(start, kv_block), :])
        return m_new, acc

    def body(i, carry):
        j = 2 * i
        scores(j + 1, s1_scr)
        carry = softmax_pv(j, s0_scr, carry)
        scores(j + 2, s0_scr)
        return softmax_pv(j + 1, s1_scr, carry)

    carry = (jnp.full((rows, 1), MASK_VALUE, F32), jnp.zeros((rows, 2 * GQA_DIM), F32))
    scores(0, s0_scr)
    carry = lax.fori_loop(0, (n_kv - 1) // 2, body, carry, unroll=True)
    if n_kv % 2 == 0:
        scores(n_kv - 1, s1_scr)
        carry = softmax_pv(n_kv - 2, s0_scr, carry)
        _, acc = softmax_pv(n_kv - 1, s1_scr, carry)
    else:
        _, acc = softmax_pv(n_kv - 1, s0_scr, carry)
    out = acc[:, :GQA_DIM] / acc[:, GQA_DIM:]
    for h, sl in enumerate(heads):
        o_ref[:, sl] = (g_ref[:, sl].astype(F32) * out[h * tq:(h + 1) * tq]).astype(BF16)


def _gqa_attention(qb, kbt, vb, gb, *, q_row0, n_q, n_keys, kv_block, out_rows):
    gw = GQA_GROUP * GQA_DIM
    tq = ATT_Q_BLOCK
    qoff = q_row0 // tq
    return pl.pallas_call(
        functools.partial(_gqa_kernel, kv_block=kv_block, n_kv=n_keys // kv_block),
        grid=(GQA_KV_HEADS, n_q // tq),
        in_specs=[pl.BlockSpec((tq, gw), lambda g, i: (qoff + i, g)),
                  pl.BlockSpec((GQA_DIM, n_keys), lambda g, i: (g, 0)),
                  pl.BlockSpec((n_keys, 2 * GQA_DIM), lambda g, i: (0, g)),
                  pl.BlockSpec((tq, gw), lambda g, i: (qoff + i, g))],
        out_specs=pl.BlockSpec((tq, gw), lambda g, i: (i, g)),
        out_shape=jax.ShapeDtypeStruct((out_rows, GQA_HEADS * GQA_DIM), BF16),
        scratch_shapes=[pltpu.VMEM((GQA_GROUP * tq, kv_block), F32)] * 2,
        compiler_params=_cparams(("arbitrary", "arbitrary"), 56 * 1024 * 1024),
        name="gqa_attention",
    )(qb, kbt, vb, gb)


def _out_proj_kernel(m1_ref, m2c_ref, m2l_ref, w_ref, resc_ref, resl_ref, mod_ref, gpost_ref, o_ref):
    d = o_ref.shape[-1]
    is_ctx = pl.program_id(0) == 0
    m2 = jnp.where(is_ctx, m2c_ref[...], m2l_ref[...])
    y = _dot(m1_ref[...], w_ref[:BRANCH_W, :]) + _dot(m2, w_ref[BRANCH_W:, :])
    mod = mod_ref[0]
    gate = jnp.where(is_ctx, mod[1:2, 2 * d:], mod[0:1, 2 * d:])
    res = jnp.where(is_ctx, resc_ref[...], resl_ref[...])
    o_ref[...] = res + gate * (_rms_rows(y) * gpost_ref[...])


def _out_proj(m1, m2_ctx, m2_lat, w_out, res_ctx, res_lat, mod, layer, g_post):
    tt = m1.shape[0]
    d = res_lat.shape[1]
    tm = ROW_BLOCK
    row = lambda w: pl.BlockSpec((tm, w), lambda i: (i, 0))
    lat = lambda w: pl.BlockSpec((tm, w), lambda i: (jnp.maximum(i - 1, 0), 0))
    ctx = lambda w: pl.BlockSpec((CTX_LEN, w), lambda i: (0, 0))
    return pl.pallas_call(
        _out_proj_kernel,
        grid=(tt // tm,),
        in_specs=[row(BRANCH_W), ctx(BRANCH_W), lat(BRANCH_W),
                  pl.BlockSpec((2 * BRANCH_W, d), lambda i: (0, 0)),
                  ctx(d), lat(d),
                  pl.BlockSpec((1, V7X_SUBLANES, 3 * d), lambda i: (layer, 0, 0)),
                  pl.BlockSpec((1, d), lambda i: (0, 0))],
        out_specs=row(d),
        out_shape=jax.ShapeDtypeStruct((tt, d), F32),
        compiler_params=_cparams(("arbitrary",), 40 * 1024 * 1024),
        name="out_proj",
    )(m1, m2_ctx, m2_lat, w_out, res_ctx, res_lat, mod, g_post.reshape(1, d))


def _odd_in_kernel(s_ref, mod_ref, gpre_ref, w_ref, q_ref, k_ref, v_ref, rw_ref, gc_ref, gd_ref):
    is_ctx = pl.program_id(0) == 0
    hb = _adaln(s_ref[...], mod_ref[0], is_ctx, gpre_ref[...]).astype(BF16)
    o = 0

    def seg(width):
        nonlocal o
        y = _dot(hb, w_ref[:, o:o + width])
        o += width
        return y

    q_ref[...] = (seg(BRANCH_W) * NA_DIM ** -0.5).astype(BF16)
    k_ref[...] = seg(BRANCH_W).astype(BF16)
    v_ref[...] = seg(BRANCH_W).astype(BF16)
    rw_ref[...] = seg(SHIFT_W)
    gc_ref[...] = _silu(seg(BRANCH_W)).astype(BF16)
    gd_ref[...] = _silu(seg(BRANCH_W)).astype(BF16)


def _odd_in_proj(stream, mod, g_pre, w_in):
    tt, d = stream.shape
    tm = ROW_BLOCK
    row = lambda w: pl.BlockSpec((tm, w), lambda i: (i, 0))
    widths = [(BRANCH_W, BF16), (BRANCH_W, BF16), (BRANCH_W, BF16), (SHIFT_W, F32), (BRANCH_W, BF16), (BRANCH_W, BF16)]
    return pl.pallas_call(
        _odd_in_kernel,
        grid=(tt // tm,),
        in_specs=[row(d),
                  pl.BlockSpec((1, V7X_SUBLANES, 3 * d), lambda i: (1, 0, 0)),
                  pl.BlockSpec((1, d), lambda i: (0, 0)),
                  pl.BlockSpec((d, ODD_IN), lambda i: (0, 0), pipeline_mode=pl.Buffered(1))],
        out_specs=[row(w) for w, _ in widths],
        out_shape=[jax.ShapeDtypeStruct((tt, w), dt) for w, dt in widths],
        compiler_params=_cparams(("arbitrary",), d * ODD_IN * 2 + 24 * 1024 * 1024),
        name="odd_in_proj",
    )(stream, mod, g_pre.reshape(1, d), w_in)


def _na_kernel(*refs, rows):
    nb = NA_BLOCKS_PER_STEP
    q_refs, g_refs = refs[:nb], refs[nb:2 * nb]
    k_ref, v_ref, bias_ref, o_ref = refs[2 * nb:]
    span = NA_UNION_ROWS * GRID_W
    bq = NA_BLOCK_ROWS * GRID_W
    first = lax.broadcasted_iota(jnp.int32, (bq, 2 * NA_DIM), 1) < NA_DIM
    kc, vc = k_ref[:CTX_LEN, :], v_ref[:CTX_LEN, :]
    blocks = range(nb)
    lhs, ku, vu, cls = [], [], [], []
    for b in blocks:
        qr0 = (pl.program_id(1) * nb + b) * NA_BLOCK_ROWS
        u0 = jnp.clip(qr0 - NA_WIN_ROWS // 2, 0, rows - NA_UNION_ROWS)
        cls.append(jnp.where(qr0 == 0, 0, jnp.where(qr0 == rows - NA_BLOCK_ROWS, 2, 1)))
        start = pl.multiple_of(CTX_LEN + u0 * GRID_W, GRID_W)
        ku.append(k_ref[pl.ds(start, span), :])
        vu.append(v_ref[pl.ds(start, span), :])
        q = q_refs[b][...]
        zero = jnp.zeros_like(q)
        lhs.append(jnp.concatenate([jnp.where(first, q, zero), jnp.where(first, zero, q)], axis=0))
    bias = [jnp.concatenate([bias_ref[0, cls[b]], bias_ref[1, cls[b]]], axis=0) for b in blocks]
    sw = [_dot_nt(lhs[b], ku[b]) + bias[b] for b in blocks]
    sc = [_dot_nt(lhs[b], kc) for b in blocks]
    m = [jnp.maximum(jnp.max(sw[b], axis=-1, keepdims=True), jnp.max(sc[b], axis=-1, keepdims=True)) for b in blocks]
    pw = [jnp.exp(sw[b] - m[b]) for b in blocks]
    pc = [jnp.exp(sc[b] - m[b]) for b in blocks]
    l = [jnp.sum(pw[b], axis=-1, keepdims=True) + jnp.sum(pc[b], axis=-1, keepdims=True) for b in blocks]
    o = [(_dot(pw[b].astype(BF16), vu[b]) + _dot(pc[b].astype(BF16), vc)) / l[b] for b in blocks]
    for b in blocks:
        out = jnp.where(first, o[b][:bq], o[b][bq:])
        o_ref[b * bq:(b + 1) * bq, :] = (g_refs[b][...].astype(F32) * out).astype(BF16)


def _na_bias_table(rpb):
    half = NA_WIN_ROWS // 2
    cols = np.arange(GRID_W)
    c0 = np.clip(cols - NA_WIN_COLS // 2, 0, GRID_W - NA_WIN_COLS)
    valid_c = (cols[None, :] >= c0[:, None]) & (cols[None, :] < c0[:, None] + NA_WIN_COLS)
    dc = np.clip(cols[None, :] - cols[:, None] + NA_WIN_COLS - 1, 0, 2 * NA_WIN_COLS - 2)
    a = np.arange(NA_BLOCK_ROWS)[:, None]
    i = np.arange(NA_UNION_ROWS)[None, :]
    dr = np.stack([i - a, i - a - half, i - a - (NA_UNION_ROWS - NA_BLOCK_ROWS)])
    w0 = np.stack([0 * a, a, (NA_UNION_ROWS - NA_WIN_ROWS) + 0 * a])
    valid_r = (i[None] >= w0) & (i[None] < w0 + NA_WIN_ROWS)
    onehot = (np.arange(2 * NA_WIN_COLS - 1)[:, None, None] == dc[None]) & valid_c[None]
    tiles = jnp.einsum('hrd,dqk->hrqk', rpb.astype(F32), jnp.asarray(onehot, F32), precision=lax.Precision.HIGHEST)
    tiles = jnp.where(jnp.asarray(valid_c)[None, None], tiles, MASK_VALUE)
    masked_tile = jnp.full((NA_HEADS, GRID_W, GRID_W), MASK_VALUE, F32)

    def tile(c, qa, ki):
        return tiles[:, dr[c, qa, ki] + NA_WIN_ROWS - 1] if valid_r[c, qa, ki] else masked_tile

    return jnp.stack([jnp.concatenate([jnp.concatenate([tile(c, qa, ki) for ki in range(NA_UNION_ROWS)], axis=-1)
                                       for qa in range(NA_BLOCK_ROWS)], axis=-2) for c in range(3)], axis=1)


def _neighbourhood_attention(q, k, v, gc, rpb):
    tt = q.shape[0]
    t = tt - CTX_LEN
    rows = t // GRID_W
    nb = NA_BLOCKS_PER_STEP
    assert NA_BLOCK_ROWS == NA_WIN_ROWS // 2 and rows >= NA_UNION_ROWS and rows % (nb * NA_BLOCK_ROWS) == 0
    bias = _na_bias_table(rpb)
    pw = 2 * NA_DIM
    bq = NA_BLOCK_ROWS * GRID_W
    qoff = CTX_LEN // bq
    assert CTX_LEN % bq == 0
    blk = lambda b: pl.BlockSpec((bq, pw), lambda p, i: (qoff + i * nb + b, p))
    resident = pl.BlockSpec((tt, pw), lambda p, i: (0, p))
    return pl.pallas_call(
        functools.partial(_na_kernel, rows=rows),
        grid=(NA_HEADS // 2, rows // (nb * NA_BLOCK_ROWS)),
        in_specs=[blk(b) for b in range(nb)] + [blk(b) for b in range(nb)] + [
            resident, resident,
            pl.BlockSpec((2,) + bias.shape[1:], lambda p, i: (p, 0, 0, 0))],
        out_specs=pl.BlockSpec((nb * bq, pw), lambda p, i: (i, p)),
        out_shape=jax.ShapeDtypeStruct((t, BRANCH_W), BF16),
        compiler_params=_cparams(("arbitrary", "arbitrary"), 48 * 1024 * 1024),
        name="neighbourhood_attention",
    )(*([q] * nb + [gc] * nb + [k, v, bias]))


def _rwkv_block_index(d, t, nblk):
    nctx = CTX_LEN // (RWKV_CHUNKS_PER_STEP * RWKV_CHUNK)
    return t if d == 0 else jnp.where(t < nctx, nctx - 1 - t, nblk - 1 - (t - nctx))


def _rwkv_prepare(d, c, nch, p, prev_row, next_row, mu_ref, w0_ref, w2_ref, a0_ref, a2_ref, kk_ref, ka_ref,
                  rk_ref, e_ref, et_ref, bon_ref, out_rows):
    lc = RWKV_CHUNK
    nctx = CTX_LEN // lc
    sgn = 1 - 2 * d
    row = lax.broadcasted_iota(jnp.int32, p.shape, 0)
    first_zero = jnp.logical_or(c == 0, c == nctx)
    last_zero = jnp.logical_or(c == nctx - 1, c == nch - 1)
    pr = jnp.where(first_zero, 0.0, prev_row)
    nx = jnp.where(last_zero, 0.0, next_row)
    prev = jnp.where(row == 0, pr, pltpu.roll(p, 1, 0))
    nxt = jnp.where(row == lc - 1, nx, pltpu.roll(p, lc - 1, 0))
    z = p + (0.5 * (prev + nxt) - p) * mu_ref[...]

    bw = BRANCH_W
    r, k, v = z[:, :bw], z[:, bw:2 * bw], z[:, 2 * bw:3 * bw]
    zw = z[:, 3 * bw:3 * bw + 2 * DECAY_LORA]
    za = z[:, 3 * bw + 2 * DECAY_LORA:]
    lw = w0_ref[d] + _dot(jnp.tanh(zw).astype(BF16), w2_ref[d])
    ld = -float(np.exp(-0.5)) * _sigmoid(lw)
    asig = _sigmoid(a0_ref[d] + _dot(za.astype(BF16), a2_ref[d]))
    kk = k * kk_ref[...]
    kd = k * (1.0 + (asig - 1.0) * ka_ref[...])
    kk_sq, rkd = _head_sums([kk * kk, r * kd * rk_ref[...]], e_ref[...], et_ref[...])
    kkn = kk / jnp.maximum(jnp.sqrt(kk_sq), 1e-12)
    a_vec = -kkn
    b_vec = kkn * asig
    bon_ref[out_rows, :] = rkd * v

    ti = lax.broadcasted_iota(jnp.int32, (lc, lc), 0)
    si = lax.broadcasted_iota(jnp.int32, (lc, lc), 1)
    cum = _dot_split_rhs(jnp.where(sgn * (ti - si) >= 0, 1.0, 0.0).astype(BF16), ld)
    tot = jnp.sum(ld, axis=0, keepdims=True)
    rem = jnp.exp(tot - cum)
    pinv = jnp.exp(-cum)
    rt = r * jnp.exp(cum)
    kt = kd * pinv
    bt = b_vec * pinv
    at = a_vec * jnp.exp(cum - ld)
    bh = b_vec * rem
    kh = kd * rem
    pend = jnp.exp(tot)
    return dict(at=at, rt=rt, bt=bt, kt=kt, bh=bh, kh=kh, v=v, pend=pend)


def _rwkv_kernel(main_f, prev_f, next_f, main_b, prev_b, next_b, mu_ref, w0_ref, w2_ref, a0_ref, a2_ref, kk_ref,
                 ka_ref, rk_ref, e_ref, et_ref, wkv_f, wkv_b, bon_f, bon_b, s_scr, *, nch):
    lc = RWKV_CHUNK
    hd = RWKV_DIM
    t = pl.program_id(0)

    @pl.when(t == 0)
    def _():
        s_scr[...] = jnp.zeros_like(s_scr)

    shared = (mu_ref, w0_ref, w2_ref, a0_ref, a2_ref, kk_ref, ka_ref, rk_ref, e_ref, et_ref)
    nsub = RWKV_CHUNKS_PER_STEP
    blocks = ((main_f, prev_f, next_f, bon_f), (main_b, prev_b, next_b, bon_b))
    wkv_refs = (wkv_f, wkv_b)

    def sub_rows(k, d):
        sub = k if d == 0 else nsub - 1 - k
        return sub, slice(sub * lc, (sub + 1) * lc)

    rows = {}
    for k in range(nsub):
        for d, (main, prev, nxt, bon) in enumerate(blocks):
            sub, rs = sub_rows(k, d)
            c = _rwkv_block_index(d, t, nch // nsub) * nsub + sub
            prev_row = main[sub * lc - 1:sub * lc, :] if sub > 0 else prev[V7X_SUBLANES - 1:V7X_SUBLANES, :]
            next_row = main[(sub + 1) * lc:(sub + 1) * lc + 1, :] if sub < nsub - 1 else nxt[0:1, :]
            rows[k, d] = _rwkv_prepare(d, c, nch, main[rs, :], prev_row, next_row, *shared, bon, rs)

    pw = 2 * hd
    lane = lax.broadcasted_iota(jnp.int32, (lc, pw), 1)
    tok = lax.broadcasted_iota(jnp.int32, (lc, pw), 0)
    first = lane < hd
    fwd_diff = tok - (lane & (hd - 1))
    eye = (fwd_diff == 0).astype(F32)
    rr = lax.broadcasted_iota(jnp.int32, (pw, pw), 0)
    cc = lax.broadcasted_iota(jnp.int32, (pw, pw), 1)
    same_head = (rr < hd) == (cc < hd)
    probs = [(k, d, p) for k in range(nsub) for d in range(2) for p in range(RWKV_HEADS // 2)]
    n = range(len(probs))
    incl = [(fwd_diff >= 0) if d == 0 else (fwd_diff <= 0) for _, d, _ in probs]
    strict = [(fwd_diff > 0) if d == 0 else (fwd_diff < 0) for _, d, _ in probs]

    def bdiag(m):
        m = m.astype(BF16)
        zero = jnp.zeros_like(m)
        return jnp.concatenate([jnp.where(first, m, zero), jnp.where(first, zero, m)], axis=0)

    def part(name):
        return [rows[k, d][name][:, p * pw:(p + 1) * pw] for k, d, p in probs]

    at_p, rt_p, v_p = part("at"), part("rt"), part("v")
    bt_p, kt_p, pend_p = part("bt"), part("kt"), part("pend")
    bh_p = [m.astype(BF16) for m in part("bh")]
    kh_p = [m.astype(BF16) for m in part("kh")]
    lhs = [jnp.concatenate([at_p[i], rt_p[i]], axis=0).astype(BF16) for i in n]
    gram = [_dot_nt(lhs[i], jnp.concatenate([bdiag(bt_p[i]), bdiag(kt_p[i])], axis=0)) for i in n]
    gb = [g[:, :pw] for g in gram]
    gk = [g[:, pw:] for g in gram]
    aab = [jnp.where(strict[i], gb[i][:lc], 0.0) for i in n]
    arb = [jnp.where(incl[i], gb[i][lc:], 0.0) for i in n]
    aak = [jnp.where(strict[i], gk[i][:lc], 0.0) for i in n]
    ark = [jnp.where(incl[i], gk[i][lc:], 0.0) for i in n]
    vbd = [bdiag(m) for m in v_p]
    aakv = [_dot(aak[i].astype(BF16), vbd[i]) for i in n]
    s_tok = lane & (hd - 1)

    def same_block(size):
        shift = int(np.log2(size))
        return (tok >> shift) == (s_tok >> shift)

    nd = [jnp.where(same_block(RWKV_INV_BASE), m, 0.0) for m in aab]
    x = [eye + m for m in nd]
    nk = [_dot(m.astype(BF16), bdiag(m)) for m in nd]
    for _ in range(int(np.log2(RWKV_INV_BASE)) - 2):
        out = [_dot(jnp.concatenate([nk[i], x[i]], axis=0).astype(BF16), bdiag(nk[i])) for i in n]
        nk = [o[:lc] for o in out]
        x = [x[i] + out[i][lc:] for i in n]
    x = [x[i] + _dot(x[i].astype(BF16), bdiag(nk[i])) for i in n]
    size = RWKV_INV_BASE
    while size < lc:
        couple = jnp.logical_and(same_block(2 * size), jnp.logical_not(same_block(size)))
        xn = [_dot(x[i].astype(BF16), bdiag(jnp.where(couple, aab[i], 0.0))) for i in n]
        x = [x[i] + _dot(xn[i].astype(BF16), bdiag(x[i])) for i in n]
        size *= 2
    xc = [_dot(x[i].astype(BF16), jnp.concatenate([bdiag(at_p[i]), bdiag(aakv[i])], axis=1)) for i in n]
    ahat = [m[:, :pw] for m in xc]
    wmat = [m[:, pw:] for m in xc]
    rhat = [rt_p[i] + _dot(arb[i].astype(BF16), bdiag(ahat[i])) for i in n]
    y0 = [_dot(jnp.concatenate([arb[i], ark[i]], axis=1).astype(BF16),
               jnp.concatenate([bdiag(wmat[i]), vbd[i]], axis=0)) for i in n]
    mab = [jnp.where(same_head, _dot(ahat[i].T.astype(BF16), bh_p[i]), 0.0).astype(BF16) for i in n]
    gtf = [_dot(jnp.concatenate([wmat[i].T, v_p[i].T], axis=1).astype(BF16),
                jnp.concatenate([bh_p[i], kh_p[i]], axis=0)) for i in n]
    for i, (k, d, p) in enumerate(probs):
        s0 = s_scr[d, p]
        wkv_refs[d][sub_rows(k, d)[1], p * pw:(p + 1) * pw] = _dot_nt(rhat[i].astype(BF16), bdiag(s0)) + y0[i]
        s_scr[d, p] = (s0 * pend_p[i] + _dot(s0.astype(BF16), mab[i])
                       + jnp.where(first, gtf[i][:lc], gtf[i][lc:]))


def _head_indicator(width, head_dim):
    idx = np.arange(width) // head_dim
    return jnp.asarray(idx[:, None] == np.arange(V7X_LANES)[None, :], BF16)


def _rwkv(rw_p, mu, w0s, w2s, a0s, a2s, k_k, k_a, r_k):
    tt, w = rw_p.shape
    lc = RWKV_CHUNK
    nch = tt // lc
    bw = BRANCH_W
    sub = V7X_SUBLANES
    zeros = jnp.zeros((DECAY_LORA, bw), F32)
    w2p = jnp.stack([jnp.concatenate([w2s[0], zeros]), jnp.concatenate([zeros, w2s[1]])]).astype(BF16)
    a2p = jnp.stack([jnp.concatenate([a2s[0], zeros]), jnp.concatenate([zeros, a2s[1]])]).astype(BF16)
    w0 = jnp.stack(w0s).reshape(2, 1, bw)
    a0 = jnp.stack(a0s).reshape(2, 1, bw)
    e_mat = _head_indicator(bw, RWKV_DIM)
    const = lambda *shape: pl.BlockSpec(shape, lambda t: tuple(0 for _ in shape))

    rows = RWKV_CHUNKS_PER_STEP * lc
    nblk = tt // rows
    assert tt % rows == 0 and CTX_LEN % rows == 0

    def block_specs(d):
        bidx = lambda t: _rwkv_block_index(d, t, nblk)
        return [pl.BlockSpec((rows, w), lambda t: (bidx(t), 0)),
                pl.BlockSpec((sub, w), lambda t: (jnp.maximum(bidx(t) * (rows // sub) - 1, 0), 0)),
                pl.BlockSpec((sub, w), lambda t: (jnp.minimum((bidx(t) + 1) * (rows // sub), tt // sub - 1), 0))]

    out_spec = lambda d: pl.BlockSpec((rows, bw), lambda t: (_rwkv_block_index(d, t, nblk), 0))
    return pl.pallas_call(
        functools.partial(_rwkv_kernel, nch=nch),
        grid=(nblk,),
        in_specs=block_specs(0) + block_specs(1) + [
            const(1, w), const(2, 1, bw), const(2, 2 * DECAY_LORA, bw), const(2, 1, bw), const(2, 2 * ICLR_LORA, bw),
            const(1, bw), const(1, bw), const(1, bw), const(bw, V7X_LANES), const(V7X_LANES, bw)],
        out_specs=[out_spec(0), out_spec(1), out_spec(0), out_spec(1)],
        out_shape=[jax.ShapeDtypeStruct((tt, bw), F32)] * 4,
        scratch_shapes=[pltpu.VMEM((2, RWKV_HEADS // 2, RWKV_DIM, 2 * RWKV_DIM), F32)],
        compiler_params=_cparams(("arbitrary",), 48 * 1024 * 1024),
        name="rwkv7",
    )(rw_p, rw_p, rw_p, rw_p, rw_p, rw_p, mu.reshape(1, w), w0, w2p, a0, a2p,
      k_k.reshape(1, bw), k_a.reshape(1, bw), r_k.reshape(1, bw), e_mat, e_mat.T)


def _odd_out_kernel(na_ref, wkvf_ref, wkvb_ref, bonf_ref, bonb_ref, gd_ref, lng_ref, lnb_ref, e_ref, et_ref, w_ref,
                    res_ref, mod_ref, gpost_ref, o_ref):
    d = o_ref.shape[-1]
    e, et = e_ref[...], et_ref[...]
    inv = 1.0 / RWKV_DIM
    nrow = o_ref.shape[0] // ODD_OUT_ROW_GROUPS
    groups = [slice(g * nrow, (g + 1) * nrow) for g in range(ODD_OUT_ROW_GROUPS)]
    wkv = [wkvf_ref[g, :] + wkvb_ref[g, :] for g in groups]
    mean = [_head_sums([x], e, et)[0] * inv for x in wkv]
    xc = [x - mu for x, mu in zip(wkv, mean)]
    var = [_head_sums([x * x], e, et)[0] * inv for x in xc]
    y = [xc[i] * lax.rsqrt(var[i] + RWKV_GN_EPS) * lng_ref[...] + lnb_ref[...] + bonf_ref[g, :] + bonb_ref[g, :]
         for i, g in enumerate(groups)]
    m2 = jnp.concatenate([(gd_ref[g, :].astype(F32) * y[i]).astype(BF16) for i, g in enumerate(groups)], axis=0)
    out = _dot(na_ref[...], w_ref[:BRANCH_W, :]) + _dot(m2, w_ref[BRANCH_W:, :])
    gate = mod_ref[0][0:1, 2 * d:]
    o_ref[...] = res_ref[...] + gate * (_rms_rows(out) * gpost_ref[...])


def _odd_out(na_g, wkv_f, wkv_b, bon_f, bon_b, gd, lnx_g, lnx_b, w_out, stream, mod, g_post):
    t, bw = na_g.shape
    d = stream.shape[1]
    tm = ROW_BLOCK
    off = CTX_LEN // tm
    lat = lambda w: pl.BlockSpec((tm, w), lambda i: (off + i, 0))
    vec = lambda w: pl.BlockSpec((1, w), lambda i: (0, 0))
    e_mat = _head_indicator(bw, RWKV_DIM)
    return pl.pallas_call(
        _odd_out_kernel,
        grid=(t // tm,),
        in_specs=[pl.BlockSpec((tm, bw), lambda i: (i, 0)), lat(bw), lat(bw), lat(bw), lat(bw), lat(bw),
                  vec(bw), vec(bw),
                  pl.BlockSpec((bw, V7X_LANES), lambda i: (0, 0)),
                  pl.BlockSpec((V7X_LANES, bw), lambda i: (0, 0)),
                  pl.BlockSpec((2 * bw, d), lambda i: (0, 0)),
                  lat(d),
                  pl.BlockSpec((1, V7X_SUBLANES, 3 * d), lambda i: (1, 0, 0)),
                  vec(d)],
        out_specs=pl.BlockSpec((tm, d), lambda i: (i, 0)),
        out_shape=jax.ShapeDtypeStruct((t, d), F32),
        compiler_params=_cparams(("arbitrary",), 48 * 1024 * 1024),
        name="odd_out_proj",
    )(na_g, wkv_f, wkv_b, bon_f, bon_b, gd, lnx_g.reshape(1, bw), lnx_b.reshape(1, bw), e_mat, e_mat.T, w_out,
      stream, mod, g_post.reshape(1, d))


def _rope_tables(t):
    n_rows = t // GRID_W
    n_freq = RET_DK // 4
    inv = ROPE_THETA ** (-jnp.arange(n_freq, dtype=F32) / n_freq)
    row_ang = jnp.arange(n_rows, dtype=F32)[:, None] * inv
    col_ang = jnp.arange(GRID_W, dtype=F32)[:, None] * inv

    def table(fn):
        rows_part = jnp.broadcast_to(fn(row_ang)[:, None, :], (n_rows, GRID_W, n_freq))
        cols_part = jnp.broadcast_to(fn(col_ang)[None, :, :], (n_rows, GRID_W, n_freq))
        return jnp.repeat(jnp.concatenate([rows_part, cols_part], axis=-1).reshape(t, 2 * n_freq), 2, axis=-1)

    cos = table(jnp.cos)
    sin = table(jnp.sin) * jnp.tile(jnp.asarray([-1.0, 1.0], F32), RET_DK // 2)
    cos = jnp.concatenate([jnp.ones((CTX_LEN, RET_DK), F32), cos], axis=0)
    sin = jnp.concatenate([jnp.zeros((CTX_LEN, RET_DK), F32), sin], axis=0)
    return cos, sin


def kernel(x, c, ctx, c_ctx, w_mod, b_mod, g_pre, g_post, ev_w_in, ev_w_out, ret_decay_fwd, ret_decay_bwd, gqa_q_norm, gqa_k_norm, od_w_in, od_w_out, na_rpb, rwkv_shift_mu, rwkv_w0_fwd, rwkv_w2_fwd, rwkv_w0_bwd, rwkv_w2_bwd, rwkv_a0_fwd, rwkv_a2_fwd, rwkv_a0_bwd, rwkv_a2_bwd, rwkv_k_k, rwkv_k_a, rwkv_r_k, rwkv_lnx_g, rwkv_lnx_b):
    assert x.shape[0] == 1 and DEPTH == 2 and RET_DK == GQA_DIM
    t = x.shape[1]
    tt = t + CTX_LEN
    assert t % ROW_BLOCK == 0 and tt % ATT_KV_BLOCK == 0 and t % GRID_W == 0
    x2, ctx2 = x[0], ctx[0]
    mod = _modulation(c, c_ctx, w_mod, b_mod)
    cos_t, sin_t = _rope_tables(t)

    qa, ka, va, qb, kbt, vb, ga, gb = _even_in_proj(x2, ctx2, mod, g_pre[0], ev_w_in[0].astype(BF16), cos_t, sin_t,
                                                    gqa_q_norm[0], gqa_k_norm[0])
    ret_g = _retention(qa, ka, va, ga, ret_decay_fwd[0], ret_decay_bwd[0])
    att_lat = _gqa_attention(qb, kbt, vb, gb, q_row0=CTX_LEN, n_q=t, n_keys=tt, kv_block=ATT_KV_BLOCK, out_rows=t)
    att_ctx = _gqa_attention(qb, kbt, vb, gb, q_row0=0, n_q=CTX_LEN, n_keys=CTX_LEN, kv_block=CTX_LEN,
                             out_rows=CTX_LEN)
    stream1 = _out_proj(ret_g, att_ctx, att_lat, ev_w_out[0].astype(BF16), ctx2, x2, mod, 0, g_post[0])

    q, k, v, rw_p, gc, gd = _odd_in_proj(stream1, mod, g_pre[1], od_w_in[0].astype(BF16))
    na_g = _neighbourhood_attention(q, k, v, gc, na_rpb[0])
    rw = _rwkv(rw_p, rwkv_shift_mu[0], (rwkv_w0_fwd[0], rwkv_w0_bwd[0]), (rwkv_w2_fwd[0], rwkv_w2_bwd[0]),
               (rwkv_a0_fwd[0], rwkv_a0_bwd[0]), (rwkv_a2_fwd[0], rwkv_a2_bwd[0]),
               rwkv_k_k[0], rwkv_k_a[0], rwkv_r_k[0].reshape(-1))
    out = _odd_out(na_g, *rw, gd, rwkv_lnx_g[0], rwkv_lnx_b[0], od_w_out[0].astype(BF16), stream1, mod, g_post[1])
    return out[None]
```

```python
import functools

import jax
import jax.numpy as jnp
import numpy as np
from jax import lax
from jax.experimental import pallas as pl
from jax.experimental.pallas import tpu as pltpu

F32 = jnp.float32
BF16 = jnp.bfloat16

D_MODEL = 1024
DEPTH = 2
GRID_W = 64
CTX_LEN = 256
BRANCH_W = D_MODEL
RET_HEADS = 4
RET_DK = 128
RET_DV = BRANCH_W // RET_HEADS
RET_CHUNK = 128
GQA_HEADS = 8
GQA_KV_HEADS = 2
GQA_DIM = BRANCH_W // GQA_HEADS
GQA_GROUP = GQA_HEADS // GQA_KV_HEADS
NA_HEADS = 16
NA_DIM = BRANCH_W // NA_HEADS
NA_WIN_ROWS = 8
NA_WIN_COLS = 16
RWKV_HEADS = 16
RWKV_DIM = BRANCH_W // RWKV_HEADS
DECAY_LORA = 64
ICLR_LORA = 64
ROPE_THETA = 10000.0
NORM_EPS = 1e-6
RWKV_GN_EPS = 64e-5
SHIFT_W = 3 * BRANCH_W + 2 * DECAY_LORA + 2 * ICLR_LORA
EVEN_IN = 2 * RET_HEADS * RET_DK + BRANCH_W + GQA_HEADS * GQA_DIM + 2 * GQA_KV_HEADS * GQA_DIM + 2 * BRANCH_W
ODD_IN = 3 * BRANCH_W + SHIFT_W + 2 * BRANCH_W

V7X_LANES = 128
V7X_SUBLANES = 8
V7X_VMEM_BYTES = 64 * 1024 * 1024

ROW_BLOCK = CTX_LEN
RWKV_CHUNK = 64
RWKV_CHUNKS_PER_STEP = 4
RWKV_INV_BASE = 8
RET_CHUNKS_PER_STEP = 2
ATT_Q_BLOCK = 256
ATT_KV_BLOCK = 1280
LOG2_E = 1.4426950408889634
NA_BLOCK_ROWS = 4
NA_UNION_ROWS = 12
NA_BLOCKS_PER_STEP = 2
ODD_OUT_ROW_GROUPS = 2
MASK_VALUE = -1e30


def _vmem_limit(nbytes):
    return int(min(V7X_VMEM_BYTES - 4 * 1024 * 1024, max(32 * 1024 * 1024, nbytes)))


def _cparams(sem, vmem_bytes):
    return pltpu.CompilerParams(dimension_semantics=sem, vmem_limit_bytes=_vmem_limit(vmem_bytes))


def _silu(x):
    return x / (1.0 + jnp.exp(-x))


def _sigmoid(x):
    return 1.0 / (1.0 + jnp.exp(-x))


def _dot(a, b):
    return jnp.dot(a, b, preferred_element_type=F32)


def _dot_nt(a, b):
    return lax.dot_general(a, b, (((1,), (1,)), ((), ())), preferred_element_type=F32)


def _split2(x):
    hi = x.astype(BF16)
    lo = (x - hi.astype(F32)).astype(BF16)
    return hi, lo


def _dot_split_rhs(a_bf16, x):
    hi, lo = _split2(x)
    return _dot(a_bf16, hi) + _dot(a_bf16, lo)


def _head_sums(xs, e, et):
    n = xs[0].shape[0]

    def stacked_dot(vals, w):
        parts = [t for x in vals for t in _split2(x)]
        y = _dot(jnp.concatenate(parts, axis=0), w)
        return [y[2 * i * n:(2 * i + 1) * n] + y[(2 * i + 1) * n:(2 * i + 2) * n] for i in range(len(vals))]

    return stacked_dot(stacked_dot(xs, e), et)


def _rms_rows(x):
    return x * lax.rsqrt(jnp.mean(x * x, axis=-1, keepdims=True) + NORM_EPS)


def _rope(t, cos, sin_signed, even):
    nxt = pltpu.roll(t, t.shape[1] - 1, 1)
    prv = pltpu.roll(t, 1, 1)
    return t * cos + jnp.where(even, nxt, prv) * sin_signed


def _mod_kernel(cc_ref, w_ref, b_ref, o_ref):
    s = _silu(cc_ref[...])
    o_ref[0] = jnp.dot(s, w_ref[0], preferred_element_type=F32, precision=lax.Precision.HIGHEST) + b_ref[0]


def _modulation(c, c_ctx, w_mod, b_mod):
    d = c.shape[-1]
    cc = jnp.concatenate([c[:1], c_ctx[None, :], jnp.zeros((V7X_SUBLANES - 2, d), F32)], axis=0)
    return pl.pallas_call(
        _mod_kernel,
        grid=(DEPTH, 3),
        in_specs=[pl.BlockSpec((V7X_SUBLANES, d), lambda l, j: (0, 0)),
                  pl.BlockSpec((1, d, d), lambda l, j: (l, 0, j)),
                  pl.BlockSpec((1, 1, d), lambda l, j: (l, 0, j))],
        out_specs=pl.BlockSpec((1, V7X_SUBLANES, d), lambda l, j: (l, 0, j)),
        out_shape=jax.ShapeDtypeStruct((DEPTH, V7X_SUBLANES, 3 * d), F32),
        compiler_params=_cparams(("arbitrary", "arbitrary"), 40 * 1024 * 1024),
        name="modulation",
    )(cc, w_mod, b_mod.reshape(DEPTH, 1, 3 * d))


def _adaln(xb, mod, is_ctx, g_pre):
    d = xb.shape[-1]
    m = jnp.where(is_ctx, mod[1:2, :], mod[0:1, :])
    shift, scale = m[:, :d], m[:, d:2 * d]
    return (_rms_rows(xb) * g_pre) * (1.0 + scale) + shift


def _even_in_kernel(x_ref, ctx_ref, mod_ref, gpre_ref, w_ref, cos_ref, sin_ref, qn_ref, kn_ref,
                    qa_ref, ka_ref, va_ref, qb_ref, kbt_ref, vb_ref, ga_ref, gb_ref):
    is_ctx = pl.program_id(0) == 0
    xb = jnp.where(is_ctx, ctx_ref[...], x_ref[...])
    hb = _adaln(xb, mod_ref[0], is_ctx, gpre_ref[...]).astype(BF16)
    cos, sin_s = cos_ref[...], sin_ref[...]
    even = (lax.broadcasted_iota(jnp.int32, cos.shape, 1) & 1) == 0
    o = 0

    def seg(width):
        nonlocal o
        y = _dot(hb, w_ref[:, o:o + width])
        o += width
        return y

    y = seg(RET_HEADS * RET_DK)
    for h in range(RET_HEADS):
        sl = slice(h * RET_DK, (h + 1) * RET_DK)
        qa_ref[:, sl] = _rope(y[:, sl], cos, sin_s, even).astype(BF16)
    y = seg(RET_HEADS * RET_DK)
    for h in range(RET_HEADS):
        sl = slice(h * RET_DK, (h + 1) * RET_DK)
        ka_ref[:, sl] = _rope(y[:, sl] * RET_DK ** -0.5, cos, sin_s, even).astype(BF16)
    va_ref[...] = seg(BRANCH_W).astype(BF16)
    y = seg(GQA_HEADS * GQA_DIM)
    for h in range(GQA_HEADS):
        sl = slice(h * GQA_DIM, (h + 1) * GQA_DIM)
        t = _rms_rows(y[:, sl]) * qn_ref[...]
        qb_ref[:, sl] = (_rope(t, cos, sin_s, even) * (GQA_DIM ** -0.5 * LOG2_E)).astype(BF16)
    y = seg(GQA_KV_HEADS * GQA_DIM)
    for h in range(GQA_KV_HEADS):
        sl = slice(h * GQA_DIM, (h + 1) * GQA_DIM)
        t = _rope(_rms_rows(y[:, sl]) * kn_ref[...], cos, sin_s, even)
        kbt_ref[sl, :] = t.T.astype(BF16)
    y = seg(GQA_KV_HEADS * GQA_DIM)
    for h in range(GQA_KV_HEADS):
        vb_ref[:, 2 * h * GQA_DIM:(2 * h + 1) * GQA_DIM] = y[:, h * GQA_DIM:(h + 1) * GQA_DIM].astype(BF16)
        vb_ref[:, (2 * h + 1) * GQA_DIM:(2 * h + 2) * GQA_DIM] = jnp.ones((y.shape[0], GQA_DIM), BF16)
    ga_ref[...] = _silu(seg(BRANCH_W)).astype(BF16)
    gb_ref[...] = _silu(seg(BRANCH_W)).astype(BF16)


def _even_in_proj(x, ctx, mod, g_pre, w_in, cos_t, sin_t, q_norm, k_norm):
    t, d = x.shape
    tt = t + CTX_LEN
    nblk = tt // ROW_BLOCK
    tm = ROW_BLOCK
    kvw = GQA_KV_HEADS * GQA_DIM
    row = lambda w: pl.BlockSpec((tm, w), lambda i: (i, 0))
    const = lambda shape: pl.BlockSpec(shape, lambda i: tuple(0 for _ in shape))
    outs = [((tt, RET_HEADS * RET_DK), row(RET_HEADS * RET_DK)),
            ((tt, RET_HEADS * RET_DK), row(RET_HEADS * RET_DK)),
            ((tt, BRANCH_W), row(BRANCH_W)),
            ((tt, GQA_HEADS * GQA_DIM), row(GQA_HEADS * GQA_DIM)),
            ((kvw, tt), pl.BlockSpec((kvw, tm), lambda i: (0, i))),
            ((tt, 2 * kvw), row(2 * kvw)),
            ((tt, BRANCH_W), row(BRANCH_W)),
            ((tt, BRANCH_W), row(BRANCH_W))]
    return pl.pallas_call(
        _even_in_kernel,
        grid=(nblk,),
        in_specs=[pl.BlockSpec((tm, d), lambda i: (jnp.maximum(i - 1, 0), 0)),
                  const((CTX_LEN, d)),
                  pl.BlockSpec((1, V7X_SUBLANES, 3 * d), lambda i: (0, 0, 0)),
                  const((1, d)),
                  const((d, EVEN_IN)),
                  row(RET_DK), row(RET_DK),
                  const((1, GQA_DIM)), const((1, GQA_DIM))],
        out_specs=[s for _, s in outs],
        out_shape=[jax.ShapeDtypeStruct(shp, BF16) for shp, _ in outs],
        compiler_params=_cparams(("arbitrary",), 2 * d * EVEN_IN * 2 + 16 * 1024 * 1024),
        name="even_in_proj",
    )(x, ctx, mod, g_pre.reshape(1, d), w_in, cos_t, sin_t, q_norm.reshape(1, -1), k_norm.reshape(1, -1))


def _log_sigmoid(x):
    return jnp.minimum(x, 0.0) - jnp.log(1.0 + jnp.exp(-jnp.abs(x)))


def _ret_bwd_block(t, nblocks):
    nctx = CTX_LEN // (RET_CHUNKS_PER_STEP * RET_CHUNK)
    return jnp.where(t < nctx, nctx - 1 - t, nblocks - 1 - (t - nctx))


def _ret_state_kernel(dec_ref, k_ref, v_ref, sb_ref, s_scr):
    c = RET_CHUNK

    @pl.when(pl.program_id(0) == 0)
    def _():
        s_scr[...] = jnp.zeros_like(s_scr)

    lg = _log_sigmoid(dec_ref[...])
    pos = lax.broadcasted_iota(jnp.int32, (c, RET_DK), 0).astype(F32)
    hs = range(RET_HEADS)
    lgb = [lg[RET_HEADS + h:RET_HEADS + h + 1, :] for h in hs]
    state = [s_scr[h] for h in hs]
    for sub in reversed(range(RET_CHUNKS_PER_STEP)):
        rows = slice(sub * c, (sub + 1) * c)
        for h in hs:
            sb_ref[sub, h] = state[h].astype(BF16)
        kzt = [(k_ref[rows, h * RET_DK:(h + 1) * RET_DK].astype(F32) * jnp.exp(pos * lgb[h])).T.astype(BF16)
               for h in hs]
        state = [jnp.exp(c * lgb[h][:, :1]) * state[h] + _dot(kzt[h], v_ref[rows, h * RET_DV:(h + 1) * RET_DV])
                 for h in hs]
    for h in hs:
        s_scr[h] = state[h]


def _ret_out_kernel(dec_ref, q_ref, k_ref, v_ref, g_ref, sb_ref, o_ref, s_scr):
    c = RET_CHUNK

    @pl.when(pl.program_id(0) == 0)
    def _():
        s_scr[...] = jnp.zeros_like(s_scr)

    lg = _log_sigmoid(dec_ref[...])
    ii = lax.broadcasted_iota(jnp.int32, (c, c), 0)
    jj = lax.broadcasted_iota(jnp.int32, (c, c), 1)
    dlt = (ii - jj).astype(F32)
    pos = lax.broadcasted_iota(jnp.int32, (c, RET_DK), 0).astype(F32)
    hs = range(RET_HEADS)
    lgf = [lg[h:h + 1, :] for h in hs]
    lgb = [lg[RET_HEADS + h:RET_HEADS + h + 1, :] for h in hs]
    dec = [jnp.where(dlt > 0, jnp.exp(jnp.maximum(dlt, 0.0) * lgf[h]),
                     jnp.where(dlt < 0, jnp.exp(jnp.maximum(-dlt, 0.0) * lgb[h]), 2.0)) for h in hs]
    state = [s_scr[h] for h in hs]
    for sub in range(RET_CHUNKS_PER_STEP):
        rows = slice(sub * c, (sub + 1) * c)
        q = [q_ref[rows, h * RET_DK:(h + 1) * RET_DK] for h in hs]
        k = [k_ref[rows, h * RET_DK:(h + 1) * RET_DK] for h in hs]
        v = [v_ref[rows, h * RET_DV:(h + 1) * RET_DV] for h in hs]
        p = [(_dot_nt(q[h], k[h]) * dec[h]).astype(BF16) for h in hs]
        qf = [q[h].astype(F32) for h in hs]
        ret = [_dot(p[h], v[h])
               + _dot((qf[h] * jnp.exp((pos + 1.0) * lgf[h])).astype(BF16), state[h].astype(BF16))
               + _dot((qf[h] * jnp.exp((c - pos) * lgb[h])).astype(BF16), sb_ref[sub, h]) for h in hs]
        kzt = [(k[h].astype(F32) * jnp.exp((c - 1.0 - pos) * lgf[h])).T.astype(BF16) for h in hs]
        state = [jnp.exp(c * lgf[h][:, :1]) * state[h] + _dot(kzt[h], v[h]) for h in hs]
        xc = [ret[h] - jnp.mean(ret[h], axis=-1, keepdims=True) for h in hs]
        y = [xc[h] * lax.rsqrt(jnp.mean(xc[h] * xc[h], axis=-1, keepdims=True) + NORM_EPS) for h in hs]
        for h in hs:
            sl = slice(h * RET_DV, (h + 1) * RET_DV)
            o_ref[rows, sl] = (g_ref[rows, sl].astype(F32) * y[h]).astype(BF16)
    for h in hs:
        s_scr[h] = state[h]


def _retention(qa, ka, va, ga, dec_f, dec_b):
    tt = qa.shape[0]
    r = RET_CHUNKS_PER_STEP
    rows = r * RET_CHUNK
    assert tt % rows == 0 and CTX_LEN % rows == 0
    n = tt // rows
    dec = jnp.broadcast_to(jnp.concatenate([dec_f, dec_b]).astype(F32)[:, None], (2 * RET_HEADS, V7X_LANES))
    kw, vw = RET_HEADS * RET_DK, BRANCH_W
    dec_spec = pl.BlockSpec((2 * RET_HEADS, V7X_LANES), lambda t: (0, 0))
    state_shape = (RET_HEADS, RET_DK, RET_DV)
    sb = pl.pallas_call(
        _ret_state_kernel,
        grid=(n,),
        in_specs=[dec_spec,
                  pl.BlockSpec((rows, kw), lambda t: (_ret_bwd_block(t, n), 0)),
                  pl.BlockSpec((rows, vw), lambda t: (_ret_bwd_block(t, n), 0))],
        out_specs=pl.BlockSpec((r,) + state_shape, lambda t: (_ret_bwd_block(t, n), 0, 0, 0)),
        out_shape=jax.ShapeDtypeStruct((n * r,) + state_shape, BF16),
        scratch_shapes=[pltpu.VMEM(state_shape, F32)],
        compiler_params=_cparams(("arbitrary",), 32 * 1024 * 1024),
        name="retention_state",
    )(dec, ka, va)
    return pl.pallas_call(
        _ret_out_kernel,
        grid=(n,),
        in_specs=[dec_spec,
                  pl.BlockSpec((rows, kw), lambda t: (t, 0)),
                  pl.BlockSpec((rows, kw), lambda t: (t, 0)),
                  pl.BlockSpec((rows, vw), lambda t: (t, 0)),
                  pl.BlockSpec((rows, vw), lambda t: (t, 0)),
                  pl.BlockSpec((r,) + state_shape, lambda t: (t, 0, 0, 0))],
        out_specs=pl.BlockSpec((rows, vw), lambda t: (t, 0)),
        out_shape=jax.ShapeDtypeStruct((tt, vw), BF16),
        scratch_shapes=[pltpu.VMEM(state_shape, F32)],
        compiler_params=_cparams(("arbitrary",), 32 * 1024 * 1024),
        name="retention_out",
    )(dec, qa, ka, va, ga, sb)


def _gqa_kernel(q_ref, kt_ref, v_ref, g_ref, o_ref, s0_scr, s1_scr, *, kv_block, n_kv):
    tq = q_ref.shape[0]
    heads = [slice(h * GQA_DIM, (h + 1) * GQA_DIM) for h in range(GQA_GROUP)]
    q = jnp.concatenate([q_ref[:, sl] for sl in heads], axis=0)
    rows = GQA_GROUP * tq

    def scores(j, s_ref):
        start = pl.multiple_of(j * kv_block, V7X_LANES)
        s_ref[...] = _dot(q, kt_ref[:, pl.ds(start, kv_block)])

    def softmax_pv(j, s_ref, carry):
        m, acc = carry
        start = pl.multiple_of(j * kv_block, V7X_LANES)
        s = s_ref[...]
        m_new = jnp.maximum(m, jnp.max(s, axis=-1, keepdims=True))
        p = jnp.exp2(s - m_new).astype(BF16)
        acc = jnp.exp2(m - m_new) * acc + _dot(p, v_ref[pl.ds(start, kv_block), :])
        return m_new, acc

    def body(i, carry):
        j = 2 * i
        scores(j + 1, s1_scr)
        carry = softmax_pv(j, s0_scr, carry)
        scores(j + 2, s0_scr)
        return softmax_pv(j + 1, s1_scr, carry)

    carry = (jnp.full((rows, 1), MASK_VALUE, F32), jnp.zeros((rows, 2 * GQA_DIM), F32))
    scores(0, s0_scr)
    carry = lax.fori_loop(0, (n_kv - 1) // 2, body, carry, unroll=True)
    if n_kv % 2 == 0:
        scores(n_kv - 1, s1_scr)
        carry = softmax_pv(n_kv - 2, s0_scr, carry)
        _, acc = softmax_pv(n_kv - 1, s1_scr, carry)
    else:
        _, acc = softmax_pv(n_kv - 1, s0_scr, carry)
    out = acc[:, :GQA_DIM] / acc[:, GQA_DIM:]
    for h, sl in enumerate(heads):
        o_ref[:, sl] = (g_ref[:, sl].astype(F32) * out[h * tq:(h + 1) * tq]).astype(BF16)


def _gqa_attention(qb, kbt, vb, gb, *, q_row0, n_q, n_keys, kv_block, out_rows):
    gw = GQA_GROUP * GQA_DIM
    tq = ATT_Q_BLOCK
    qoff = q_row0 // tq
    return pl.pallas_call(
        functools.partial(_gqa_kernel, kv_block=kv_block, n_kv=n_keys // kv_block),
        grid=(GQA_KV_HEADS, n_q // tq),
        in_specs=[pl.BlockSpec((tq, gw), lambda g, i: (qoff + i, g)),
                  pl.BlockSpec((GQA_DIM, n_keys), lambda g, i: (g, 0)),
                  pl.BlockSpec((n_keys, 2 * GQA_DIM), lambda g, i: (0, g)),
                  pl.BlockSpec((tq, gw), lambda g, i: (qoff + i, g))],
        out_specs=pl.BlockSpec((tq, gw), lambda g, i: (i, g)),
        out_shape=jax.ShapeDtypeStruct((out_rows, GQA_HEADS * GQA_DIM), BF16),
        scratch_shapes=[pltpu.VMEM((GQA_GROUP * tq, kv_block), F32)] * 2,
        compiler_params=_cparams(("arbitrary", "arbitrary"), 56 * 1024 * 1024),
        name="gqa_attention",
    )(qb, kbt, vb, gb)


def _out_proj_kernel(m1_ref, m2c_ref, m2l_ref, w_ref, resc_ref, resl_ref, mod_ref, gpost_ref, o_ref):
    d = o_ref.shape[-1]
    is_ctx = pl.program_id(0) == 0
    m2 = jnp.where(is_ctx, m2c_ref[...], m2l_ref[...])
    y = _dot(m1_ref[...], w_ref[:BRANCH_W, :]) + _dot(m2, w_ref[BRANCH_W:, :])
    mod = mod_ref[0]
    gate = jnp.where(is_ctx, mod[1:2, 2 * d:], mod[0:1, 2 * d:])
    res = jnp.where(is_ctx, resc_ref[...], resl_ref[...])
    o_ref[...] = res + gate * (_rms_rows(y) * gpost_ref[...])


def _out_proj(m1, m2_ctx, m2_lat, w_out, res_ctx, res_lat, mod, layer, g_post):
    tt = m1.shape[0]
    d = res_lat.shape[1]
    tm = ROW_BLOCK
    row = lambda w: pl.BlockSpec((tm, w), lambda i: (i, 0))
    lat = lambda w: pl.BlockSpec((tm, w), lambda i: (jnp.maximum(i - 1, 0), 0))
    ctx = lambda w: pl.BlockSpec((CTX_LEN, w), lambda i: (0, 0))
    return pl.pallas_call(
        _out_proj_kernel,
        grid=(tt // tm,),
        in_specs=[row(BRANCH_W), ctx(BRANCH_W), lat(BRANCH_W),
                  pl.BlockSpec((2 * BRANCH_W, d), lambda i: (0, 0)),
                  ctx(d), lat(d),
                  pl.BlockSpec((1, V7X_SUBLANES, 3 * d), lambda i: (layer, 0, 0)),
                  pl.BlockSpec((1, d), lambda i: (0, 0))],
        out_specs=row(d),
        out_shape=jax.ShapeDtypeStruct((tt, d), F32),
        compiler_params=_cparams(("arbitrary",), 40 * 1024 * 1024),
        name="out_proj",
    )(m1, m2_ctx, m2_lat, w_out, res_ctx, res_lat, mod, g_post.reshape(1, d))


def _odd_in_kernel(s_ref, mod_ref, gpre_ref, w_ref, q_ref, k_ref, v_ref, rw_ref, gc_ref, gd_ref):
    is_ctx = pl.program_id(0) == 0
    hb = _adaln(s_ref[...], mod_ref[0], is_ctx, gpre_ref[...]).astype(BF16)
    o = 0

    def seg(width):
        nonlocal o
        y = _dot(hb, w_ref[:, o:o + width])
        o += width
        return y

    q_ref[...] = (seg(BRANCH_W) * NA_DIM ** -0.5).astype(BF16)
    k_ref[...] = seg(BRANCH_W).astype(BF16)
    v_ref[...] = seg(BRANCH_W).astype(BF16)
    rw_ref[...] = seg(SHIFT_W)
    gc_ref[...] = _silu(seg(BRANCH_W)).astype(BF16)
    gd_ref[...] = _silu(seg(BRANCH_W)).astype(BF16)


def _odd_in_proj(stream, mod, g_pre, w_in):
    tt, d = stream.shape
    tm = ROW_BLOCK
    row = lambda w: pl.BlockSpec((tm, w), lambda i: (i, 0))
    widths = [(BRANCH_W, BF16), (BRANCH_W, BF16), (BRANCH_W, BF16), (SHIFT_W, F32), (BRANCH_W, BF16), (BRANCH_W, BF16)]
    return pl.pallas_call(
        _odd_in_kernel,
        grid=(tt // tm,),
        in_specs=[row(d),
                  pl.BlockSpec((1, V7X_SUBLANES, 3 * d), lambda i: (1, 0, 0)),
                  pl.BlockSpec((1, d), lambda i: (0, 0)),
                  pl.BlockSpec((d, ODD_IN), lambda i: (0, 0), pipeline_mode=pl.Buffered(1))],
        out_specs=[row(w) for w, _ in widths],
        out_shape=[jax.ShapeDtypeStruct((tt, w), dt) for w, dt in widths],
        compiler_params=_cparams(("arbitrary",), d * ODD_IN * 2 + 24 * 1024 * 1024),
        name="odd_in_proj",
    )(stream, mod, g_pre.reshape(1, d), w_in)


def _na_kernel(*refs, rows):
    nb = NA_BLOCKS_PER_STEP
    q_refs, g_refs = refs[:nb], refs[nb:2 * nb]
    k_ref, v_ref, bias_ref, o_ref = refs[2 * nb:]
    span = NA_UNION_ROWS * GRID_W
    bq = NA_BLOCK_ROWS * GRID_W
    first = lax.broadcasted_iota(jnp.int32, (bq, 2 * NA_DIM), 1) < NA_DIM
    kc, vc = k_ref[:CTX_LEN, :], v_ref[:CTX_LEN, :]
    blocks = range(nb)
    lhs, ku, vu, cls = [], [], [], []
    for b in blocks:
        qr0 = (pl.program_id(1) * nb + b) * NA_BLOCK_ROWS
        u0 = jnp.clip(qr0 - NA_WIN_ROWS // 2, 0, rows - NA_UNION_ROWS)
        cls.append(jnp.where(qr0 == 0, 0, jnp.where(qr0 == rows - NA_BLOCK_ROWS, 2, 1)))
        start = pl.multiple_of(CTX_LEN + u0 * GRID_W, GRID_W)
        ku.append(k_ref[pl.ds(start, span), :])
        vu.append(v_ref[pl.ds(start, span), :])
        q = q_refs[b][...]
        zero = jnp.zeros_like(q)
        lhs.append(jnp.concatenate([jnp.where(first, q, zero), jnp.where(first, zero, q)], axis=0))
    bias = [jnp.concatenate([bias_ref[0, cls[b]], bias_ref[1, cls[b]]], axis=0) for b in blocks]
    sw = [_dot_nt(lhs[b], ku[b]) + bias[b] for b in blocks]
    sc = [_dot_nt(lhs[b], kc) for b in blocks]
    m = [jnp.maximum(jnp.max(sw[b], axis=-1, keepdims=True), jnp.max(sc[b], axis=-1, keepdims=True)) for b in blocks]
    pw = [jnp.exp(sw[b] - m[b]) for b in blocks]
    pc = [jnp.exp(sc[b] - m[b]) for b in blocks]
    l = [jnp.sum(pw[b], axis=-1, keepdims=True) + jnp.sum(pc[b], axis=-1, keepdims=True) for b in blocks]
    o = [(_dot(pw[b].astype(BF16), vu[b]) + _dot(pc[b].astype(BF16), vc)) / l[b] for b in blocks]
    for b in blocks:
        out = jnp.where(first, o[b][:bq], o[b][bq:])
        o_ref[b * bq:(b + 1) * bq, :] = (g_refs[b][...].astype(F32) * out).astype(BF16)


def _na_bias_table(rpb):
    half = NA_WIN_ROWS // 2
    cols = np.arange(GRID_W)
    c0 = np.clip(cols - NA_WIN_COLS // 2, 0, GRID_W - NA_WIN_COLS)
    valid_c = (cols[None, :] >= c0[:, None]) & (cols[None, :] < c0[:, None] + NA_WIN_COLS)
    dc = np.clip(cols[None, :] - cols[:, None] + NA_WIN_COLS - 1, 0, 2 * NA_WIN_COLS - 2)
    a = np.arange(NA_BLOCK_ROWS)[:, None]
    i = np.arange(NA_UNION_ROWS)[None, :]
    dr = np.stack([i - a, i - a - half, i - a - (NA_UNION_ROWS - NA_BLOCK_ROWS)])
    w0 = np.stack([0 * a, a, (NA_UNION_ROWS - NA_WIN_ROWS) + 0 * a])
    valid_r = (i[None] >= w0) & (i[None] < w0 + NA_WIN_ROWS)
    onehot = (np.arange(2 * NA_WIN_COLS - 1)[:, None, None] == dc[None]) & valid_c[None]
    tiles = jnp.einsum('hrd,dqk->hrqk', rpb.astype(F32), jnp.asarray(onehot, F32), precision=lax.Precision.HIGHEST)
    tiles = jnp.where(jnp.asarray(valid_c)[None, None], tiles, MASK_VALUE)
    masked_tile = jnp.full((NA_HEADS, GRID_W, GRID_W), MASK_VALUE, F32)

    def tile(c, qa, ki):
        return tiles[:, dr[c, qa, ki] + NA_WIN_ROWS - 1] if valid_r[c, qa, ki] else masked_tile

    return jnp.stack([jnp.concatenate([jnp.concatenate([tile(c, qa, ki) for ki in range(NA_UNION_ROWS)], axis=-1)
                                       for qa in range(NA_BLOCK_ROWS)], axis=-2) for c in range(3)], axis=1)


def _neighbourhood_attention(q, k, v, gc, rpb):
    tt = q.shape[0]
    t = tt - CTX_LEN
    rows = t // GRID_W
    nb = NA_BLOCKS_PER_STEP
    assert NA_BLOCK_ROWS == NA_WIN_ROWS // 2 and rows >= NA_UNION_ROWS and rows % (nb * NA_BLOCK_ROWS) == 0
    bias = _na_bias_table(rpb)
    pw = 2 * NA_DIM
    bq = NA_BLOCK_ROWS * GRID_W
    qoff = CTX_LEN // bq
    assert CTX_LEN % bq == 0
    blk = lambda b: pl.BlockSpec((bq, pw), lambda p, i: (qoff + i * nb + b, p))
    resident = pl.BlockSpec((tt, pw), lambda p, i: (0, p))
    return pl.pallas_call(
        functools.partial(_na_kernel, rows=rows),
        grid=(NA_HEADS // 2, rows // (nb * NA_BLOCK_ROWS)),
        in_specs=[blk(b) for b in range(nb)] + [blk(b) for b in range(nb)] + [
            resident, resident,
            pl.BlockSpec((2,) + bias.shape[1:], lambda p, i: (p, 0, 0, 0))],
        out_specs=pl.BlockSpec((nb * bq, pw), lambda p, i: (i, p)),
        out_shape=jax.ShapeDtypeStruct((t, BRANCH_W), BF16),
        compiler_params=_cparams(("arbitrary", "arbitrary"), 48 * 1024 * 1024),
        name="neighbourhood_attention",
    )(*([q] * nb + [gc] * nb + [k, v, bias]))


def _rwkv_block_index(d, t, nblk):
    nctx = CTX_LEN // (RWKV_CHUNKS_PER_STEP * RWKV_CHUNK)
    return t if d == 0 else jnp.where(t < nctx, nctx - 1 - t, nblk - 1 - (t - nctx))


def _rwkv_prepare(d, c, nch, p, prev_row, next_row, mu_ref, w0_ref, w2_ref, a0_ref, a2_ref, kk_ref, ka_ref,
                  rk_ref, e_ref, et_ref, bon_ref, out_rows):
    lc = RWKV_CHUNK
    nctx = CTX_LEN // lc
    sgn = 1 - 2 * d
    row = lax.broadcasted_iota(jnp.int32, p.shape, 0)
    first_zero = jnp.logical_or(c == 0, c == nctx)
    last_zero = jnp.logical_or(c == nctx - 1, c == nch - 1)
    pr = jnp.where(first_zero, 0.0, prev_row)
    nx = jnp.where(last_zero, 0.0, next_row)
    prev = jnp.where(row == 0, pr, pltpu.roll(p, 1, 0))
    nxt = jnp.where(row == lc - 1, nx, pltpu.roll(p, lc - 1, 0))
    z = p + (0.5 * (prev + nxt) - p) * mu_ref[...]

    bw = BRANCH_W
    r, k, v = z[:, :bw], z[:, bw:2 * bw], z[:, 2 * bw:3 * bw]
    zw = z[:, 3 * bw:3 * bw + 2 * DECAY_LORA]
    za = z[:, 3 * bw + 2 * DECAY_LORA:]
    lw = w0_ref[d] + _dot(jnp.tanh(zw).astype(BF16), w2_ref[d])
    ld = -float(np.exp(-0.5)) * _sigmoid(lw)
    asig = _sigmoid(a0_ref[d] + _dot(za.astype(BF16), a2_ref[d]))
    kk = k * kk_ref[...]
    kd = k * (1.0 + (asig - 1.0) * ka_ref[...])
    kk_sq, rkd = _head_sums([kk * kk, r * kd * rk_ref[...]], e_ref[...], et_ref[...])
    kkn = kk / jnp.maximum(jnp.sqrt(kk_sq), 1e-12)
    a_vec = -kkn
    b_vec = kkn * asig
    bon_ref[out_rows, :] = rkd * v

    ti = lax.broadcasted_iota(jnp.int32, (lc, lc), 0)
    si = lax.broadcasted_iota(jnp.int32, (lc, lc), 1)
    cum = _dot_split_rhs(jnp.where(sgn * (ti - si) >= 0, 1.0, 0.0).astype(BF16), ld)
    tot = jnp.sum(ld, axis=0, keepdims=True)
    rem = jnp.exp(tot - cum)
    pinv = jnp.exp(-cum)
    rt = r * jnp.exp(cum)
    kt = kd * pinv
    bt = b_vec * pinv
    at = a_vec * jnp.exp(cum - ld)
    bh = b_vec * rem
    kh = kd * rem
    pend = jnp.exp(tot)
    return dict(at=at, rt=rt, bt=bt, kt=kt, bh=bh, kh=kh, v=v, pend=pend)


def _rwkv_kernel(main_f, prev_f, next_f, main_b, prev_b, next_b, mu_ref, w0_ref, w2_ref, a0_ref, a2_ref, kk_ref,
                 ka_ref, rk_ref, e_ref, et_ref, wkv_f, wkv_b, bon_f, bon_b, s_scr, *, nch):
    lc = RWKV_CHUNK
    hd = RWKV_DIM
    t = pl.program_id(0)

    @pl.when(t == 0)
    def _():
        s_scr[...] = jnp.zeros_like(s_scr)

    shared = (mu_ref, w0_ref, w2_ref, a0_ref, a2_ref, kk_ref, ka_ref, rk_ref, e_ref, et_ref)
    nsub = RWKV_CHUNKS_PER_STEP
    blocks = ((main_f, prev_f, next_f, bon_f), (main_b, prev_b, next_b, bon_b))
    wkv_refs = (wkv_f, wkv_b)

    def sub_rows(k, d):
        sub = k if d == 0 else nsub - 1 - k
        return sub, slice(sub * lc, (sub + 1) * lc)

    rows = {}
    for k in range(nsub):
        for d, (main, prev, nxt, bon) in enumerate(blocks):
            sub, rs = sub_rows(k, d)
            c = _rwkv_block_index(d, t, nch // nsub) * nsub + sub
            prev_row = main[sub * lc - 1:sub * lc, :] if sub > 0 else prev[V7X_SUBLANES - 1:V7X_SUBLANES, :]
            next_row = main[(sub + 1) * lc:(sub + 1) * lc + 1, :] if sub < nsub - 1 else nxt[0:1, :]
            rows[k, d] = _rwkv_prepare(d, c, nch, main[rs, :], prev_row, next_row, *shared, bon, rs)

    pw = 2 * hd
    lane = lax.broadcasted_iota(jnp.int32, (lc, pw), 1)
    tok = lax.broadcasted_iota(jnp.int32, (lc, pw), 0)
    first = lane < hd
    fwd_diff = tok - (lane & (hd - 1))
    eye = (fwd_diff == 0).astype(F32)
    rr = lax.broadcasted_iota(jnp.int32, (pw, pw), 0)
    cc = lax.broadcasted_iota(jnp.int32, (pw, pw), 1)
    same_head = (rr < hd) == (cc < hd)
    probs = [(k, d, p) for k in range(nsub) for d in range(2) for p in range(RWKV_HEADS // 2)]
    n = range(len(probs))
    incl = [(fwd_diff >= 0) if d == 0 else (fwd_diff <= 0) for _, d, _ in probs]
    strict = [(fwd_diff > 0) if d == 0 else (fwd_diff < 0) for _, d, _ in probs]

    def bdiag(m):
        m = m.astype(BF16)
        zero = jnp.zeros_like(m)
        return jnp.concatenate([jnp.where(first, m, zero), jnp.where(first, zero, m)], axis=0)

    def part(name):
        return [rows[k, d][name][:, p * pw:(p + 1) * pw] for k, d, p in probs]

    at_p, rt_p, v_p = part("at"), part("rt"), part("v")
    bt_p, kt_p, pend_p = part("bt"), part("kt"), part("pend")
    bh_p = [m.astype(BF16) for m in part("bh")]
    kh_p = [m.astype(BF16) for m in part("kh")]
    lhs = [jnp.concatenate([at_p[i], rt_p[i]], axis=0).astype(BF16) for i in n]
    gram = [_dot_nt(lhs[i], jnp.concatenate([bdiag(bt_p[i]), bdiag(kt_p[i])], axis=0)) for i in n]
    gb = [g[:, :pw] for g in gram]
    gk = [g[:, pw:] for g in gram]
    aab = [jnp.where(strict[i], gb[i][:lc], 0.0) for i in n]
    arb = [jnp.where(incl[i], gb[i][lc:], 0.0) for i in n]
    aak = [jnp.where(strict[i], gk[i][:lc], 0.0) for i in n]
    ark = [jnp.where(incl[i], gk[i][lc:], 0.0) for i in n]
    vbd = [bdiag(m) for m in v_p]
    aakv = [_dot(aak[i].astype(BF16), vbd[i]) for i in n]
    s_tok = lane & (hd - 1)

    def same_block(size):
        shift = int(np.log2(size))
        return (tok >> shift) == (s_tok >> shift)

    nd = [jnp.where(same_block(RWKV_INV_BASE), m, 0.0) for m in aab]
    x = [eye + m for m in nd]
    nk = [_dot(m.astype(BF16), bdiag(m)) for m in nd]
    for _ in range(int(np.log2(RWKV_INV_BASE)) - 2):
        out = [_dot(jnp.concatenate([nk[i], x[i]], axis=0).astype(BF16), bdiag(nk[i])) for i in n]
        nk = [o[:lc] for o in out]
        x = [x[i] + out[i][lc:] for i in n]
    x = [x[i] + _dot(x[i].astype(BF16), bdiag(nk[i])) for i in n]
    size = RWKV_INV_BASE
    while size < lc:
        couple = jnp.logical_and(same_block(2 * size), jnp.logical_not(same_block(size)))
        xn = [_dot(x[i].astype(BF16), bdiag(jnp.where(couple, aab[i], 0.0))) for i in n]
        x = [x[i] + _dot(xn[i].astype(BF16), bdiag(x[i])) for i in n]
        size *= 2
    xc = [_dot(x[i].astype(BF16), jnp.concatenate([bdiag(at_p[i]), bdiag(aakv[i])], axis=1)) for i in n]
    ahat = [m[:, :pw] for m in xc]
    wmat = [m[:, pw:] for m in xc]
    rhat = [rt_p[i] + _dot(arb[i].astype(BF16), bdiag(ahat[i])) for i in n]
    y0 = [_dot(jnp.concatenate([arb[i], ark[i]], axis=1).astype(BF16),
               jnp.concatenate([bdiag(wmat[i]), vbd[i]], axis=0)) for i in n]
    mab = [jnp.where(same_head, _dot(ahat[i].T.astype(BF16), bh_p[i]), 0.0).astype(BF16) for i in n]
    gtf = [_dot(jnp.concatenate([wmat[i].T, v_p[i].T], axis=1).astype(BF16),
                jnp.concatenate([bh_p[i], kh_p[i]], axis=0)) for i in n]
    for i, (k, d, p) in enumerate(probs):
        s0 = s_scr[d, p]
        wkv_refs[d][sub_rows(k, d)[1], p * pw:(p + 1) * pw] = _dot_nt(rhat[i].astype(BF16), bdiag(s0)) + y0[i]
        s_scr[d, p] = (s0 * pend_p[i] + _dot(s0.astype(BF16), mab[i])
                       + jnp.where(first, gtf[i][:lc], gtf[i][lc:]))


def _head_indicator(width, head_dim):
    idx = np.arange(width) // head_dim
    return jnp.asarray(idx[:, None] == np.arange(V7X_LANES)[None, :], BF16)


def _rwkv(rw_p, mu, w0s, w2s, a0s, a2s, k_k, k_a, r_k):
    tt, w = rw_p.shape
    lc = RWKV_CHUNK
    nch = tt // lc
    bw = BRANCH_W
    sub = V7X_SUBLANES
    zeros = jnp.zeros((DECAY_LORA, bw), F32)
    w2p = jnp.stack([jnp.concatenate([w2s[0], zeros]), jnp.concatenate([zeros, w2s[1]])]).astype(BF16)
    a2p = jnp.stack([jnp.concatenate([a2s[0], zeros]), jnp.concatenate([zeros, a2s[1]])]).astype(BF16)
    w0 = jnp.stack(w0s).reshape(2, 1, bw)
    a0 = jnp.stack(a0s).reshape(2, 1, bw)
    e_mat = _head_indicator(bw, RWKV_DIM)
    const = lambda *shape: pl.BlockSpec(shape, lambda t: tuple(0 for _ in shape))

    rows = RWKV_CHUNKS_PER_STEP * lc
    nblk = tt // rows
    assert tt % rows == 0 and CTX_LEN % rows == 0

    def block_specs(d):
        bidx = lambda t: _rwkv_block_index(d, t, nblk)
        return [pl.BlockSpec((rows, w), lambda t: (bidx(t), 0)),
                pl.BlockSpec((sub, w), lambda t: (jnp.maximum(bidx(t) * (rows // sub) - 1, 0), 0)),
                pl.BlockSpec((sub, w), lambda t: (jnp.minimum((bidx(t) + 1) * (rows // sub), tt // sub - 1), 0))]

    out_spec = lambda d: pl.BlockSpec((rows, bw), lambda t: (_rwkv_block_index(d, t, nblk), 0))
    return pl.pallas_call(
        functools.partial(_rwkv_kernel, nch=nch),
        grid=(nblk,),
        in_specs=block_specs(0) + block_specs(1) + [
            const(1, w), const(2, 1, bw), const(2, 2 * DECAY_LORA, bw), const(2, 1, bw), const(2, 2 * ICLR_LORA, bw),
            const(1, bw), const(1, bw), const(1, bw), const(bw, V7X_LANES), const(V7X_LANES, bw)],
        out_specs=[out_spec(0), out_spec(1), out_spec(0), out_spec(1)],
        out_shape=[jax.ShapeDtypeStruct((tt, bw), F32)] * 4,
        scratch_shapes=[pltpu.VMEM((2, RWKV_HEADS // 2, RWKV_DIM, 2 * RWKV_DIM), F32)],
        compiler_params=_cparams(("arbitrary",), 48 * 1024 * 1024),
        name="rwkv7",
    )(rw_p, rw_p, rw_p, rw_p, rw_p, rw_p, mu.reshape(1, w), w0, w2p, a0, a2p,
      k_k.reshape(1, bw), k_a.reshape(1, bw), r_k.reshape(1, bw), e_mat, e_mat.T)


def _odd_out_kernel(na_ref, wkvf_ref, wkvb_ref, bonf_ref, bonb_ref, gd_ref, lng_ref, lnb_ref, e_ref, et_ref, w_ref,
                    res_ref, mod_ref, gpost_ref, o_ref):
    d = o_ref.shape[-1]
    e, et = e_ref[...], et_ref[...]
    inv = 1.0 / RWKV_DIM
    nrow = o_ref.shape[0] // ODD_OUT_ROW_GROUPS
    groups = [slice(g * nrow, (g + 1) * nrow) for g in range(ODD_OUT_ROW_GROUPS)]
    wkv = [wkvf_ref[g, :] + wkvb_ref[g, :] for g in groups]
    mean = [_head_sums([x], e, et)[0] * inv for x in wkv]
    xc = [x - mu for x, mu in zip(wkv, mean)]
    var = [_head_sums([x * x], e, et)[0] * inv for x in xc]
    y = [xc[i] * lax.rsqrt(var[i] + RWKV_GN_EPS) * lng_ref[...] + lnb_ref[...] + bonf_ref[g, :] + bonb_ref[g, :]
         for i, g in enumerate(groups)]
    m2 = jnp.concatenate([(gd_ref[g, :].astype(F32) * y[i]).astype(BF16) for i, g in enumerate(groups)], axis=0)
    out = _dot(na_ref[...], w_ref[:BRANCH_W, :]) + _dot(m2, w_ref[BRANCH_W:, :])
    gate = mod_ref[0][0:1, 2 * d:]
    o_ref[...] = res_ref[...] + gate * (_rms_rows(out) * gpost_ref[...])


def _odd_out(na_g, wkv_f, wkv_b, bon_f, bon_b, gd, lnx_g, lnx_b, w_out, stream, mod, g_post):
    t, bw = na_g.shape
    d = stream.shape[1]
    tm = ROW_BLOCK
    off = CTX_LEN // tm
    lat = lambda w: pl.BlockSpec((tm, w), lambda i: (off + i, 0))
    vec = lambda w: pl.BlockSpec((1, w), lambda i: (0, 0))
    e_mat = _head_indicator(bw, RWKV_DIM)
    return pl.pallas_call(
        _odd_out_kernel,
        grid=(t // tm,),
        in_specs=[pl.BlockSpec((tm, bw), lambda i: (i, 0)), lat(bw), lat(bw), lat(bw), lat(bw), lat(bw),
                  vec(bw), vec(bw),
                  pl.BlockSpec((bw, V7X_LANES), lambda i: (0, 0)),
                  pl.BlockSpec((V7X_LANES, bw), lambda i: (0, 0)),
                  pl.BlockSpec((2 * bw, d), lambda i: (0, 0)),
                  lat(d),
                  pl.BlockSpec((1, V7X_SUBLANES, 3 * d), lambda i: (1, 0, 0)),
                  vec(d)],
        out_specs=pl.BlockSpec((tm, d), lambda i: (i, 0)),
        out_shape=jax.ShapeDtypeStruct((t, d), F32),
        compiler_params=_cparams(("arbitrary",), 48 * 1024 * 1024),
        name="odd_out_proj",
    )(na_g, wkv_f, wkv_b, bon_f, bon_b, gd, lnx_g.reshape(1, bw), lnx_b.reshape(1, bw), e_mat, e_mat.T, w_out,
      stream, mod, g_post.reshape(1, d))


def _rope_tables(t):
    n_rows = t // GRID_W
    n_freq = RET_DK // 4
    inv = ROPE_THETA ** (-jnp.arange(n_freq, dtype=F32) / n_freq)
    row_ang = jnp.arange(n_rows, dtype=F32)[:, None] * inv
    col_ang = jnp.arange(GRID_W, dtype=F32)[:, None] * inv

    def table(fn):
        rows_part = jnp.broadcast_to(fn(row_ang)[:, None, :], (n_rows, GRID_W, n_freq))
        cols_part = jnp.broadcast_to(fn(col_ang)[None, :, :], (n_rows, GRID_W, n_freq))
        return jnp.repeat(jnp.concatenate([rows_part, cols_part], axis=-1).reshape(t, 2 * n_freq), 2, axis=-1)

    cos = table(jnp.cos)
    sin = table(jnp.sin) * jnp.tile(jnp.asarray([-1.0, 1.0], F32), RET_DK // 2)
    cos = jnp.concatenate([jnp.ones((CTX_LEN, RET_DK), F32), cos], axis=0)
    sin = jnp.concatenate([jnp.zeros((CTX_LEN, RET_DK), F32), sin], axis=0)
    return cos, sin


def kernel(x, c, ctx, c_ctx, w_mod, b_mod, g_pre, g_post, ev_w_in, ev_w_out, ret_decay_fwd, ret_decay_bwd, gqa_q_norm, gqa_k_norm, od_w_in, od_w_out, na_rpb, rwkv_shift_mu, rwkv_w0_fwd, rwkv_w2_fwd, rwkv_w0_bwd, rwkv_w2_bwd, rwkv_a0_fwd, rwkv_a2_fwd, rwkv_a0_bwd, rwkv_a2_bwd, rwkv_k_k, rwkv_k_a, rwkv_r_k, rwkv_lnx_g, rwkv_lnx_b):
    assert x.shape[0] == 1 and DEPTH == 2 and RET_DK == GQA_DIM
    t = x.shape[1]
    tt = t + CTX_LEN
    assert t % ROW_BLOCK == 0 and tt % ATT_KV_BLOCK == 0 and t % GRID_W == 0
    x2, ctx2 = x[0], ctx[0]
    mod = _modulation(c, c_ctx, w_mod, b_mod)
    cos_t, sin_t = _rope_tables(t)

    qa, ka, va, qb, kbt, vb, ga, gb = _even_in_proj(x2, ctx2, mod, g_pre[0], ev_w_in[0].astype(BF16), cos_t, sin_t,
                                                    gqa_q_norm[0], gqa_k_norm[0])
    ret_g = _retention(qa, ka, va, ga, ret_decay_fwd[0], ret_decay_bwd[0])
    att_lat = _gqa_attention(qb, kbt, vb, gb, q_row0=CTX_LEN, n_q=t, n_keys=tt, kv_block=ATT_KV_BLOCK, out_rows=t)
    att_ctx = _gqa_attention(qb, kbt, vb, gb, q_row0=0, n_q=CTX_LEN, n_keys=CTX_LEN, kv_block=CTX_LEN,
                             out_rows=CTX_LEN)
    stream1 = _out_proj(ret_g, att_ctx, att_lat, ev_w_out[0].astype(BF16), ctx2, x2, mod, 0, g_post[0])

    q, k, v, rw_p, gc, gd = _odd_in_proj(stream1, mod, g_pre[1], od_w_in[0].astype(BF16))
    na_g = _neighbourhood_attention(q, k, v, gc, na_rpb[0])
    rw = _rwkv(rw_p, rwkv_shift_mu[0], (rwkv_w0_fwd[0], rwkv_w0_bwd[0]), (rwkv_w2_fwd[0], rwkv_w2_bwd[0]),
               (rwkv_a0_fwd[0], rwkv_a0_bwd[0]), (rwkv_a2_fwd[0], rwkv_a2_bwd[0]),
               rwkv_k_k[0], rwkv_k_a[0], rwkv_r_k[0].reshape(-1))
    out = _odd_out(na_g, *rw, gd, rwkv_lnx_g[0], rwkv_lnx_b[0], od_w_out[0].astype(BF16), stream1, mod, g_post[1])
    return out[None]
```
